```python
import math
import jax, jax.numpy as jnp
from jax import lax
import numpy as np

D_MODEL = 1024
BATCH = 2
SEQ = 8192
DEPTH = 1

GDN_HEADS = 4
GDN_DK = 128
GDN_DV = 128
GDN_QK = GDN_HEADS * GDN_DK
GDN_WIDTH = GDN_HEADS * GDN_DV
GDN_CHUNK = 64
CONV_K = 4
DIFF_HEADS = 4
DIFF_DH = 64
DIFF_DV = 2 * DIFF_DH
DIFF_QK = DIFF_HEADS * 2 * DIFF_DH
DIFF_WIDTH = DIFF_HEADS * DIFF_DV
Q_BLOCK = 128
SPLITS = (GDN_QK, GDN_QK, GDN_WIDTH, GDN_HEADS, GDN_HEADS, GDN_WIDTH,
          DIFF_QK, DIFF_QK, DIFF_WIDTH, DIFF_WIDTH, D_MODEL, D_MODEL)
IN_WIDTH = sum(SPLITS)
NORM_EPS = 1e-6
SUBLN_EPS = 1e-5
LN_EPS = 1e-5
DEEPNORM_ALPHA = (2.0 * DEPTH) ** 0.25
DEEPNORM_BETA = (8.0 * DEPTH) ** -0.25

kernel_name = "hybrid_gdn_diffattn_deepnorm"


def _split_points():
    pts, acc = [], 0
    for s in SPLITS[:-1]:
        acc += s
        pts.append(acc)
    return pts


def _l2norm(t):
    return t * lax.rsqrt(jnp.sum(t * t, axis=-1, keepdims=True) + NORM_EPS)


def _rms_norm(t, w, eps):
    t = t.astype(jnp.float32)
    return t * lax.rsqrt(jnp.mean(t * t, axis=-1, keepdims=True) + eps) * w.astype(jnp.float32)


def _layer_norm(t, g, b):
    t32 = t.astype(jnp.float32)
    mu = jnp.mean(t32, axis=-1, keepdims=True)
    var = jnp.mean(jnp.square(t32 - mu), axis=-1, keepdims=True)
    out = (t32 - mu) * lax.rsqrt(var + LN_EPS) * g.astype(jnp.float32) + b.astype(jnp.float32)
    return out.astype(t.dtype)


def _causal_depthwise_conv(t, w):
    c = t.shape[-1]
    return lax.conv_general_dilated(
        t, w[:, None, :].astype(t.dtype), window_strides=(1,),
        padding=[(CONV_K - 1, 0)], dimension_numbers=("NWC", "WIO", "NWC"),
        feature_group_count=c)


def _alibi_slopes(n):
    return 2.0 ** (-8.0 * jnp.arange(1, n + 1, dtype=jnp.float32) / n)


def _gated_delta_rule(q, k, v, g, beta):
    B, S, H, DK = q.shape
    DV = v.shape[-1]
    C = GDN_CHUNK
    N = S // C
    q = q * (DK ** -0.5)

    def to_chunks(t):
        return jnp.moveaxis(t, 1, 2).reshape((B, H, N, C) + t.shape[3:])

    qc, kc, vc = to_chunks(q), to_chunks(k), to_chunks(v)
    gc = jnp.cumsum(to_chunks(g), axis=-1)
    bc = to_chunks(beta)
    causal = jnp.tril(jnp.ones((C, C), dtype=bool))
    strict = jnp.tril(jnp.ones((C, C), dtype=bool), -1)
    gdiff = gc[..., :, None] - gc[..., None, :]
    decay = jnp.where(causal, jnp.exp(jnp.where(causal, gdiff, 0.0)), 0.0)
    kb = kc * bc[..., None]
    lmat = jnp.where(strict, jnp.einsum("bhncd,bhnsd->bhncs", kb, kc) * decay, 0.0)
    tmat = lmat + jnp.eye(C, dtype=lmat.dtype)
    rhs = jnp.concatenate([vc * bc[..., None], kb * jnp.exp(gc)[..., None]], axis=-1)
    sol = lax.linalg.triangular_solve(tmat, rhs, left_side=True, lower=True,
                                      unit_diagonal=True)
    u, w = sol[..., :DV], sol[..., DV:]
    qk_intra = jnp.where(causal, jnp.einsum("bhncd,bhnsd->bhncs", qc, kc) * decay, 0.0)

    def step(state, inp):
        q_i, k_i, u_i, w_i, g_i, a_i = inp
        v_new = u_i - jnp.einsum("bhck,bhkv->bhcv", w_i, state)
        o = (jnp.einsum("bhck,bhkv->bhcv", q_i * jnp.exp(g_i)[..., None], state)
             + jnp.einsum("bhcs,bhsv->bhcv", a_i, v_new))
        g_last = g_i[..., -1]
        state = (state * jnp.exp(g_last)[..., None, None]
                 + jnp.einsum("bhck,bhcv->bhkv",
                              k_i * jnp.exp(g_last[..., None] - g_i)[..., None], v_new))
        return state, o

    xs = tuple(jnp.moveaxis(t, 2, 0) for t in (qc, kc, u, w, gc, qk_intra))
    state0 = jnp.zeros((B, H, DK, DV), dtype=jnp.float32)
    _, o = lax.scan(step, state0, xs)
    o = jnp.moveaxis(o, 0, 2).reshape(B, H, S, DV)
    return jnp.moveaxis(o, 1, 2)


def _diff_attention(q, k, v, lam, slopes):
    S = q.shape[1]
    q = q.astype(jnp.float32) * (DIFF_DH ** -0.5)
    k = k.astype(jnp.float32)
    v = v.astype(jnp.float32)
    outs = []
    for blk in range(S // Q_BLOCK):
        start = blk * Q_BLOCK
        end = start + Q_BLOCK
        qb, kb, vb = q[:, start:end], k[:, :end], v[:, :end]
        s = jnp.einsum("bqhmd,bkhmd->bhmqk", qb, kb)
        dist = (jnp.arange(start, end)[:, None] - jnp.arange(end)[None, :]).astype(jnp.float32)
        bias = jnp.where(dist[None] >= 0, -slopes[:, None, None] * dist[None], -jnp.inf)
        p = jax.nn.softmax(s + bias[None, :, None], axis=-1)
        attn = p[:, :, 0] - lam * p[:, :, 1]
        outs.append(jnp.einsum("bhqk,bkhv->bqhv", attn, vb))
    return jnp.concatenate(outs, axis=1)


def setup_inputs(seed: int = 0) -> dict:
    key = jax.random.key(seed)
    ks = jax.random.split(key, 17)
    f32 = jnp.float32
    col_scale = jnp.concatenate([
        jnp.full((s,), DEEPNORM_BETA if i in (2, 8) else 1.0, dtype=f32)
        for i, s in enumerate(SPLITS)])
    x = jax.random.normal(ks[0], (BATCH, SEQ, D_MODEL), f32)
    w_in = jax.random.normal(ks[1], (DEPTH, D_MODEL, IN_WIDTH), f32) * (D_MODEL ** -0.5) * col_scale
    conv_w = jax.random.normal(ks[2], (DEPTH, CONV_K, 2 * GDN_QK + GDN_WIDTH), f32) * (CONV_K ** -0.5)
    a_log = jnp.log(jax.random.uniform(ks[3], (DEPTH, GDN_HEADS), f32, 1.0, 16.0))
    dt = jnp.exp(jax.random.uniform(ks[4], (DEPTH, GDN_HEADS), f32, math.log(1e-3), math.log(1e-1)))
    dt_bias = dt + jnp.log(-jnp.expm1(-dt))
    gdn_norm_w = 1.0 + 0.02 * jax.random.normal(ks[5], (DEPTH, GDN_DV), f32)
    w_up_a = jax.random.normal(ks[6], (DEPTH, GDN_WIDTH, D_MODEL), f32) * (GDN_WIDTH ** -0.5) * DEEPNORM_BETA
    lambda_q1 = 0.1 * jax.random.normal(ks[7], (DEPTH, DIFF_DH), f32)
    lambda_k1 = 0.1 * jax.random.normal(ks[8], (DEPTH, DIFF_DH), f32)
    lambda_q2 = 0.1 * jax.random.normal(ks[9], (DEPTH, DIFF_DH), f32)
    lambda_k2 = 0.1 * jax.random.normal(ks[10], (DEPTH, DIFF_DH), f32)
    diff_norm_w = 1.0 + 0.02 * jax.random.normal(ks[11], (DEPTH, DIFF_DV), f32)
    w_up_b = jax.random.normal(ks[12], (DEPTH, DIFF_WIDTH, D_MODEL), f32) * (DIFF_WIDTH ** -0.5) * DEEPNORM_BETA
    w_out = jax.random.normal(ks[13], (DEPTH, D_MODEL, D_MODEL), f32) * (D_MODEL ** -0.5) * DEEPNORM_BETA
    ln_g = 1.0 + 0.02 * jax.random.normal(ks[14], (DEPTH, D_MODEL), f32)
    ln_b = 0.02 * jax.random.normal(ks[15], (DEPTH, D_MODEL), f32)
    return {"x": x, "w_in": w_in, "conv_w": conv_w, "a_log": a_log, "dt_bias": dt_bias,
            "gdn_norm_w": gdn_norm_w, "w_up_a": w_up_a, "lambda_q1": lambda_q1,
            "lambda_k1": lambda_k1, "lambda_q2": lambda_q2, "lambda_k2": lambda_k2,
            "diff_norm_w": diff_norm_w, "w_up_b": w_up_b, "w_out": w_out,
            "ln_g": ln_g, "ln_b": ln_b}


def reference(x, w_in, conv_w, a_log, dt_bias, gdn_norm_w, w_up_a, lambda_q1, lambda_k1,
              lambda_q2, lambda_k2, diff_norm_w, w_up_b, w_out, ln_g, ln_b):
    B, S, _ = x.shape
    f32 = jnp.float32
    slopes = _alibi_slopes(DIFF_HEADS)
    pts = _split_points()
    for layer in range(DEPTH):
        h = x @ w_in[layer]
        gq, gk, gv, ga, gb, gz, dq, dk, dv, dz, ma, mb = jnp.split(h, pts, axis=-1)

        qkv = jax.nn.silu(_causal_depthwise_conv(jnp.concatenate([gq, gk, gv], axis=-1), conv_w[layer]))
        cq, ck, cv = jnp.split(qkv.astype(f32), [GDN_QK, 2 * GDN_QK], axis=-1)
        q_a = _l2norm(cq.reshape(B, S, GDN_HEADS, GDN_DK))
        k_a = _l2norm(ck.reshape(B, S, GDN_HEADS, GDN_DK))
        v_a = cv.reshape(B, S, GDN_HEADS, GDN_DV)
        beta = jax.nn.sigmoid(gb.astype(f32))
        g = -jnp.exp(a_log[layer].astype(f32)) * jax.nn.softplus(ga.astype(f32) + dt_bias[layer].astype(f32))
        o_a = _gated_delta_rule(q_a, k_a, v_a, g, beta)
        o_a = _rms_norm(o_a, gdn_norm_w[layer], NORM_EPS) * jax.nn.silu(
            gz.astype(f32).reshape(B, S, GDN_HEADS, GDN_DV))
        y_a = o_a.reshape(B, S, GDN_WIDTH).astype(x.dtype) @ w_up_a[layer]

        lam_init = 0.8 - 0.6 * math.exp(-0.3 * layer)
        lam = (jnp.exp(jnp.sum(lambda_q1[layer].astype(f32) * lambda_k1[layer].astype(f32)))
               - jnp.exp(jnp.sum(lambda_q2[layer].astype(f32) * lambda_k2[layer].astype(f32)))
               + lam_init)
        o_b = _diff_attention(dq.reshape(B, S, DIFF_HEADS, 2, DIFF_DH),
                              dk.reshape(B, S, DIFF_HEADS, 2, DIFF_DH),
                              dv.reshape(B, S, DIFF_HEADS, DIFF_DV), lam, slopes)
        o_b = _rms_norm(o_b, diff_norm_w[layer], SUBLN_EPS) * (1.0 - lam_init) * jax.nn.silu(
            dz.astype(f32).reshape(B, S, DIFF_HEADS, DIFF_DV))
        y_b = o_b.reshape(B, S, DIFF_WIDTH).astype(x.dtype) @ w_up_b[layer]

        merged = jax.nn.sigmoid(ma) * y_a + jax.nn.sigmoid(mb) * y_b
        y = merged @ w_out[layer]
        x = _layer_norm(DEEPNORM_ALPHA * x + y, ln_g[layer], ln_b[layer])
    return x
```

```python
import functools
import math

import jax
import jax.numpy as jnp
from jax import lax
from jax.experimental import pallas as pl
from jax.experimental.pallas import tpu as pltpu

F32 = jnp.float32
BF16 = jnp.bfloat16

D_MODEL = 1024
GDN_HEADS = 4
GDN_DK = 128
GDN_DV = 128
GDN_QK = GDN_HEADS * GDN_DK
GDN_WIDTH = GDN_HEADS * GDN_DV
CONV_K = 4
DIFF_HEADS = 4
DIFF_DH = 64
DIFF_DV = 2 * DIFF_DH
DIFF_QK = DIFF_HEADS * 2 * DIFF_DH
DIFF_WIDTH = DIFF_HEADS * DIFF_DV
NORM_EPS = 1e-6
SUBLN_EPS = 1e-5
LN_EPS = 1e-5
DEPTH = 1
DEEPNORM_ALPHA = (2.0 * DEPTH) ** 0.25
LAM_INIT = 0.8 - 0.6 * math.exp(-0.3 * 0)
ALIBI_SLOPES = tuple(2.0 ** (-8.0 * (i + 1) / DIFF_HEADS) for i in range(DIFF_HEADS))

LANES = 128
CONV_HIST = 8
VMEM_LIMIT = 56 * 1024 * 1024

PROJ_TM = 256
GDN_T = 256
GDN_C = 64
ATT_TQ = 256
ATT_TK = 512
OUT_TM = 512
MASK_NEG = -1e30


def _sigmoid(x):
    return 1.0 / (1.0 + jnp.exp(-x))


def _silu(x):
    return x * _sigmoid(x)


def _dot(a, b):
    return jnp.dot(a, b, preferred_element_type=F32)


def _dot_nt(a, b):
    return lax.dot_general(a, b, (((1,), (1,)), ((), ())), preferred_element_type=F32)


def _split2(a):
    hi = a.astype(BF16)
    lo = (a - hi.astype(F32)).astype(BF16)
    return hi, lo


def _dot3(a, b):
    ah, al = _split2(a)
    bh, bl = _split2(b)
    return _dot(ah, bh) + _dot(ah, bl) + _dot(al, bh)


def _proj_kernel(x_ref, w_ref, wab_ref, hg_ref, hab_ref, gz_ref, dq_ref, dk_ref, dv_ref,
                 dz_ref, mab_ref):
    xb = x_ref[...].astype(BF16)

    def mm(c0, width):
        return _dot(xb, w_ref[:, c0:c0 + width])

    for j in range(3):
        hg_ref[:, j * 512:(j + 1) * 512] = mm(j * 512, 512)
    hab_ref[...] = _dot(xb, wab_ref[...])
    gz_ref[...] = mm(1536, 512)
    dq_ref[...] = (mm(2048, 512) * (DIFF_DH ** -0.5)).astype(BF16)
    dk_ref[...] = mm(2560, 512).astype(BF16)
    dv_ref[...] = mm(3072, 512).astype(BF16)
    dz_ref[...] = mm(3584, 512)
    for j in range(4):
        mab_ref[:, j * 512:(j + 1) * 512] = mm(4096 + j * 512, 512)


def _project(x2, w_main, w_ab):
    m = x2.shape[0]
    tm = PROJ_TM
    row = lambda width: pl.BlockSpec((tm, width), lambda i: (i, 0))
    full = lambda a: pl.BlockSpec(a.shape, lambda i: (0, 0))
    out_shape = (
        jax.ShapeDtypeStruct((m, 1536), F32),
        jax.ShapeDtypeStruct((m, LANES), F32),
        jax.ShapeDtypeStruct((m, 512), F32),
        jax.ShapeDtypeStruct((m, 512), BF16),
        jax.ShapeDtypeStruct((m, 512), BF16),
        jax.ShapeDtypeStruct((m, 512), BF16),
        jax.ShapeDtypeStruct((m, 512), F32),
        jax.ShapeDtypeStruct((m, 2048), F32),
    )
    return pl.pallas_call(
        _proj_kernel,
        grid=(m // tm,),
        in_specs=[row(D_MODEL), full(w_main), full(w_ab)],
        out_specs=tuple(row(s.shape[1]) for s in out_shape),
        out_shape=out_shape,
        compiler_params=pltpu.CompilerParams(
            dimension_semantics=("arbitrary",), vmem_limit_bytes=VMEM_LIMIT),
        name="in_proj",
    )(x2, w_main, w_ab)


def _inv_unit_lower(l_strict, eye):
    c = l_strict.shape[0]
    n = -l_strict
    x = eye + n
    p = n
    for _ in range(int(math.log2(c)) - 1):
        p = _dot3(p, p)
        x = x + _dot3(x, p)
    return x


def _gdn_kernel(hg_ref, hab_ref, gz_ref, cw_ref, alog_ref, dtb_ref, nw_ref, o_ref,
                ext_scr, st_scr):
    t_rows, c = GDN_T, GDN_C
    nc = t_rows // c

    @pl.when(pl.program_id(1) == 0)
    def _():
        st_scr[...] = jnp.zeros_like(st_scr)
        ext_scr[0:CONV_HIST, :] = jnp.zeros((CONV_HIST, ext_scr.shape[1]), F32)

    ext_scr[CONV_HIST:CONV_HIST + t_rows, :] = hg_ref[...]

    def conv_silu(c0):
        acc = None
        for j in range(CONV_K):
            r0 = CONV_HIST - (CONV_K - 1) + j
            term = cw_ref[j:j + 1, c0:c0 + LANES] * ext_scr[r0:r0 + t_rows, c0:c0 + LANES]
            acc = term if acc is None else acc + term
        return _silu(acc)

    def l2norm(v):
        return v * lax.rsqrt(jnp.sum(v * v, axis=-1, keepdims=True) + NORM_EPS)

    hab = hab_ref[...]
    xg = hab + dtb_ref[...]
    softplus = jnp.maximum(xg, 0.0) + jnp.log1p(jnp.exp(-jnp.abs(xg)))
    g_full = -jnp.exp(alog_ref[...]) * softplus
    beta_full = _sigmoid(hab)

    ri = lax.broadcasted_iota(jnp.int32, (t_rows, t_rows), 0)
    ci = lax.broadcasted_iota(jnp.int32, (t_rows, t_rows), 1)
    tri = jnp.where((ri // c == ci // c) & (ci <= ri), 1.0, 0.0).astype(BF16)
    g_hi = g_full.astype(BF16)
    g_r1 = g_full - g_hi.astype(F32)
    g_mid = g_r1.astype(BF16)
    g_lo = (g_r1 - g_mid.astype(F32)).astype(BF16)
    gc = _dot(tri, g_hi) + _dot(tri, g_mid) + _dot(tri, g_lo)
    gc_t = gc.T
    eg = jnp.exp(gc)

    rc = lax.broadcasted_iota(jnp.int32, (c, c), 0)
    cc = lax.broadcasted_iota(jnp.int32, (c, c), 1)
    causal = rc >= cc
    strict = rc > cc
    eye = jnp.where(rc == cc, 1.0, 0.0).astype(F32)

    nw = nw_ref[...]
    for h in range(GDN_HEADS):
        qn = l2norm(conv_silu(h * GDN_DK)) * (GDN_DK ** -0.5)
        kn = l2norm(conv_silu(GDN_QK + h * GDN_DK))
        vv = conv_silu(2 * GDN_QK + h * GDN_DV)
        gcol_all = gc[:, h:h + 1]
        bcol_all = beta_full[:, GDN_HEADS + h:GDN_HEADS + h + 1]
        egcol_all = eg[:, h:h + 1]

        pre = []
        kd_rows = []
        for ch in range(nc):
            r0 = ch * c
            sl = slice(r0, r0 + c)
            gcol = gcol_all[sl]
            grow = gc_t[h:h + 1, r0:r0 + c]
            gd = gcol - grow
            decay = jnp.where(causal, jnp.exp(jnp.where(causal, gd, 0.0)), 0.0)
            bcol = bcol_all[sl]
            kc = kn[sl]
            qc = qn[sl]
            kb = kc * bcol
            kq = _dot_nt(jnp.concatenate([kb, qc], axis=0).astype(BF16), kc.astype(BF16))
            lmat = jnp.where(strict, kq[:c] * decay, 0.0)
            amat = kq[c:] * decay
            tinv = _inv_unit_lower(lmat, eye)
            egc = egcol_all[sl]
            rhs = jnp.concatenate([vv[sl] * bcol, kb * egc], axis=1)
            sol = _dot3(tinv, rhs)
            g_last = gcol_all[r0 + c - 1:r0 + c]
            kd_rows.append(kc * jnp.exp(g_last - gcol))
            pre.append((sol[:, :GDN_DV], sol[:, GDN_DV:], qc * egc, amat, jnp.exp(g_last)))
        kd_t = jnp.concatenate(kd_rows, axis=0).T.astype(BF16)

        for ch in range(nc):
            r0 = ch * c
            u, w, qe, amat, eg_last = pre[ch]
            s_prev = st_scr[h]
            ws = _dot(jnp.concatenate([w, qe], axis=0).astype(BF16), s_prev.astype(BF16))
            v_new = u - ws[:c]
            v_new_b = v_new.astype(BF16)
            o = ws[c:] + _dot(amat.astype(BF16), v_new_b)
            st_scr[h] = s_prev * eg_last + _dot(kd_t[:, r0:r0 + c], v_new_b)
            o = o * lax.rsqrt(jnp.mean(o * o, axis=-1, keepdims=True) + NORM_EPS) * nw
            gate = _silu(gz_ref[r0:r0 + c, h * GDN_DV:(h + 1) * GDN_DV])
            o_ref[r0:r0 + c, h * GDN_DV:(h + 1) * GDN_DV] = (o * gate).astype(o_ref.dtype)

    ext_scr[0:CONV_HIST, :] = ext_scr[t_rows:t_rows + CONV_HIST, :]


def _gdn(hg, hab, gz, conv_w, alog_row, dtb_row, norm_w, batch, seq):
    t_rows = GDN_T
    nt = seq // t_rows
    row = lambda width: pl.BlockSpec((t_rows, width), lambda b, t: (b * nt + t, 0))
    full = lambda a: pl.BlockSpec(a.shape, lambda b, t: (0, 0))
    return pl.pallas_call(
        _gdn_kernel,
        grid=(batch, nt),
        in_specs=[row(1536), row(LANES), row(512), full(conv_w), full(alog_row),
                  full(dtb_row), full(norm_w)],
        out_specs=row(GDN_WIDTH),
        out_shape=jax.ShapeDtypeStruct((batch * seq, GDN_WIDTH), BF16),
        scratch_shapes=[pltpu.VMEM((t_rows + CONV_HIST, 1536), F32),
                        pltpu.VMEM((GDN_HEADS, GDN_DK, GDN_DV), F32)],
        compiler_params=pltpu.CompilerParams(
            dimension_semantics=("arbitrary", "arbitrary"), vmem_limit_bytes=VMEM_LIMIT),
        name="gated_deltanet",
    )(hg, hab, gz, conv_w, alog_row, dtb_row, norm_w)


def _attn_kernel(q_ref, k_ref, v_ref, dz_ref, nw_ref, lq1_ref, lk1_ref, lq2_ref, lk2_ref,
                 o_ref, m_scr, acc_scr):
    tq, tk = ATT_TQ, ATT_TK
    h = pl.program_id(1)
    i = pl.program_id(2)
    slope = jnp.float32(ALIBI_SLOPES[DIFF_HEADS - 1])
    for hh in range(DIFF_HEADS - 1):
        slope = jnp.where(h == hh, jnp.float32(ALIBI_SLOPES[hh]), slope)

    q = q_ref[...]
    lane = lax.broadcasted_iota(jnp.int32, q.shape, 1)
    zero = jnp.zeros_like(q)
    qz = jnp.concatenate([jnp.where(lane < DIFF_DH, q, zero),
                          jnp.where(lane >= DIFF_DH, q, zero)], axis=0)

    m_scr[...] = jnp.full(m_scr.shape, MASK_NEG, F32)
    acc_scr[...] = jnp.zeros_like(acc_scr)

    row_l = lax.broadcasted_iota(jnp.int32, (tq, tk), 0)
    col_l = lax.broadcasted_iota(jnp.int32, (tq, tk), 1)
    rel = (col_l - row_l).astype(F32)
    ones_cols = jnp.ones((tk, LANES), BF16)

    def step(j, masked):
        kb = k_ref[pl.ds(pl.multiple_of(j * tk, tk), tk), :]
        vb = v_ref[pl.ds(pl.multiple_of(j * tk, tk), tk), :]
        s = _dot_nt(qz, kb)
        off = (j * tk - i * tq).astype(F32)
        bias = slope * (rel + off)
        if masked:
            bias = jnp.where(col_l + j * tk <= row_l + i * tq, bias, MASK_NEG)
        s = s + jnp.concatenate([bias, bias], axis=0)
        m_prev = m_scr[...]
        m_new = jnp.maximum(m_prev, jnp.max(s, axis=-1, keepdims=True))
        p = jnp.exp(s - m_new)
        alpha = jnp.exp(m_prev - m_new)
        v_ext = jnp.concatenate([vb, ones_cols], axis=1)
        acc_scr[...] = acc_scr[...] * alpha + _dot(p.astype(BF16), v_ext)
        m_scr[...] = m_new

    n_full = (i * tq) // tk

    def body(j, carry):
        step(j, False)
        return carry

    lax.fori_loop(0, n_full, body, 0)
    step(n_full, True)

    acc = acc_scr[...]
    o1 = acc[:tq, :DIFF_DV] / acc[:tq, DIFF_DV:DIFF_DV + 1]
    o2 = acc[tq:, :DIFF_DV] / acc[tq:, DIFF_DV:DIFF_DV + 1]
    lam = (jnp.exp(jnp.sum(lq1_ref[...] * lk1_ref[...], axis=-1, keepdims=True))
           - jnp.exp(jnp.sum(lq2_ref[...] * lk2_ref[...], axis=-1, keepdims=True))
           + LAM_INIT)
    o = o1 - lam * o2
    o = o * lax.rsqrt(jnp.mean(o * o, axis=-1, keepdims=True) + SUBLN_EPS) * nw_ref[...]
    o = o * (1.0 - LAM_INIT) * _silu(dz_ref[...])
    o_ref[...] = o.astype(o_ref.dtype)


def _diff_attention(dq, dk, dv, dz, norm_w, lq1, lk1, lq2, lk2, batch, seq):
    tq = ATT_TQ
    nq = seq // tq
    qspec = pl.BlockSpec((tq, DIFF_DV), lambda b, h, i: (b * nq + i, h))
    kvspec = pl.BlockSpec((seq, DIFF_DV), lambda b, h, i: (b, h))
    full = lambda a: pl.BlockSpec(a.shape, lambda b, h, i: (0, 0))
    return pl.pallas_call(
        _attn_kernel,
        grid=(batch, DIFF_HEADS, nq),
        in_specs=[qspec, kvspec, kvspec, qspec, full(norm_w), full(lq1), full(lk1),
                  full(lq2), full(lk2)],
        out_specs=qspec,
        out_shape=jax.ShapeDtypeStruct((batch * seq, DIFF_WIDTH), BF16),
        scratch_shapes=[pltpu.VMEM((2 * tq, 1), F32),
                        pltpu.VMEM((2 * tq, 2 * LANES), F32)],
        compiler_params=pltpu.CompilerParams(
            dimension_semantics=("arbitrary", "arbitrary", "arbitrary"),
            vmem_limit_bytes=VMEM_LIMIT),
        name="diff_attention",
    )(dq, dk, dv, dz, norm_w, lq1, lk1, lq2, lk2)


def _out_kernel(x_ref, oa_ref, ob_ref, mab_ref, wa_ref, wb_ref, wo_ref, g_ref, b_ref, y_ref):
    ya = _dot(oa_ref[...], wa_ref[...])
    yb = _dot(ob_ref[...], wb_ref[...])
    merged = (_sigmoid(mab_ref[:, :D_MODEL]) * ya + _sigmoid(mab_ref[:, D_MODEL:]) * yb)
    y = _dot(merged.astype(BF16), wo_ref[...])
    z = DEEPNORM_ALPHA * x_ref[...] + y
    mu = jnp.mean(z, axis=-1, keepdims=True)
    zc = z - mu
    var = jnp.mean(zc * zc, axis=-1, keepdims=True)
    y_ref[...] = zc * lax.rsqrt(var + LN_EPS) * g_ref[...] + b_ref[...]


def _output(x2, o_a, o_b, mab, wa, wb, wo, ln_g, ln_b):
    m = x2.shape[0]
    tm = OUT_TM
    row = lambda width: pl.BlockSpec((tm, width), lambda i: (i, 0))
    full = lambda a: pl.BlockSpec(a.shape, lambda i: (0, 0))
    return pl.pallas_call(
        _out_kernel,
        grid=(m // tm,),
        in_specs=[row(D_MODEL), row(GDN_WIDTH), row(DIFF_WIDTH), row(2 * D_MODEL),
                  full(wa), full(wb), full(wo), full(ln_g), full(ln_b)],
        out_specs=row(D_MODEL),
        out_shape=jax.ShapeDtypeStruct((m, D_MODEL), F32),
        compiler_params=pltpu.CompilerParams(
            dimension_semantics=("arbitrary",), vmem_limit_bytes=VMEM_LIMIT),
        name="merge_out_ln",
    )(x2, o_a, o_b, mab, wa, wb, wo, ln_g, ln_b)


def _lane_row(v):
    return jnp.zeros((1, LANES), F32).at[0, :v.shape[0]].set(v.astype(F32))


def kernel(x, w_in, conv_w, a_log, dt_bias, gdn_norm_w, w_up_a, lambda_q1, lambda_k1,
           lambda_q2, lambda_k2, diff_norm_w, w_up_b, w_out, ln_g, ln_b):
    batch, seq, d = x.shape
    x2 = x.reshape(batch * seq, d)
    layer = 0
    w = w_in[layer]
    ab0 = 2 * GDN_QK + GDN_WIDTH
    ab1 = ab0 + 2 * GDN_HEADS
    w_main = jnp.concatenate([w[:, :ab0], w[:, ab1:]], axis=1).astype(BF16)
    w_ab = jnp.pad(w[:, ab0:ab1], ((0, 0), (0, LANES - 2 * GDN_HEADS))).astype(BF16)

    hg, hab, gz, dq, dk, dv, dz, mab = _project(x2, w_main, w_ab)

    o_a = _gdn(hg, hab, gz, conv_w[layer].astype(F32), _lane_row(a_log[layer]),
               _lane_row(dt_bias[layer]), gdn_norm_w[layer].reshape(1, GDN_DV).astype(F32),
               batch, seq)

    o_b = _diff_attention(
        dq, dk, dv, dz, diff_norm_w[layer].reshape(1, DIFF_DV).astype(F32),
        lambda_q1[layer].reshape(1, DIFF_DH).astype(F32),
        lambda_k1[layer].reshape(1, DIFF_DH).astype(F32),
        lambda_q2[layer].reshape(1, DIFF_DH).astype(F32),
        lambda_k2[layer].reshape(1, DIFF_DH).astype(F32), batch, seq)

    y = _output(x2, o_a, o_b, mab, w_up_a[layer].astype(BF16), w_up_b[layer].astype(BF16),
                w_out[layer].astype(BF16), ln_g[layer].reshape(1, d).astype(F32),
                ln_b[layer].reshape(1, d).astype(F32))
    return y.reshape(batch, seq, d)
```

```python
import math

import jax
import jax.numpy as jnp
from jax import lax
from jax.experimental import pallas as pl
from jax.experimental.pallas import tpu as pltpu

F32 = jnp.float32
BF16 = jnp.bfloat16

D_MODEL = 1024
GDN_HEADS = 4
GDN_DK = 128
GDN_DV = 128
GDN_QK = GDN_HEADS * GDN_DK
GDN_WIDTH = GDN_HEADS * GDN_DV
CONV_K = 4
DIFF_HEADS = 4
DIFF_DH = 64
DIFF_DV = 2 * DIFF_DH
DIFF_QK = DIFF_HEADS * 2 * DIFF_DH
DIFF_WIDTH = DIFF_HEADS * DIFF_DV
NORM_EPS = 1e-6
SUBLN_EPS = 1e-5
LN_EPS = 1e-5
DEPTH = 1
DEEPNORM_ALPHA = (2.0 * DEPTH) ** 0.25
LAM_INIT = 0.8 - 0.6 * math.exp(-0.3 * 0)
ALIBI_SLOPES = tuple(2.0 ** (-8.0 * (i + 1) / DIFF_HEADS) for i in range(DIFF_HEADS))
LOG2E = math.log2(math.e)

LANES = 128
CONV_HIST = 8
VMEM_LIMIT = 56 * 1024 * 1024

PROJ_TM = 256
GDN_T = 256
GDN_INV_BASE = 16
ATT_TQ = 512
ATT_TK = 512
ATT_ROWS = 256
OUT_TM = 512
MASK_NEG = -1e30


def _sigmoid(x):
    return 1.0 / (1.0 + jnp.exp(-x))


def _silu(x):
    return x * _sigmoid(x)


def _dot(a, b):
    return jnp.dot(a, b, preferred_element_type=F32)


def _dot_nt(a, b):
    return lax.dot_general(a, b, (((1,), (1,)), ((), ())), preferred_element_type=F32)


def _proj_kernel(x_ref, w_ref, wab_ref, hg_ref, hab_ref, gz_ref, dq_ref, dk_ref, dv_ref,
                 dz_ref, mab_ref):
    xb = x_ref[...].astype(BF16)

    def mm(c0, width):
        return _dot(xb, w_ref[:, c0:c0 + width])

    for j in range(3):
        hg_ref[:, j * 512:(j + 1) * 512] = mm(j * 512, 512)
    hab_ref[...] = _dot(xb, wab_ref[...])
    gz_ref[...] = mm(1536, 512)
    dq_ref[...] = (mm(2048, 512) * (DIFF_DH ** -0.5 * LOG2E)).astype(BF16)
    dk_ref[...] = mm(2560, 512).astype(BF16)
    dv_ref[...] = mm(3072, 512).astype(BF16)
    dz_ref[...] = mm(3584, 512)
    for j in range(4):
        mab_ref[:, j * 512:(j + 1) * 512] = mm(4096 + j * 512, 512)


def _project(x2, w_main, w_ab):
    m = x2.shape[0]
    tm = PROJ_TM
    row = lambda width: pl.BlockSpec((tm, width), lambda i: (i, 0))
    full = lambda a: pl.BlockSpec(a.shape, lambda i: (0, 0))
    out_shape = (
        jax.ShapeDtypeStruct((m, 1536), F32),
        jax.ShapeDtypeStruct((m, LANES), F32),
        jax.ShapeDtypeStruct((m, 512), F32),
        jax.ShapeDtypeStruct((m, 512), BF16),
        jax.ShapeDtypeStruct((m, 512), BF16),
        jax.ShapeDtypeStruct((m, 512), BF16),
        jax.ShapeDtypeStruct((m, 512), F32),
        jax.ShapeDtypeStruct((m, 2048), F32),
    )
    return pl.pallas_call(
        _proj_kernel,
        grid=(m // tm,),
        in_specs=[row(D_MODEL), full(w_main), full(w_ab)],
        out_specs=tuple(row(s.shape[1]) for s in out_shape),
        out_shape=out_shape,
        compiler_params=pltpu.CompilerParams(
            dimension_semantics=("arbitrary",), vmem_limit_bytes=VMEM_LIMIT),
        name="in_proj",
    )(x2, w_main, w_ab)


def _gdn_kernel(hg_ref, hab_ref, gz_ref, cw_ref, alog_ref, dtb_ref, nw_ref, o_ref,
                ext_scr, st_scr):
    t = GDN_T
    heads = range(GDN_HEADS)

    @pl.when(pl.program_id(1) == 0)
    def _():
        st_scr[...] = jnp.zeros_like(st_scr)
        ext_scr[0:CONV_HIST, :] = jnp.zeros((CONV_HIST, ext_scr.shape[1]), F32)

    ext_scr[CONV_HIST:CONV_HIST + t, :] = hg_ref[...]

    def conv_silu(c0):
        acc = None
        for j in range(CONV_K):
            r0 = CONV_HIST - (CONV_K - 1) + j
            term = cw_ref[j:j + 1, c0:c0 + LANES] * ext_scr[r0:r0 + t, c0:c0 + LANES]
            acc = term if acc is None else acc + term
        return _silu(acc)

    def l2norm(v):
        return v * lax.rsqrt(jnp.sum(v * v, axis=-1, keepdims=True) + NORM_EPS)

    hab = hab_ref[...]
    xg = hab + dtb_ref[...]
    softplus = jnp.maximum(xg, 0.0) + jnp.log1p(jnp.exp(-jnp.abs(xg)))
    g_full = -jnp.exp(alog_ref[...]) * softplus
    beta_full = _sigmoid(hab)

    ri = lax.broadcasted_iota(jnp.int32, (t, t), 0)
    ci = lax.broadcasted_iota(jnp.int32, (t, t), 1)
    causal = ri >= ci
    strict = ri > ci
    eye = jnp.where(ri == ci, 1.0, 0.0).astype(F32)

    tri = jnp.where(causal, 1.0, 0.0).astype(BF16)
    g_hi = g_full.astype(BF16)
    g_r1 = g_full - g_hi.astype(F32)
    g_mid = g_r1.astype(BF16)
    g_lo = (g_r1 - g_mid.astype(F32)).astype(BF16)
    gc = _dot(tri, g_hi) + _dot(tri, g_mid) + _dot(tri, g_lo)
    gc_t = gc.T
    eg = jnp.exp(gc)
    g_last = gc[t - 1:t, :]
    k_dec = jnp.exp(g_last - gc)
    eg_last = jnp.exp(g_last)

    qn = [l2norm(conv_silu(h * GDN_DK)) * (GDN_DK ** -0.5) for h in heads]
    kn = [l2norm(conv_silu(GDN_QK + h * GDN_DK)) for h in heads]
    vv = [conv_silu(2 * GDN_QK + h * GDN_DV) for h in heads]
    ext_scr[0:CONV_HIST, :] = ext_scr[t:t + CONV_HIST, :]

    bcol = [beta_full[:, GDN_HEADS + h:GDN_HEADS + h + 1] for h in heads]
    egc = [eg[:, h:h + 1] for h in heads]
    kb = [kn[h] * bcol[h] for h in heads]
    kq = [_dot_nt(jnp.concatenate([kb[h], qn[h]], axis=0).astype(BF16), kn[h].astype(BF16))
          for h in heads]
    decay = []
    for h in heads:
        gd = gc[:, h:h + 1] - gc_t[h:h + 1, :]
        decay.append(jnp.where(causal, jnp.exp(jnp.where(causal, gd, 0.0)), 0.0))
    nmat = [jnp.where(strict, -(kq[h][:t] * decay[h]), 0.0) for h in heads]
    amat = [(kq[h][t:] * decay[h]).astype(BF16) for h in heads]
    rhs = [jnp.concatenate([vv[h] * bcol[h], kb[h] * egc[h]], axis=1).astype(BF16)
           for h in heads]
    qe = [qn[h] * egc[h] for h in heads]
    kd_t = [(kn[h] * k_dec[:, h:h + 1]).T.astype(BF16) for h in heads]

    def same_block(b):
        sh = int(math.log2(b))
        return (ri >> sh) == (ci >> sh)

    blk = GDN_INV_BASE
    in_blk = same_block(blk)
    n0 = [jnp.where(in_blk, nmat[h], 0.0) for h in heads]
    x = [eye + n0[h] for h in heads]
    pw = [n0[h].astype(BF16) for h in heads]
    for _ in range(int(math.log2(blk)) - 1):
        p32 = [_dot(pw[h], pw[h]) for h in heads]
        pw = [p32[h].astype(BF16) for h in heads]
        x = [x[h] + _dot(x[h].astype(BF16), pw[h]) for h in heads]
    while blk < t:
        in_big = same_block(2 * blk)
        n_off = [jnp.where(in_big, jnp.where(in_blk, 0.0, nmat[h]), 0.0).astype(BF16)
                 for h in heads]
        xb = [x[h].astype(BF16) for h in heads]
        xn = [_dot(xb[h], n_off[h]).astype(BF16) for h in heads]
        x = [x[h] + _dot(xn[h], xb[h]) for h in heads]
        in_blk = in_big
        blk *= 2
    sol = [_dot(x[h].astype(BF16), rhs[h]) for h in heads]

    s_prev = [st_scr[h] for h in heads]
    ws = [_dot(jnp.concatenate([sol[h][:, GDN_DV:], qe[h]], axis=0).astype(BF16),
               s_prev[h].astype(BF16)) for h in heads]
    v_new = [(sol[h][:, :GDN_DV] - ws[h][:t]).astype(BF16) for h in heads]
    o_l = [ws[h][t:] + _dot(amat[h], v_new[h]) for h in heads]
    for h in heads:
        st_scr[h] = s_prev[h] * eg_last[:, h:h + 1] + _dot(kd_t[h], v_new[h])
    nw = nw_ref[...]
    for h in heads:
        o = o_l[h]
        o = o * lax.rsqrt(jnp.mean(o * o, axis=-1, keepdims=True) + NORM_EPS) * nw
        gate = _silu(gz_ref[:, h * GDN_DV:(h + 1) * GDN_DV])
        o_ref[:, h * GDN_DV:(h + 1) * GDN_DV] = (o * gate).astype(o_ref.dtype)


def _gdn(hg, hab, gz, conv_w, alog_row, dtb_row, norm_w, batch, seq):
    t = GDN_T
    nt = seq // t
    row = lambda width: pl.BlockSpec((t, width), lambda b, s: (b * nt + s, 0))
    full = lambda a: pl.BlockSpec(a.shape, lambda b, s: (0, 0))
    return pl.pallas_call(
        _gdn_kernel,
        grid=(batch, nt),
        in_specs=[row(1536), row(LANES), row(512), full(conv_w), full(alog_row),
                  full(dtb_row), full(norm_w)],
        out_specs=row(GDN_WIDTH),
        out_shape=jax.ShapeDtypeStruct((batch * seq, GDN_WIDTH), BF16),
        scratch_shapes=[pltpu.VMEM((t + CONV_HIST, 1536), F32),
                        pltpu.VMEM((GDN_HEADS, GDN_DK, GDN_DV), F32)],
        compiler_params=pltpu.CompilerParams(
            dimension_semantics=("arbitrary", "arbitrary"), vmem_limit_bytes=VMEM_LIMIT),
        name="gated_deltanet",
    )(hg, hab, gz, conv_w, alog_row, dtb_row, norm_w)


def _attn_kernel(q_ref, k_ref, v_ref, dz_ref, nw_ref, lq1_ref, lk1_ref, lq2_ref, lk2_ref,
                 o_ref, bias_scr, m_scr, acc_scr, p_scr, alpha_scr):
    tq, tk, rows = ATT_TQ, ATT_TK, ATT_ROWS
    n_half = tq // rows
    h = pl.program_id(1)
    i = pl.program_id(2)
    slope = jnp.float32(ALIBI_SLOPES[DIFF_HEADS - 1] * LOG2E)
    for hh in range(DIFF_HEADS - 1):
        slope = jnp.where(h == hh, jnp.float32(ALIBI_SLOPES[hh] * LOG2E), slope)

    q = q_ref[...]
    lane = lax.broadcasted_iota(jnp.int32, q.shape, 1)
    zero = jnp.zeros_like(q)
    qmaps = (jnp.where(lane < DIFF_DH, q, zero), jnp.where(lane >= DIFF_DH, q, zero))
    chains = [(mp, r) for mp in range(2) for r in range(n_half)]
    n_chains = len(chains)

    m_scr[...] = jnp.full(m_scr.shape, MASK_NEG, F32)
    acc_scr[...] = jnp.zeros_like(acc_scr)

    row_l = lax.broadcasted_iota(jnp.int32, (rows, tk), 0)
    col_l = lax.broadcasted_iota(jnp.int32, (rows, tk), 1)
    bias_scr[...] = slope * (col_l - row_l).astype(F32)
    row_t = lax.broadcasted_iota(jnp.int32, (rows, LANES), 0)
    col_t = lax.broadcasted_iota(jnp.int32, (rows, LANES), 1)

    def load_k(j):
        return k_ref[pl.ds(pl.multiple_of(j * tk, tk), tk), :]

    def load_v_ext(j):
        vb = v_ref[pl.ds(pl.multiple_of(j * tk, tk), tk), :]
        return jnp.concatenate([vb, jnp.ones((tk, LANES), BF16)], axis=1)

    def scores_step(c, j, kb, slot, masked):
        mp, r = chains[c]
        nt = tk // LANES
        qc = qmaps[mp][r * rows:(r + 1) * rows]
        s = _dot_nt(qc, kb)
        cj = slope * (j * tk - i * tq - r * rows).astype(F32)
        tiles = []
        for tl in range(nt):
            st = s[:, tl * LANES:(tl + 1) * LANES] + bias_scr[:, tl * LANES:(tl + 1) * LANES]
            if masked and (tl + 1) * LANES - 1 > r * rows:
                st = jnp.where(col_t + tl * LANES <= row_t + r * rows, st, MASK_NEG)
            tiles.append(st)
        mx = tiles[0]
        for st in tiles[1:]:
            mx = jnp.maximum(mx, st)
        m_prev = m_scr[c]
        m_new = jnp.maximum(m_prev, jnp.max(mx, axis=-1, keepdims=True) + cj)
        shift = m_new - cj
        for tl in range(nt):
            p_scr[slot, c, :, tl * LANES:(tl + 1) * LANES] = (
                jnp.exp2(tiles[tl] - shift).astype(BF16))
        alpha_scr[slot, c] = jnp.exp2(m_prev - m_new)
        m_scr[c] = m_new

    def pv_step(c, v_ext, slot):
        alpha = alpha_scr[slot, c]
        pv = _dot(p_scr[slot, c], v_ext)
        acc_scr[c] = acc_scr[c] * jnp.concatenate([alpha, alpha], axis=1) + pv

    def scores_block(j, slot, masked=False):
        kb = load_k(j)
        for c in range(n_chains):
            scores_step(c, j, kb, slot, masked)

    def pv_block(j, slot):
        v_ext = load_v_ext(j)
        for c in range(n_chains):
            pv_step(c, v_ext, slot)

    scores_block(i, 0, masked=True)

    def pair(u, carry):
        j = i - 2 * u - 1
        scores_block(j, 1)
        pv_block(j + 1, 0)
        scores_block(j - 1, 0)
        pv_block(j, 1)
        return carry

    lax.fori_loop(0, i // 2, pair, 0)

    @pl.when(i % 2 == 1)
    def _():
        scores_block(0, 1)
        pv_block(1, 0)
        pv_block(0, 1)

    @pl.when(i % 2 == 0)
    def _():
        pv_block(0, 0)


    def map_out(mp):
        parts = []
        for r in range(n_half):
            acc = acc_scr[mp * n_half + r]
            parts.append(acc[:, :DIFF_DV] / acc[:, DIFF_DV:])
        return jnp.concatenate(parts, axis=0)

    lam = (jnp.exp(jnp.sum(lq1_ref[...] * lk1_ref[...], axis=-1, keepdims=True))
           - jnp.exp(jnp.sum(lq2_ref[...] * lk2_ref[...], axis=-1, keepdims=True))
           + LAM_INIT)
    o = map_out(0) - lam * map_out(1)
    o = o * lax.rsqrt(jnp.mean(o * o, axis=-1, keepdims=True) + SUBLN_EPS) * nw_ref[...]
    o = o * (1.0 - LAM_INIT) * _silu(dz_ref[...])
    o_ref[...] = o.astype(o_ref.dtype)


def _diff_attention(dq, dk, dv, dz, norm_w, lq1, lk1, lq2, lk2, batch, seq):
    tq = ATT_TQ
    nq = seq // tq
    n_chains = 2 * (tq // ATT_ROWS)
    qspec = pl.BlockSpec((tq, DIFF_DV), lambda b, h, i: (b * nq + i, h))
    kvspec = pl.BlockSpec((seq, DIFF_DV), lambda b, h, i: (b, h))
    full = lambda a: pl.BlockSpec(a.shape, lambda b, h, i: (0, 0))
    return pl.pallas_call(
        _attn_kernel,
        grid=(batch, DIFF_HEADS, nq),
        in_specs=[qspec, kvspec, kvspec, qspec, full(norm_w), full(lq1), full(lk1),
                  full(lq2), full(lk2)],
        out_specs=qspec,
        out_shape=jax.ShapeDtypeStruct((batch * seq, DIFF_WIDTH), BF16),
        scratch_shapes=[pltpu.VMEM((ATT_ROWS, ATT_TK), F32),
                        pltpu.VMEM((n_chains, ATT_ROWS, LANES), F32),
                        pltpu.VMEM((n_chains, ATT_ROWS, 2 * LANES), F32),
                        pltpu.VMEM((2, n_chains, ATT_ROWS, ATT_TK), BF16),
                        pltpu.VMEM((2, n_chains, ATT_ROWS, LANES), F32)],
        compiler_params=pltpu.CompilerParams(
            dimension_semantics=("arbitrary", "arbitrary", "arbitrary"),
            vmem_limit_bytes=VMEM_LIMIT),
        name="diff_attention",
    )(dq, dk, dv, dz, norm_w, lq1, lk1, lq2, lk2)


def _out_kernel(x_ref, oa_ref, ob_ref, mab_ref, wa_ref, wb_ref, wo_ref, g_ref, b_ref, y_ref):
    ya = _dot(oa_ref[...], wa_ref[...])
    yb = _dot(ob_ref[...], wb_ref[...])
    merged = (_sigmoid(mab_ref[:, :D_MODEL]) * ya + _sigmoid(mab_ref[:, D_MODEL:]) * yb)
    y = _dot(merged.astype(BF16), wo_ref[...])
    z = DEEPNORM_ALPHA * x_ref[...] + y
    mu = jnp.mean(z, axis=-1, keepdims=True)
    zc = z - mu
    var = jnp.mean(zc * zc, axis=-1, keepdims=True)
    y_ref[...] = zc * lax.rsqrt(var + LN_EPS) * g_ref[...] + b_ref[...]


def _output(x2, o_a, o_b, mab, wa, wb, wo, ln_g, ln_b):
    m = x2.shape[0]
    tm = OUT_TM
    row = lambda width: pl.BlockSpec((tm, width), lambda i: (i, 0))
    full = lambda a: pl.BlockSpec(a.shape, lambda i: (0, 0))
    return pl.pallas_call(
        _out_kernel,
        grid=(m // tm,),
        in_specs=[row(D_MODEL), row(GDN_WIDTH), row(DIFF_WIDTH), row(2 * D_MODEL),
                  full(wa), full(wb), full(wo), full(ln_g), full(ln_b)],
        out_specs=row(D_MODEL),
        out_shape=jax.ShapeDtypeStruct((m, D_MODEL), F32),
        compiler_params=pltpu.CompilerParams(
            dimension_semantics=("arbitrary",), vmem_limit_bytes=VMEM_LIMIT),
        name="merge_out_ln",
    )(x2, o_a, o_b, mab, wa, wb, wo, ln_g, ln_b)


def _lane_row(v):
    return jnp.zeros((1, LANES), F32).at[0, :v.shape[0]].set(v.astype(F32))


def kernel(x, w_in, conv_w, a_log, dt_bias, gdn_norm_w, w_up_a, lambda_q1, lambda_k1,
           lambda_q2, lambda_k2, diff_norm_w, w_up_b, w_out, ln_g, ln_b):
    batch, seq, d = x.shape
    x2 = x.reshape(batch * seq, d)
    layer = 0
    w = w_in[layer]
    ab0 = 2 * GDN_QK + GDN_WIDTH
    ab1 = ab0 + 2 * GDN_HEADS
    w_main = jnp.concatenate([w[:, :ab0], w[:, ab1:]], axis=1).astype(BF16)
    w_ab = jnp.pad(w[:, ab0:ab1], ((0, 0), (0, LANES - 2 * GDN_HEADS))).astype(BF16)

    hg, hab, gz, dq, dk, dv, dz, mab = _project(x2, w_main, w_ab)

    o_a = _gdn(hg, hab, gz, conv_w[layer].astype(F32), _lane_row(a_log[layer]),
               _lane_row(dt_bias[layer]), gdn_norm_w[layer].reshape(1, GDN_DV).astype(F32),
               batch, seq)

    o_b = _diff_attention(
        dq, dk, dv, dz, diff_norm_w[layer].reshape(1, DIFF_DV).astype(F32),
        lambda_q1[layer].reshape(1, DIFF_DH).astype(F32),
        lambda_k1[layer].reshape(1, DIFF_DH).astype(F32),
        lambda_q2[layer].reshape(1, DIFF_DH).astype(F32),
        lambda_k2[layer].reshape(1, DIFF_DH).astype(F32), batch, seq)

    y = _output(x2, o_a, o_b, mab, w_up_a[layer].astype(BF16), w_up_b[layer].astype(BF16),
                w_out[layer].astype(BF16), ln_g[layer].reshape(1, d).astype(F32),
                ln_b[layer].reshape(1, d).astype(F32))
    return y.reshape(batch, seq, d)
```

```python
import functools
import math

import jax
import jax.numpy as jnp
from jax import lax
from jax.experimental import pallas as pl
from jax.experimental.pallas import tpu as pltpu

F32 = jnp.float32
BF16 = jnp.bfloat16

D_MODEL = 1024
GDN_HEADS = 4
GDN_DK = 128
GDN_DV = 128
GDN_QK = GDN_HEADS * GDN_DK
GDN_WIDTH = GDN_HEADS * GDN_DV
CONV_K = 4
DIFF_HEADS = 4
DIFF_DH = 64
DIFF_DV = 2 * DIFF_DH
DIFF_QK = DIFF_HEADS * 2 * DIFF_DH
DIFF_WIDTH = DIFF_HEADS * DIFF_DV
NORM_EPS = 1e-6
SUBLN_EPS = 1e-5
LN_EPS = 1e-5
DEPTH = 1
DEEPNORM_ALPHA = (2.0 * DEPTH) ** 0.25
LAM_INIT = 0.8 - 0.6 * math.exp(-0.3 * 0)
ALIBI_SLOPES = tuple(2.0 ** (-8.0 * (i + 1) / DIFF_HEADS) for i in range(DIFF_HEADS))
LOG2E = math.log2(math.e)

LANES = 128
CONV_HIST = 8
VMEM_LIMIT = 56 * 1024 * 1024

PROJ_TM = 256
GDN_T = 256
GDN_INV_BASE = 16
ATT_TQ = 512
ATT_TK = 512
ATT_ROWS = 256
OUT_TM = 512
MASK_NEG = -1e30


def _sigmoid(x):
    return 1.0 / (1.0 + jnp.exp(-x))


def _silu(x):
    return x * _sigmoid(x)


def _dot(a, b):
    return jnp.dot(a, b, preferred_element_type=F32)


def _dot_nt(a, b):
    return lax.dot_general(a, b, (((1,), (1,)), ((), ())), preferred_element_type=F32)


def _proj_kernel(tiles_per_seq, x_ref, w_ref, wab_ref, cw_ref, gq_ref, gk_ref, gv_ref,
                 hab_ref, gz_ref, dq_ref, dk_ref, dv_ref, dz_ref, mab_ref, hist_scr, xb_scr):
    tm = PROJ_TM

    @pl.when(pl.program_id(0) % tiles_per_seq == 0)
    def _():
        hist_scr[...] = jnp.zeros_like(hist_scr)

    xb_scr[...] = x_ref[...].astype(BF16)

    def mm(c0, width):
        return _dot(xb_scr[...], w_ref[:, c0:c0 + width])

    piece = 2 * GDN_DK

    def conv_silu(c0):
        cols = slice(c0, c0 + piece)
        acc = mm(c0, piece)
        ext = jnp.concatenate([hist_scr[:, cols], acc], axis=0)
        hist_scr[:, cols] = acc[tm - CONV_HIST:]
        y = None
        for j in range(CONV_K):
            r0 = CONV_HIST - (CONV_K - 1) + j
            term = cw_ref[j:j + 1, cols] * ext[r0:r0 + tm]
            y = term if y is None else y + term
        return _silu(y)

    def l2norm_heads(y, scale):
        parts = []
        for h in range(piece // GDN_DK):
            v = y[:, h * GDN_DK:(h + 1) * GDN_DK]
            inv = lax.rsqrt(jnp.sum(v * v, axis=-1, keepdims=True) + NORM_EPS)
            parts.append(v * (inv * scale) if scale != 1.0 else v * inv)
        return jnp.concatenate(parts, axis=1)

    def plain(out_ref, c0, scale=None):
        for j in range(out_ref.shape[1] // 512):
            acc = mm(c0 + j * 512, 512)
            if scale is not None:
                acc = acc * scale
            out_ref[:, j * 512:(j + 1) * 512] = acc.astype(out_ref.dtype)

    for half in range(2):
        c0 = half * piece
        gq_ref[:, c0:c0 + piece] = l2norm_heads(conv_silu(c0), GDN_DK ** -0.5).astype(BF16)
        if half == 0:
            plain(gz_ref, 1536)
        else:
            plain(dq_ref, 2048, DIFF_DH ** -0.5 * LOG2E)
    for half in range(2):
        c0 = half * piece
        gk_ref[:, c0:c0 + piece] = l2norm_heads(conv_silu(GDN_QK + c0), 1.0).astype(BF16)
        plain(dk_ref if half == 0 else dv_ref, 2560 + half * 512)
    for half in range(2):
        c0 = half * piece
        gv_ref[:, c0:c0 + piece] = conv_silu(2 * GDN_QK + c0).astype(BF16)
        if half == 0:
            plain(dz_ref, 3584)
    hab_ref[...] = _dot(xb_scr[...], wab_ref[...])
    plain(mab_ref, 4096)


def _project(x2, w_main, w_ab, conv_w, seq):
    m = x2.shape[0]
    tm = PROJ_TM
    row = lambda width: pl.BlockSpec((tm, width), lambda i: (i, 0))
    full = lambda a: pl.BlockSpec(a.shape, lambda i: (0, 0))
    out_shape = (
        jax.ShapeDtypeStruct((m, GDN_QK), BF16),
        jax.ShapeDtypeStruct((m, GDN_QK), BF16),
        jax.ShapeDtypeStruct((m, GDN_WIDTH), BF16),
        jax.ShapeDtypeStruct((m, LANES), F32),
        jax.ShapeDtypeStruct((m, 512), BF16),
        jax.ShapeDtypeStruct((m, 512), BF16),
        jax.ShapeDtypeStruct((m, 512), BF16),
        jax.ShapeDtypeStruct((m, 512), BF16),
        jax.ShapeDtypeStruct((m, 512), BF16),
        jax.ShapeDtypeStruct((m, 2048), BF16),
    )
    return pl.pallas_call(
        functools.partial(_proj_kernel, seq // tm),
        grid=(m // tm,),
        in_specs=[row(D_MODEL), full(w_main), full(w_ab), full(conv_w)],
        out_specs=tuple(row(s.shape[1]) for s in out_shape),
        out_shape=out_shape,
        scratch_shapes=[pltpu.VMEM((CONV_HIST, 3 * GDN_QK), F32),
                        pltpu.VMEM((tm, D_MODEL), BF16)],
        compiler_params=pltpu.CompilerParams(
            dimension_semantics=("arbitrary",), vmem_limit_bytes=VMEM_LIMIT),
        name="in_proj",
    )(x2, w_main, w_ab, conv_w)


def _gdn_kernel(gq_ref, gk_ref, gv_ref, hab_ref, gz_ref, alog_ref, dtb_ref, nw_ref, o_ref,
                st_scr):
    t = GDN_T
    heads = range(GDN_HEADS)

    @pl.when(pl.program_id(1) == 0)
    def _():
        st_scr[...] = jnp.zeros_like(st_scr)

    hab = hab_ref[...]
    xg = hab + dtb_ref[...]
    y_sp = jnp.exp(-jnp.abs(xg))
    u_sp = 1.0 + y_sp
    softplus = jnp.maximum(xg, 0.0) + (jnp.log(u_sp) - ((u_sp - 1.0) - y_sp) / u_sp)
    g_full = -jnp.exp(alog_ref[...]) * softplus
    beta_full = _sigmoid(hab)

    ri = lax.broadcasted_iota(jnp.int32, (t, t), 0)
    ci = lax.broadcasted_iota(jnp.int32, (t, t), 1)
    causal = ri >= ci
    strict = ri > ci
    eye = jnp.where(ri == ci, 1.0, 0.0).astype(F32)

    tri = jnp.where(causal, 1.0, 0.0).astype(BF16)
    g_hi = g_full.astype(BF16)
    g_r1 = g_full - g_hi.astype(F32)
    g_mid = g_r1.astype(BF16)
    g_lo = (g_r1 - g_mid.astype(F32)).astype(BF16)
    gc = _dot(tri, g_hi) + _dot(tri, g_mid) + _dot(tri, g_lo)
    gc_t = gc.T
    eg = jnp.exp(gc)
    g_last = gc[t - 1:t, :]
    k_dec = jnp.exp(g_last - gc)
    eg_last = jnp.exp(g_last)

    qn_b = [gq_ref[:, h * GDN_DK:(h + 1) * GDN_DK] for h in heads]
    kn_b = [gk_ref[:, h * GDN_DK:(h + 1) * GDN_DK] for h in heads]
    qn = [qn_b[h].astype(F32) for h in heads]
    kn = [kn_b[h].astype(F32) for h in heads]
    vv = [gv_ref[:, h * GDN_DV:(h + 1) * GDN_DV].astype(F32) for h in heads]

    bcol = [beta_full[:, GDN_HEADS + h:GDN_HEADS + h + 1] for h in heads]
    egc = [eg[:, h:h + 1] for h in heads]
    kb = [kn[h] * bcol[h] for h in heads]
    kq = [_dot_nt(jnp.concatenate([kb[h].astype(BF16), qn_b[h]], axis=0), kn_b[h])
          for h in heads]
    decay = []
    for h in heads:
        gd = gc[:, h:h + 1] - gc_t[h:h + 1, :]
        decay.append(jnp.where(causal, jnp.exp(jnp.where(causal, gd, 0.0)), 0.0))
    nmat = [jnp.where(strict, -(kq[h][:t] * decay[h]), 0.0) for h in heads]
    amat = [(kq[h][t:] * decay[h]).astype(BF16) for h in heads]
    rhs = [jnp.concatenate([vv[h] * bcol[h], kb[h] * egc[h]], axis=1).astype(BF16)
           for h in heads]
    qe = [qn[h] * egc[h] for h in heads]
    kd_t = [(kn[h] * k_dec[:, h:h + 1]).T.astype(BF16) for h in heads]

    def same_block(b):
        sh = int(math.log2(b))
        return (ri >> sh) == (ci >> sh)

    blk = GDN_INV_BASE
    in_blk = same_block(blk)
    n0 = [jnp.where(in_blk, nmat[h], 0.0) for h in heads]
    x = [eye + n0[h] for h in heads]
    pw = [n0[h].astype(BF16) for h in heads]
    for _ in range(int(math.log2(blk)) - 1):
        p32 = [_dot(pw[h], pw[h]) for h in heads]
        pw = [p32[h].astype(BF16) for h in heads]
        x = [x[h] + _dot(x[h].astype(BF16), pw[h]) for h in heads]
    while blk < t:
        in_big = same_block(2 * blk)
        n_off = [jnp.where(in_big, jnp.where(in_blk, 0.0, nmat[h]), 0.0).astype(BF16)
                 for h in heads]
        xb = [x[h].astype(BF16) for h in heads]
        xn = [_dot(xb[h], n_off[h]).astype(BF16) for h in heads]
        x = [x[h] + _dot(xn[h], xb[h]) for h in heads]
        in_blk = in_big
        blk *= 2
    sol = [_dot(x[h].astype(BF16), rhs[h]) for h in heads]

    s_prev = [st_scr[h] for h in heads]
    ws = [_dot(jnp.concatenate([sol[h][:, GDN_DV:], qe[h]], axis=0).astype(BF16),
               s_prev[h].astype(BF16)) for h in heads]
    v_new = [(sol[h][:, :GDN_DV] - ws[h][:t]).astype(BF16) for h in heads]
    o_l = [ws[h][t:] + _dot(amat[h], v_new[h]) for h in heads]
    for h in heads:
        st_scr[h] = s_prev[h] * eg_last[:, h:h + 1] + _dot(kd_t[h], v_new[h])
    nw = nw_ref[...]
    for h in heads:
        o = o_l[h]
        o = o * lax.rsqrt(jnp.mean(o * o, axis=-1, keepdims=True) + NORM_EPS) * nw
        gate = _silu(gz_ref[:, h * GDN_DV:(h + 1) * GDN_DV].astype(F32))
        o_ref[:, h * GDN_DV:(h + 1) * GDN_DV] = (o * gate).astype(o_ref.dtype)


def _gdn(gq, gk, gv, hab, gz, alog_row, dtb_row, norm_w, batch, seq):
    t = GDN_T
    nt = seq // t
    row = lambda width: pl.BlockSpec((t, width), lambda b, s: (b * nt + s, 0))
    full = lambda a: pl.BlockSpec(a.shape, lambda b, s: (0, 0))
    return pl.pallas_call(
        _gdn_kernel,
        grid=(batch, nt),
        in_specs=[row(GDN_QK), row(GDN_QK), row(GDN_WIDTH), row(LANES), row(GDN_WIDTH),
                  full(alog_row), full(dtb_row), full(norm_w)],
        out_specs=row(GDN_WIDTH),
        out_shape=jax.ShapeDtypeStruct((batch * seq, GDN_WIDTH), BF16),
        scratch_shapes=[pltpu.VMEM((GDN_HEADS, GDN_DK, GDN_DV), F32)],
        compiler_params=pltpu.CompilerParams(
            dimension_semantics=("arbitrary", "arbitrary"), vmem_limit_bytes=VMEM_LIMIT),
        name="gated_deltanet",
    )(gq, gk, gv, hab, gz, alog_row, dtb_row, norm_w)


def _attn_kernel(q_ref, k_ref, v_ref, dz_ref, nw_ref, lq1_ref, lk1_ref, lq2_ref, lk2_ref,
                 o_ref, ke_scr, m_scr, acc_scr, p_scr, alpha_scr):
    tq, tk, rows = ATT_TQ, ATT_TK, ATT_ROWS
    n_half = tq // rows
    seq = k_ref.shape[0]
    h = pl.program_id(1)
    i = pl.program_id(2)
    slope = jnp.float32(ALIBI_SLOPES[DIFF_HEADS - 1] * LOG2E)
    for hh in range(DIFF_HEADS - 1):
        slope = jnp.where(h == hh, jnp.float32(ALIBI_SLOPES[hh] * LOG2E), slope)

    lane = lax.broadcasted_iota(jnp.int32, (tk, LANES), 1)
    feat_lane = (lane - DIFF_DH, lane)

    @pl.when(i == 0)
    def _():
        def build(blk, carry):
            r0 = pl.multiple_of(blk * tk, tk)
            kf = k_ref[pl.ds(r0, tk), :].astype(F32)
            pos = (lax.broadcasted_iota(jnp.int32, (tk, LANES), 0) + r0).astype(F32) * slope
            p_hi = pos.astype(BF16).astype(F32)
            r1 = pos - p_hi
            p_mid = r1.astype(BF16).astype(F32)
            p_lo = r1 - p_mid
            for mp in range(2):
                fl = feat_lane[mp]
                feat = jnp.where(fl == 0, p_hi, jnp.where(fl == 1, p_mid,
                                                          jnp.where(fl == 2, p_lo, 0.0)))
                own = (lane < DIFF_DH) if mp == 0 else (lane >= DIFF_DH)
                ke_scr[mp, pl.ds(r0, tk), :] = jnp.where(own, kf, feat).astype(BF16)
            return carry

        lax.fori_loop(0, seq // tk, build, 0)

    qf = q_ref[...].astype(F32)
    qmaps = []
    for mp in range(2):
        fl = feat_lane[mp]
        own = (lane < DIFF_DH) if mp == 0 else (lane >= DIFF_DH)
        ones = jnp.where((fl >= 0) & (fl < 3), 1.0, 0.0)
        qmaps.append(jnp.where(own, qf, ones).astype(BF16))
    chains = [(mp, r) for mp in range(2) for r in range(n_half)]
    n_chains = len(chains)

    m_scr[...] = jnp.full(m_scr.shape, MASK_NEG, F32)
    acc_scr[...] = jnp.zeros_like(acc_scr)

    row_t = lax.broadcasted_iota(jnp.int32, (rows, LANES), 0)
    col_t = lax.broadcasted_iota(jnp.int32, (rows, LANES), 1)

    def load_v_ext(j):
        vb = v_ref[pl.ds(pl.multiple_of(j * tk, tk), tk), :]
        return jnp.concatenate([vb, jnp.ones((tk, LANES), BF16)], axis=1)

    def scores_step(c, j, slot, masked):
        mp, r = chains[c]
        nt = tk // LANES
        qc = qmaps[mp][r * rows:(r + 1) * rows]
        kb = ke_scr[mp, pl.ds(pl.multiple_of(j * tk, tk), tk), :]
        s = _dot_nt(qc, kb)
        tiles = []
        for tl in range(nt):
            st = s[:, tl * LANES:(tl + 1) * LANES]
            if masked and (tl + 1) * LANES - 1 > r * rows:
                st = jnp.where(col_t + tl * LANES <= row_t + r * rows, st, MASK_NEG)
            tiles.append(st)
        mx = tiles[0]
        for st in tiles[1:]:
            mx = jnp.maximum(mx, st)
        m_prev = m_scr[c]
        m_new = jnp.maximum(m_prev, jnp.max(mx, axis=-1, keepdims=True))
        for tl in range(nt):
            p_scr[slot, c, :, tl * LANES:(tl + 1) * LANES] = (
                jnp.exp2(tiles[tl] - m_new).astype(BF16))
        alpha_scr[slot, c] = jnp.exp2(m_prev - m_new)
        m_scr[c] = m_new

    def pv_step(c, v_ext, slot):
        alpha = alpha_scr[slot, c]
        pv = _dot(p_scr[slot, c], v_ext)
        acc_scr[c] = acc_scr[c] * jnp.concatenate([alpha, alpha], axis=1) + pv

    def scores_block(j, slot, masked=False):
        for c in range(n_chains):
            scores_step(c, j, slot, masked)

    def pv_block(j, slot):
        v_ext = load_v_ext(j)
        for c in range(n_chains):
            pv_step(c, v_ext, slot)

    scores_block(i, 0, masked=True)

    def pair(u, carry):
        j = i - 2 * u - 1
        scores_block(j, 1)
        pv_block(j + 1, 0)
        scores_block(j - 1, 0)
        pv_block(j, 1)
        return carry

    lax.fori_loop(0, i // 2, pair, 0)

    @pl.when(i % 2 == 1)
    def _():
        scores_block(0, 1)
        pv_block(1, 0)
        pv_block(0, 1)

    @pl.when(i % 2 == 0)
    def _():
        pv_block(0, 0)


    def map_out(mp):
        parts = []
        for r in range(n_half):
            acc = acc_scr[mp * n_half + r]
            parts.append(acc[:, :DIFF_DV] / acc[:, DIFF_DV:])
        return jnp.concatenate(parts, axis=0)

    lam = (jnp.exp(jnp.sum(lq1_ref[...] * lk1_ref[...], axis=-1, keepdims=True))
           - jnp.exp(jnp.sum(lq2_ref[...] * lk2_ref[...], axis=-1, keepdims=True))
           + LAM_INIT)
    o = map_out(0) - lam * map_out(1)
    o = o * lax.rsqrt(jnp.mean(o * o, axis=-1, keepdims=True) + SUBLN_EPS) * nw_ref[...]
    o = o * (1.0 - LAM_INIT) * _silu(dz_ref[...].astype(F32))
    o_ref[...] = o.astype(o_ref.dtype)


def _diff_attention(dq, dk, dv, dz, norm_w, lq1, lk1, lq2, lk2, batch, seq):
    tq = ATT_TQ
    nq = seq // tq
    n_chains = 2 * (tq // ATT_ROWS)
    qspec = pl.BlockSpec((tq, DIFF_DV), lambda b, h, i: (b * nq + i, h))
    kvspec = pl.BlockSpec((seq, DIFF_DV), lambda b, h, i: (b, h))
    full = lambda a: pl.BlockSpec(a.shape, lambda b, h, i: (0, 0))
    return pl.pallas_call(
        _attn_kernel,
        grid=(batch, DIFF_HEADS, nq),
        in_specs=[qspec, kvspec, kvspec, qspec, full(norm_w), full(lq1), full(lk1),
                  full(lq2), full(lk2)],
        out_specs=qspec,
        out_shape=jax.ShapeDtypeStruct((batch * seq, DIFF_WIDTH), BF16),
        scratch_shapes=[pltpu.VMEM((2, seq, DIFF_DV), BF16),
                        pltpu.VMEM((n_chains, ATT_ROWS, LANES), F32),
                        pltpu.VMEM((n_chains, ATT_ROWS, 2 * LANES), F32),
                        pltpu.VMEM((2, n_chains, ATT_ROWS, ATT_TK), BF16),
                        pltpu.VMEM((2, n_chains, ATT_ROWS, LANES), F32)],
        compiler_params=pltpu.CompilerParams(
            dimension_semantics=("arbitrary", "arbitrary", "arbitrary"),
            vmem_limit_bytes=VMEM_LIMIT),
        name="diff_attention",
    )(dq, dk, dv, dz, norm_w, lq1, lk1, lq2, lk2)


def _out_kernel(x_ref, oa_ref, ob_ref, mab_ref, wa_ref, wb_ref, wo_ref, g_ref, b_ref, y_ref):
    ya = _dot(oa_ref[...], wa_ref[...])
    yb = _dot(ob_ref[...], wb_ref[...])
    merged = (_sigmoid(mab_ref[:, :D_MODEL].astype(F32)) * ya
              + _sigmoid(mab_ref[:, D_MODEL:].astype(F32)) * yb)
    y = _dot(merged.astype(BF16), wo_ref[...])
    z = DEEPNORM_ALPHA * x_ref[...] + y
    mu = jnp.mean(z, axis=-1, keepdims=True)
    zc = z - mu
    var = jnp.mean(zc * zc, axis=-1, keepdims=True)
    y_ref[...] = zc * lax.rsqrt(var + LN_EPS) * g_ref[...] + b_ref[...]


def _output(x2, o_a, o_b, mab, wa, wb, wo, ln_g, ln_b):
    m = x2.shape[0]
    tm = OUT_TM
    row = lambda width: pl.BlockSpec((tm, width), lambda i: (i, 0))
    full = lambda a: pl.BlockSpec(a.shape, lambda i: (0, 0))
    return pl.pallas_call(
        _out_kernel,
        grid=(m // tm,),
        in_specs=[row(D_MODEL), row(GDN_WIDTH), row(DIFF_WIDTH), row(2 * D_MODEL),
                  full(wa), full(wb), full(wo), full(ln_g), full(ln_b)],
        out_specs=row(D_MODEL),
        out_shape=jax.ShapeDtypeStruct((m, D_MODEL), F32),
        compiler_params=pltpu.CompilerParams(
            dimension_semantics=("arbitrary",), vmem_limit_bytes=VMEM_LIMIT),
        name="merge_out_ln",
    )(x2, o_a, o_b, mab, wa, wb, wo, ln_g, ln_b)


def _lane_row(v):
    return jnp.zeros((1, LANES), F32).at[0, :v.shape[0]].set(v.astype(F32))


def kernel(x, w_in, conv_w, a_log, dt_bias, gdn_norm_w, w_up_a, lambda_q1, lambda_k1,
           lambda_q2, lambda_k2, diff_norm_w, w_up_b, w_out, ln_g, ln_b):
    batch, seq, d = x.shape
    x2 = x.reshape(batch * seq, d)
    layer = 0
    w = w_in[layer].astype(BF16)
    ab0 = 2 * GDN_QK + GDN_WIDTH
    ab1 = ab0 + 2 * GDN_HEADS
    w_main = jnp.concatenate([w[:, :ab0], w[:, ab1:]], axis=1)
    w_ab = jnp.pad(w[:, ab0:ab1], ((0, 0), (0, LANES - 2 * GDN_HEADS)))

    gq, gk, gv, hab, gz, dq, dk, dv, dz, mab = _project(
        x2, w_main, w_ab, conv_w[layer].astype(F32), seq)

    o_a = _gdn(gq, gk, gv, hab, gz, _lane_row(a_log[layer]), _lane_row(dt_bias[layer]),
               gdn_norm_w[layer].reshape(1, GDN_DV).astype(F32), batch, seq)

    o_b = _diff_attention(
        dq, dk, dv, dz, diff_norm_w[layer].reshape(1, DIFF_DV).astype(F32),
        lambda_q1[layer].reshape(1, DIFF_DH).astype(F32),
        lambda_k1[layer].reshape(1, DIFF_DH).astype(F32),
        lambda_q2[layer].reshape(1, DIFF_DH).astype(F32),
        lambda_k2[layer].reshape(1, DIFF_DH).astype(F32), batch, seq)

    y = _output(x2, o_a, o_b, mab, w_up_a[layer].astype(BF16), w_up_b[layer].astype(BF16),
                w_out[layer].astype(BF16), ln_g[layer].reshape(1, d).astype(F32),
                ln_b[layer].reshape(1, d).astype(F32))
    return y.reshape(batch, seq, d)
```

```python
import functools
import math

import jax
import jax.numpy as jnp
from jax import lax
from jax.experimental import pallas as pl
from jax.experimental.pallas import tpu as pltpu

F32 = jnp.float32
BF16 = jnp.bfloat16

D_MODEL = 1024
GDN_HEADS = 4
GDN_DK = 128
GDN_DV = 128
GDN_QK = GDN_HEADS * GDN_DK
GDN_WIDTH = GDN_HEADS * GDN_DV
CONV_K = 4
DIFF_HEADS = 4
DIFF_DH = 64
DIFF_DV = 2 * DIFF_DH
DIFF_QK = DIFF_HEADS * 2 * DIFF_DH
DIFF_WIDTH = DIFF_HEADS * DIFF_DV
NORM_EPS = 1e-6
SUBLN_EPS = 1e-5
LN_EPS = 1e-5
DEPTH = 1
DEEPNORM_ALPHA = (2.0 * DEPTH) ** 0.25
LAM_INIT = 0.8 - 0.6 * math.exp(-0.3 * 0)
ALIBI_SLOPES = tuple(2.0 ** (-8.0 * (i + 1) / DIFF_HEADS) for i in range(DIFF_HEADS))
LOG2E = math.log2(math.e)

LANES = 128
CONV_HIST = 8
VMEM_LIMIT = 56 * 1024 * 1024

PROJ_TM = 256
GDN_T = 256
GDN_INV_BASE = 16
ATT_TQ = 512
ATT_TK = 512
ATT_ROWS = 256
OUT_TM = 512
MASK_NEG = -1e30
ATT_SKIP_EXP2 = 150.0
ATT_SKIP_SLACK = 1.0


def _sigmoid(x):
    return 1.0 / (1.0 + jnp.exp(-x))


def _silu(x):
    return x * _sigmoid(x)


def _dot(a, b):
    return jnp.dot(a, b, preferred_element_type=F32)


def _dot_nt(a, b):
    return lax.dot_general(a, b, (((1,), (1,)), ((), ())), preferred_element_type=F32)


def _proj_kernel(tiles_per_seq, x_ref, w_ref, wab_ref, cw_ref, gq_ref, gk_ref, gv_ref,
                 hab_ref, gz_ref, dq_ref, dk_ref, dv_ref, dz_ref, mab_ref, hist_scr, xb_scr):
    tm = PROJ_TM

    @pl.when(pl.program_id(0) % tiles_per_seq == 0)
    def _():
        hist_scr[...] = jnp.zeros_like(hist_scr)

    xb_scr[...] = x_ref[...].astype(BF16)

    def mm(c0, width):
        return _dot(xb_scr[...], w_ref[:, c0:c0 + width])

    piece = 2 * GDN_DK

    def conv_silu(c0):
        cols = slice(c0, c0 + piece)
        acc = mm(c0, piece)
        ext = jnp.concatenate([hist_scr[:, cols], acc], axis=0)
        hist_scr[:, cols] = acc[tm - CONV_HIST:]
        y = None
        for j in range(CONV_K):
            r0 = CONV_HIST - (CONV_K - 1) + j
            term = cw_ref[j:j + 1, cols] * ext[r0:r0 + tm]
            y = term if y is None else y + term
        return _silu(y)

    def l2norm_heads(y, scale):
        parts = []
        for h in range(piece // GDN_DK):
            v = y[:, h * GDN_DK:(h + 1) * GDN_DK]
            inv = lax.rsqrt(jnp.sum(v * v, axis=-1, keepdims=True) + NORM_EPS)
            parts.append(v * (inv * scale) if scale != 1.0 else v * inv)
        return jnp.concatenate(parts, axis=1)

    def plain(out_ref, c0, scale=None):
        for j in range(out_ref.shape[1] // 512):
            acc = mm(c0 + j * 512, 512)
            if scale is not None:
                acc = acc * scale
            out_ref[:, j * 512:(j + 1) * 512] = acc.astype(out_ref.dtype)

    for half in range(2):
        c0 = half * piece
        gq_ref[:, c0:c0 + piece] = l2norm_heads(conv_silu(c0), GDN_DK ** -0.5).astype(BF16)
        if half == 0:
            plain(gz_ref, 1536)
        else:
            plain(dq_ref, 2048, DIFF_DH ** -0.5 * LOG2E)
    for half in range(2):
        c0 = half * piece
        gk_ref[:, c0:c0 + piece] = l2norm_heads(conv_silu(GDN_QK + c0), 1.0).astype(BF16)
        plain(dk_ref if half == 0 else dv_ref, 2560 + half * 512)
    for half in range(2):
        c0 = half * piece
        gv_ref[:, c0:c0 + piece] = conv_silu(2 * GDN_QK + c0).astype(BF16)
        if half == 0:
            plain(dz_ref, 3584)
    hab_ref[...] = _dot(xb_scr[...], wab_ref[...])
    plain(mab_ref, 4096)


def _project(x2, w_main, w_ab, conv_w, seq):
    m = x2.shape[0]
    tm = PROJ_TM
    row = lambda width: pl.BlockSpec((tm, width), lambda i: (i, 0))
    full = lambda a: pl.BlockSpec(a.shape, lambda i: (0, 0))
    out_shape = (
        jax.ShapeDtypeStruct((m, GDN_QK), BF16),
        jax.ShapeDtypeStruct((m, GDN_QK), BF16),
        jax.ShapeDtypeStruct((m, GDN_WIDTH), BF16),
        jax.ShapeDtypeStruct((m, LANES), F32),
        jax.ShapeDtypeStruct((m, 512), BF16),
        jax.ShapeDtypeStruct((m, 512), BF16),
        jax.ShapeDtypeStruct((m, 512), BF16),
        jax.ShapeDtypeStruct((m, 512), BF16),
        jax.ShapeDtypeStruct((m, 512), BF16),
        jax.ShapeDtypeStruct((m, 2048), BF16),
    )
    return pl.pallas_call(
        functools.partial(_proj_kernel, seq // tm),
        grid=(m // tm,),
        in_specs=[row(D_MODEL), full(w_main), full(w_ab), full(conv_w)],
        out_specs=tuple(row(s.shape[1]) for s in out_shape),
        out_shape=out_shape,
        scratch_shapes=[pltpu.VMEM((CONV_HIST, 3 * GDN_QK), F32),
                        pltpu.VMEM((tm, D_MODEL), BF16)],
        compiler_params=pltpu.CompilerParams(
            dimension_semantics=("arbitrary",), vmem_limit_bytes=VMEM_LIMIT),
        name="in_proj",
    )(x2, w_main, w_ab, conv_w)


def _gdn_kernel(gq_ref, gk_ref, gv_ref, hab_ref, gz_ref, alog_ref, dtb_ref, nw_ref, o_ref,
                st_scr):
    t = GDN_T
    heads = range(GDN_HEADS)

    @pl.when(pl.program_id(1) == 0)
    def _():
        st_scr[...] = jnp.zeros_like(st_scr)

    hab = hab_ref[...]
    xg = hab + dtb_ref[...]
    y_sp = jnp.exp(-jnp.abs(xg))
    u_sp = 1.0 + y_sp
    softplus = jnp.maximum(xg, 0.0) + (jnp.log(u_sp) - ((u_sp - 1.0) - y_sp) / u_sp)
    g_full = -jnp.exp(alog_ref[...]) * softplus
    beta_full = _sigmoid(hab)

    ri = lax.broadcasted_iota(jnp.int32, (t, t), 0)
    ci = lax.broadcasted_iota(jnp.int32, (t, t), 1)
    causal = ri >= ci
    strict = ri > ci
    eye = jnp.where(ri == ci, 1.0, 0.0).astype(F32)

    tri = jnp.where(causal, 1.0, 0.0).astype(BF16)
    g_hi = g_full.astype(BF16)
    g_r1 = g_full - g_hi.astype(F32)
    g_mid = g_r1.astype(BF16)
    g_lo = (g_r1 - g_mid.astype(F32)).astype(BF16)
    gc = _dot(tri, g_hi) + _dot(tri, g_mid) + _dot(tri, g_lo)
    gc_t = gc.T
    eg = jnp.exp(gc)
    g_last = gc[t - 1:t, :]
    k_dec = jnp.exp(g_last - gc)
    eg_last = jnp.exp(g_last)

    qn_b = [gq_ref[:, h * GDN_DK:(h + 1) * GDN_DK] for h in heads]
    kn_b = [gk_ref[:, h * GDN_DK:(h + 1) * GDN_DK] for h in heads]
    qn = [qn_b[h].astype(F32) for h in heads]
    kn = [kn_b[h].astype(F32) for h in heads]
    vv = [gv_ref[:, h * GDN_DV:(h + 1) * GDN_DV].astype(F32) for h in heads]

    bcol = [beta_full[:, GDN_HEADS + h:GDN_HEADS + h + 1] for h in heads]
    egc = [eg[:, h:h + 1] for h in heads]
    kb = [kn[h] * bcol[h] for h in heads]
    kq = [_dot_nt(jnp.concatenate([kb[h].astype(BF16), qn_b[h]], axis=0), kn_b[h])
          for h in heads]
    decay = []
    for h in heads:
        gd = gc[:, h:h + 1] - gc_t[h:h + 1, :]
        decay.append(jnp.where(causal, jnp.exp(jnp.where(causal, gd, 0.0)), 0.0))
    nmat = [jnp.where(strict, -(kq[h][:t] * decay[h]), 0.0) for h in heads]
    amat = [(kq[h][t:] * decay[h]).astype(BF16) for h in heads]
    rhs = [jnp.concatenate([vv[h] * bcol[h], kb[h] * egc[h]], axis=1).astype(BF16)
           for h in heads]
    qe = [qn[h] * egc[h] for h in heads]
    kd_t = [(kn[h] * k_dec[:, h:h + 1]).T.astype(BF16) for h in heads]

    def same_block(b):
        sh = int(math.log2(b))
        return (ri >> sh) == (ci >> sh)

    blk = GDN_INV_BASE
    in_blk = same_block(blk)
    n0 = [jnp.where(in_blk, nmat[h], 0.0) for h in heads]
    x = [eye + n0[h] for h in heads]
    pw = [n0[h].astype(BF16) for h in heads]
    for _ in range(int(math.log2(blk)) - 1):
        p32 = [_dot(pw[h], pw[h]) for h in heads]
        pw = [p32[h].astype(BF16) for h in heads]
        x = [x[h] + _dot(x[h].astype(BF16), pw[h]) for h in heads]
    while blk < t:
        in_big = same_block(2 * blk)
        n_off = [jnp.where(in_big, jnp.where(in_blk, 0.0, nmat[h]), 0.0).astype(BF16)
                 for h in heads]
        xb = [x[h].astype(BF16) for h in heads]
        xn = [_dot(xb[h], n_off[h]).astype(BF16) for h in heads]
        x = [x[h] + _dot(xn[h], xb[h]) for h in heads]
        in_blk = in_big
        blk *= 2
    sol = [_dot(x[h].astype(BF16), rhs[h]) for h in heads]

    s_prev = [st_scr[h] for h in heads]
    ws = [_dot(jnp.concatenate([sol[h][:, GDN_DV:], qe[h]], axis=0).astype(BF16),
               s_prev[h].astype(BF16)) for h in heads]
    v_new = [(sol[h][:, :GDN_DV] - ws[h][:t]).astype(BF16) for h in heads]
    o_l = [ws[h][t:] + _dot(amat[h], v_new[h]) for h in heads]
    for h in heads:
        st_scr[h] = s_prev[h] * eg_last[:, h:h + 1] + _dot(kd_t[h], v_new[h])
    nw = nw_ref[...]
    for h in heads:
        o = o_l[h]
        o = o * lax.rsqrt(jnp.mean(o * o, axis=-1, keepdims=True) + NORM_EPS) * nw
        gate = _silu(gz_ref[:, h * GDN_DV:(h + 1) * GDN_DV].astype(F32))
        o_ref[:, h * GDN_DV:(h + 1) * GDN_DV] = (o * gate).astype(o_ref.dtype)


def _gdn(gq, gk, gv, hab, gz, alog_row, dtb_row, norm_w, batch, seq):
    t = GDN_T
    nt = seq // t
    row = lambda width: pl.BlockSpec((t, width), lambda b, s: (b * nt + s, 0))
    full = lambda a: pl.BlockSpec(a.shape, lambda b, s: (0, 0))
    return pl.pallas_call(
        _gdn_kernel,
        grid=(batch, nt),
        in_specs=[row(GDN_QK), row(GDN_QK), row(GDN_WIDTH), row(LANES), row(GDN_WIDTH),
                  full(alog_row), full(dtb_row), full(norm_w)],
        out_specs=row(GDN_WIDTH),
        out_shape=jax.ShapeDtypeStruct((batch * seq, GDN_WIDTH), BF16),
        scratch_shapes=[pltpu.VMEM((GDN_HEADS, GDN_DK, GDN_DV), F32)],
        compiler_params=pltpu.CompilerParams(
            dimension_semantics=("arbitrary", "arbitrary"), vmem_limit_bytes=VMEM_LIMIT),
        name="gated_deltanet",
    )(gq, gk, gv, hab, gz, alog_row, dtb_row, norm_w)


def _attn_kernel(q_ref, k_ref, v_ref, dz_ref, nw_ref, lq1_ref, lk1_ref, lq2_ref, lk2_ref,
                 o_ref, ke_scr, m_scr, acc_scr, p_scr, alpha_scr, kmax_scr):
    tq, tk, rows = ATT_TQ, ATT_TK, ATT_ROWS
    n_half = tq // rows
    seq = k_ref.shape[0]
    h = pl.program_id(1)
    i = pl.program_id(2)
    slope = jnp.float32(ALIBI_SLOPES[DIFF_HEADS - 1] * LOG2E)
    for hh in range(DIFF_HEADS - 1):
        slope = jnp.where(h == hh, jnp.float32(ALIBI_SLOPES[hh] * LOG2E), slope)

    lane = lax.broadcasted_iota(jnp.int32, (tk, LANES), 1)
    feat_lane = (lane - DIFF_DH, lane)

    def half_norm_max(vf):
        sq = vf * vf
        lo = jnp.sum(jnp.where(lane < DIFF_DH, sq, 0.0), axis=-1, keepdims=True)
        hi = jnp.sum(jnp.where(lane >= DIFF_DH, sq, 0.0), axis=-1, keepdims=True)
        return jnp.sqrt(jnp.max(jnp.maximum(lo, hi), axis=0, keepdims=True))

    @pl.when(i == 0)
    def _():
        kmax_scr[...] = jnp.zeros_like(kmax_scr)

        def build(blk, carry):
            r0 = pl.multiple_of(blk * tk, tk)
            kf = k_ref[pl.ds(r0, tk), :].astype(F32)
            kmax_scr[...] = jnp.maximum(kmax_scr[...], half_norm_max(kf))
            pos = (lax.broadcasted_iota(jnp.int32, (tk, LANES), 0) + r0).astype(F32) * slope
            p_hi = pos.astype(BF16).astype(F32)
            r1 = pos - p_hi
            p_mid = r1.astype(BF16).astype(F32)
            p_lo = r1 - p_mid
            for mp in range(2):
                fl = feat_lane[mp]
                feat = jnp.where(fl == 0, p_hi, jnp.where(fl == 1, p_mid,
                                                          jnp.where(fl == 2, p_lo, 0.0)))
                own = (lane < DIFF_DH) if mp == 0 else (lane >= DIFF_DH)
                ke_scr[mp, pl.ds(r0, tk), :] = jnp.where(own, kf, feat).astype(BF16)
            return carry

        lax.fori_loop(0, seq // tk, build, 0)

    qf = q_ref[...].astype(F32)
    qk_bound = half_norm_max(qf) * kmax_scr[0:1, 0:1]
    qmaps = []
    for mp in range(2):
        fl = feat_lane[mp]
        own = (lane < DIFF_DH) if mp == 0 else (lane >= DIFF_DH)
        ones = jnp.where((fl >= 0) & (fl < 3), 1.0, 0.0)
        qmaps.append(jnp.where(own, qf, ones).astype(BF16))
    chains = [(mp, r) for mp in range(2) for r in range(n_half)]
    n_chains = len(chains)

    m_scr[...] = jnp.full(m_scr.shape, MASK_NEG, F32)
    acc_scr[...] = jnp.zeros_like(acc_scr)

    row_t = lax.broadcasted_iota(jnp.int32, (rows, LANES), 0)
    col_t = lax.broadcasted_iota(jnp.int32, (rows, LANES), 1)

    def load_v_ext(j):
        vb = v_ref[pl.ds(pl.multiple_of(j * tk, tk), tk), :]
        return jnp.concatenate([vb, jnp.ones((tk, LANES), BF16)], axis=1)

    def scores_step(c, j, slot, masked):
        mp, r = chains[c]
        nt = tk // LANES
        qc = qmaps[mp][r * rows:(r + 1) * rows]
        kb = ke_scr[mp, pl.ds(pl.multiple_of(j * tk, tk), tk), :]
        s = _dot_nt(qc, kb)
        tiles = []
        for tl in range(nt):
            st = s[:, tl * LANES:(tl + 1) * LANES]
            if masked and (tl + 1) * LANES - 1 > r * rows:
                st = jnp.where(col_t + tl * LANES <= row_t + r * rows, st, MASK_NEG)
            tiles.append(st)
        mx = tiles[0]
        for st in tiles[1:]:
            mx = jnp.maximum(mx, st)
        m_prev = m_scr[c]
        m_new = jnp.maximum(m_prev, jnp.max(mx, axis=-1, keepdims=True))
        for tl in range(nt):
            p_scr[slot, c, :, tl * LANES:(tl + 1) * LANES] = (
                jnp.exp2(tiles[tl] - m_new).astype(BF16))
        alpha_scr[slot, c] = jnp.exp2(m_prev - m_new)
        m_scr[c] = m_new

    def pv_step(c, v_ext, slot):
        alpha = alpha_scr[slot, c]
        pv = _dot(p_scr[slot, c], v_ext)
        acc_scr[c] = acc_scr[c] * jnp.concatenate([alpha, alpha], axis=1) + pv

    def scores_block(j, slot, masked=False):
        for c in range(n_chains):
            scores_step(c, j, slot, masked)

    def pv_block(j, slot):
        v_ext = load_v_ext(j)
        for c in range(n_chains):
            pv_step(c, v_ext, slot)

    scores_block(i, 0, masked=True)

    m_min = jnp.min(jnp.min(m_scr[...], axis=0), axis=0, keepdims=True)[:, 0:1]
    need = m_min - (ATT_SKIP_EXP2 + ATT_SKIP_SLACK) - qk_bound
    j_first = jnp.ceil(need / (slope * tk) - (tk - 1) / tk)
    n_arr = jnp.clip(i.astype(F32) - jnp.maximum(j_first, 0.0), 0.0, i.astype(F32))
    n = jnp.max(n_arr).astype(jnp.int32)
    last = i - n

    def pair(u, carry):
        j = i - 2 * u - 1
        scores_block(j, 1)
        pv_block(j + 1, 0)
        scores_block(j - 1, 0)
        pv_block(j, 1)
        return carry

    lax.fori_loop(0, n // 2, pair, 0)

    @pl.when(n % 2 == 1)
    def _():
        scores_block(last, 1)
        pv_block(last + 1, 0)
        pv_block(last, 1)

    @pl.when(n % 2 == 0)
    def _():
        pv_block(last, 0)


    def map_out(mp):
        parts = []
        for r in range(n_half):
            acc = acc_scr[mp * n_half + r]
            parts.append(acc[:, :DIFF_DV] / acc[:, DIFF_DV:])
        return jnp.concatenate(parts, axis=0)

    lam = (jnp.exp(jnp.sum(lq1_ref[...] * lk1_ref[...], axis=-1, keepdims=True))
           - jnp.exp(jnp.sum(lq2_ref[...] * lk2_ref[...], axis=-1, keepdims=True))
           + LAM_INIT)
    o = map_out(0) - lam * map_out(1)
    o = o * lax.rsqrt(jnp.mean(o * o, axis=-1, keepdims=True) + SUBLN_EPS) * nw_ref[...]
    o = o * (1.0 - LAM_INIT) * _silu(dz_ref[...].astype(F32))
    o_ref[...] = o.astype(o_ref.dtype)


def _diff_attention(dq, dk, dv, dz, norm_w, lq1, lk1, lq2, lk2, batch, seq):
    tq = ATT_TQ
    nq = seq // tq
    n_chains = 2 * (tq // ATT_ROWS)
    qspec = pl.BlockSpec((tq, DIFF_DV), lambda b, h, i: (b * nq + i, h))
    kvspec = pl.BlockSpec((seq, DIFF_DV), lambda b, h, i: (b, h))
    full = lambda a: pl.BlockSpec(a.shape, lambda b, h, i: (0, 0))
    return pl.pallas_call(
        _attn_kernel,
        grid=(batch, DIFF_HEADS, nq),
        in_specs=[qspec, kvspec, kvspec, qspec, full(norm_w), full(lq1), full(lk1),
                  full(lq2), full(lk2)],
        out_specs=qspec,
        out_shape=jax.ShapeDtypeStruct((batch * seq, DIFF_WIDTH), BF16),
        scratch_shapes=[pltpu.VMEM((2, seq, DIFF_DV), BF16),
                        pltpu.VMEM((n_chains, ATT_ROWS, LANES), F32),
                        pltpu.VMEM((n_chains, ATT_ROWS, 2 * LANES), F32),
                        pltpu.VMEM((2, n_chains, ATT_ROWS, ATT_TK), BF16),
                        pltpu.VMEM((2, n_chains, ATT_ROWS, LANES), F32),
                        pltpu.VMEM((8, LANES), F32)],
        compiler_params=pltpu.CompilerParams(
            dimension_semantics=("arbitrary", "arbitrary", "arbitrary"),
            vmem_limit_bytes=VMEM_LIMIT),
        name="diff_attention",
    )(dq, dk, dv, dz, norm_w, lq1, lk1, lq2, lk2)


def _out_kernel(x_ref, oa_ref, ob_ref, mab_ref, wa_ref, wb_ref, wo_ref, g_ref, b_ref, y_ref):
    ya = _dot(oa_ref[...], wa_ref[...])
    yb = _dot(ob_ref[...], wb_ref[...])
    merged = (_sigmoid(mab_ref[:, :D_MODEL].astype(F32)) * ya
              + _sigmoid(mab_ref[:, D_MODEL:].astype(F32)) * yb)
    y = _dot(merged.astype(BF16), wo_ref[...])
    z = DEEPNORM_ALPHA * x_ref[...] + y
    mu = jnp.mean(z, axis=-1, keepdims=True)
    zc = z - mu
    var = jnp.mean(zc * zc, axis=-1, keepdims=True)
    y_ref[...] = zc * lax.rsqrt(var + LN_EPS) * g_ref[...] + b_ref[...]


def _output(x2, o_a, o_b, mab, wa, wb, wo, ln_g, ln_b):
    m = x2.shape[0]
    tm = OUT_TM
    row = lambda width: pl.BlockSpec((tm, width), lambda i: (i, 0))
    full = lambda a: pl.BlockSpec(a.shape, lambda i: (0, 0))
    return pl.pallas_call(
        _out_kernel,
        grid=(m // tm,),
        in_specs=[row(D_MODEL), row(GDN_WIDTH), row(DIFF_WIDTH), row(2 * D_MODEL),
                  full(wa), full(wb), full(wo), full(ln_g), full(ln_b)],
        out_specs=row(D_MODEL),
        out_shape=jax.ShapeDtypeStruct((m, D_MODEL), F32),
        compiler_params=pltpu.CompilerParams(
            dimension_semantics=("arbitrary",), vmem_limit_bytes=VMEM_LIMIT),
        name="merge_out_ln",
    )(x2, o_a, o_b, mab, wa, wb, wo, ln_g, ln_b)


def _lane_row(v):
    return jnp.zeros((1, LANES), F32).at[0, :v.shape[0]].set(v.astype(F32))


def kernel(x, w_in, conv_w, a_log, dt_bias, gdn_norm_w, w_up_a, lambda_q1, lambda_k1,
           lambda_q2, lambda_k2, diff_norm_w, w_up_b, w_out, ln_g, ln_b):
    batch, seq, d = x.shape
    x2 = x.reshape(batch * seq, d)
    layer = 0
    w = w_in[layer].astype(BF16)
    ab0 = 2 * GDN_QK + GDN_WIDTH
    ab1 = ab0 + 2 * GDN_HEADS
    w_main = jnp.concatenate([w[:, :ab0], w[:, ab1:]], axis=1)
    w_ab = jnp.pad(w[:, ab0:ab1], ((0, 0), (0, LANES - 2 * GDN_HEADS)))

    gq, gk, gv, hab, gz, dq, dk, dv, dz, mab = _project(
        x2, w_main, w_ab, conv_w[layer].astype(F32), seq)

    o_a = _gdn(gq, gk, gv, hab, gz, _lane_row(a_log[layer]), _lane_row(dt_bias[layer]),
               gdn_norm_w[layer].reshape(1, GDN_DV).astype(F32), batch, seq)

    o_b = _diff_attention(
        dq, dk, dv, dz, diff_norm_w[layer].reshape(1, DIFF_DV).astype(F32),
        lambda_q1[layer].reshape(1, DIFF_DH).astype(F32),
        lambda_k1[layer].reshape(1, DIFF_DH).astype(F32),
        lambda_q2[layer].reshape(1, DIFF_DH).astype(F32),
        lambda_k2[layer].reshape(1, DIFF_DH).astype(F32), batch, seq)

    y = _output(x2, o_a, o_b, mab, w_up_a[layer].astype(BF16), w_up_b[layer].astype(BF16),
                w_out[layer].astype(BF16), ln_g[layer].reshape(1, d).astype(F32),
                ln_b[layer].reshape(1, d).astype(F32))
    return y.reshape(batch, seq, d)
```

```python
import functools
import math

import jax
import jax.numpy as jnp
from jax import lax
from jax.experimental import pallas as pl
from jax.experimental.pallas import tpu as pltpu

F32 = jnp.float32
BF16 = jnp.bfloat16

D_MODEL = 1024
GDN_HEADS = 4
GDN_DK = 128
GDN_DV = 128
GDN_QK = GDN_HEADS * GDN_DK
GDN_WIDTH = GDN_HEADS * GDN_DV
CONV_K = 4
DIFF_HEADS = 4
DIFF_DH = 64
DIFF_DV = 2 * DIFF_DH
DIFF_QK = DIFF_HEADS * 2 * DIFF_DH
DIFF_WIDTH = DIFF_HEADS * DIFF_DV
NORM_EPS = 1e-6
SUBLN_EPS = 1e-5
LN_EPS = 1e-5
DEPTH = 1
DEEPNORM_ALPHA = (2.0 * DEPTH) ** 0.25
LAM_INIT = 0.8 - 0.6 * math.exp(-0.3 * 0)
ALIBI_SLOPES = tuple(2.0 ** (-8.0 * (i + 1) / DIFF_HEADS) for i in range(DIFF_HEADS))
LOG2E = math.log2(math.e)

LANES = 128
CONV_HIST = 8
VMEM_LIMIT = 56 * 1024 * 1024

PROJ_TM = 256
PROJ_HEAD = 2 * GDN_QK + GDN_WIDTH
GDN_T = 256
GDN_INV_BASE = 16
ATT_TQ = 512
ATT_TK = 512
ATT_ROWS = 256
OUT_TM = 512
MASK_NEG = -1e30
ATT_SKIP_EXP2 = 150.0
ATT_SKIP_SLACK = 1.0


def _sigmoid(x):
    return 1.0 / (1.0 + jnp.exp(-x))


def _silu(x):
    return x * _sigmoid(x)


def _dot(a, b):
    return jnp.dot(a, b, preferred_element_type=F32)


def _dot_nt(a, b):
    return lax.dot_general(a, b, (((1,), (1,)), ((), ())), preferred_element_type=F32)


def _proj_kernel(tiles_per_seq, x_ref, wh_ref, wt_ref, cw_ref, gq_ref, gk_ref, gv_ref,
                 hab_ref, gz_ref, dq_ref, dk_ref, dv_ref, dz_ref, mab_ref, hist_scr, xb_scr):
    tm = PROJ_TM

    @pl.when(pl.program_id(0) % tiles_per_seq == 0)
    def _():
        hist_scr[...] = jnp.zeros_like(hist_scr)

    xb_scr[...] = x_ref[...].astype(BF16)

    def mm(c0, width):
        if c0 < PROJ_HEAD:
            return _dot(xb_scr[...], wh_ref[:, c0:c0 + width])
        return _dot(xb_scr[...], wt_ref[:, c0 - PROJ_HEAD:c0 - PROJ_HEAD + width])

    piece = 2 * GDN_DK

    def conv_silu(c0):
        cols = slice(c0, c0 + piece)
        acc = mm(c0, piece)
        ext = jnp.concatenate([hist_scr[:, cols], acc], axis=0)
        hist_scr[:, cols] = acc[tm - CONV_HIST:]
        y = None
        for j in range(CONV_K):
            r0 = CONV_HIST - (CONV_K - 1) + j
            term = cw_ref[j:j + 1, cols] * ext[r0:r0 + tm]
            y = term if y is None else y + term
        return _silu(y)

    def l2norm_heads(y, scale):
        parts = []
        for h in range(piece // GDN_DK):
            v = y[:, h * GDN_DK:(h + 1) * GDN_DK]
            inv = lax.rsqrt(jnp.sum(v * v, axis=-1, keepdims=True) + NORM_EPS)
            parts.append(v * (inv * scale) if scale != 1.0 else v * inv)
        return jnp.concatenate(parts, axis=1)

    def plain(out_ref, c0, scale=None):
        for j in range(out_ref.shape[1] // 512):
            acc = mm(c0 + j * 512, 512)
            if scale is not None:
                acc = acc * scale
            out_ref[:, j * 512:(j + 1) * 512] = acc.astype(out_ref.dtype)

    for half in range(2):
        c0 = half * piece
        gq_ref[:, c0:c0 + piece] = l2norm_heads(conv_silu(c0), GDN_DK ** -0.5).astype(BF16)
        if half == 0:
            plain(gz_ref, 1536)
        else:
            plain(dq_ref, 2048, DIFF_DH ** -0.5 * LOG2E)
    for half in range(2):
        c0 = half * piece
        gk_ref[:, c0:c0 + piece] = l2norm_heads(conv_silu(GDN_QK + c0), 1.0).astype(BF16)
        plain(dk_ref if half == 0 else dv_ref, 2560 + half * 512)
    for half in range(2):
        c0 = half * piece
        gv_ref[:, c0:c0 + piece] = conv_silu(2 * GDN_QK + c0).astype(BF16)
        if half == 0:
            plain(dz_ref, 3584)
    hab_ref[...] = _dot(xb_scr[...], wh_ref[:, PROJ_HEAD:PROJ_HEAD + LANES])
    plain(mab_ref, 4096)


def _project(x2, w_bf, w_tail, conv_w, seq):
    m = x2.shape[0]
    tm = PROJ_TM
    row = lambda width: pl.BlockSpec((tm, width), lambda i: (i, 0))
    full = lambda a: pl.BlockSpec(a.shape, lambda i: (0, 0))
    out_shape = (
        jax.ShapeDtypeStruct((m, GDN_QK), BF16),
        jax.ShapeDtypeStruct((m, GDN_QK), BF16),
        jax.ShapeDtypeStruct((m, GDN_WIDTH), BF16),
        jax.ShapeDtypeStruct((m, LANES), F32),
        jax.ShapeDtypeStruct((m, 512), BF16),
        jax.ShapeDtypeStruct((m, 512), BF16),
        jax.ShapeDtypeStruct((m, 512), BF16),
        jax.ShapeDtypeStruct((m, 512), BF16),
        jax.ShapeDtypeStruct((m, 512), BF16),
        jax.ShapeDtypeStruct((m, 2048), BF16),
    )
    return pl.pallas_call(
        functools.partial(_proj_kernel, seq // tm),
        grid=(m // tm,),
        in_specs=[row(D_MODEL),
                  pl.BlockSpec((D_MODEL, PROJ_HEAD + LANES), lambda i: (0, 0)),
                  full(w_tail), full(conv_w)],
        out_specs=tuple(row(s.shape[1]) for s in out_shape),
        out_shape=out_shape,
        scratch_shapes=[pltpu.VMEM((CONV_HIST, 3 * GDN_QK), F32),
                        pltpu.VMEM((tm, D_MODEL), BF16)],
        compiler_params=pltpu.CompilerParams(
            dimension_semantics=("arbitrary",), vmem_limit_bytes=VMEM_LIMIT),
        name="in_proj",
    )(x2, w_bf, w_tail, conv_w)


def _gdn_kernel(gq_ref, gk_ref, gv_ref, hab_ref, gz_ref, alog_ref, dtb_ref, nw_ref, o_ref,
                st_scr):
    t = GDN_T
    heads = range(GDN_HEADS)

    @pl.when(pl.program_id(1) == 0)
    def _():
        st_scr[...] = jnp.zeros_like(st_scr)

    hab = hab_ref[...]
    xg = hab + dtb_ref[...]
    y_sp = jnp.exp(-jnp.abs(xg))
    u_sp = 1.0 + y_sp
    softplus = jnp.maximum(xg, 0.0) + (jnp.log(u_sp) - ((u_sp - 1.0) - y_sp) / u_sp)
    g_full = -jnp.exp(alog_ref[...]) * softplus
    beta_full = _sigmoid(hab)

    ri = lax.broadcasted_iota(jnp.int32, (t, t), 0)
    ci = lax.broadcasted_iota(jnp.int32, (t, t), 1)
    causal = ri >= ci
    strict = ri > ci
    eye = jnp.where(ri == ci, 1.0, 0.0).astype(F32)

    tri = jnp.where(causal, 1.0, 0.0).astype(BF16)
    g_hi = g_full.astype(BF16)
    g_r1 = g_full - g_hi.astype(F32)
    g_mid = g_r1.astype(BF16)
    g_lo = (g_r1 - g_mid.astype(F32)).astype(BF16)
    gc = _dot(tri, g_hi) + _dot(tri, g_mid) + _dot(tri, g_lo)
    gc_t = gc.T
    eg = jnp.exp(gc)
    g_last = gc[t - 1:t, :]
    k_dec = jnp.exp(g_last - gc)
    eg_last = jnp.exp(g_last)

    qn_b = [gq_ref[:, h * GDN_DK:(h + 1) * GDN_DK] for h in heads]
    kn_b = [gk_ref[:, h * GDN_DK:(h + 1) * GDN_DK] for h in heads]
    qn = [qn_b[h].astype(F32) for h in heads]
    kn = [kn_b[h].astype(F32) for h in heads]
    vv = [gv_ref[:, h * GDN_DV:(h + 1) * GDN_DV].astype(F32) for h in heads]

    bcol = [beta_full[:, GDN_HEADS + h:GDN_HEADS + h + 1] for h in heads]
    egc = [eg[:, h:h + 1] for h in heads]
    kb = [kn[h] * bcol[h] for h in heads]
    kq = [_dot_nt(jnp.concatenate([kb[h].astype(BF16), qn_b[h]], axis=0), kn_b[h])
          for h in heads]
    decay = []
    for h in heads:
        gd = gc[:, h:h + 1] - gc_t[h:h + 1, :]
        decay.append(jnp.where(causal, jnp.exp(jnp.where(causal, gd, 0.0)), 0.0))
    nmat = [jnp.where(strict, -(kq[h][:t] * decay[h]), 0.0) for h in heads]
    amat = [(kq[h][t:] * decay[h]).astype(BF16) for h in heads]
    rhs = [jnp.concatenate([vv[h] * bcol[h], kb[h] * egc[h]], axis=1).astype(BF16)
           for h in heads]
    qe = [qn[h] * egc[h] for h in heads]
    kd_t = [(kn[h] * k_dec[:, h:h + 1]).T.astype(BF16) for h in heads]

    def same_block(b):
        sh = int(math.log2(b))
        return (ri >> sh) == (ci >> sh)

    blk = GDN_INV_BASE
    in_blk = same_block(blk)
    n0 = [jnp.where(in_blk, nmat[h], 0.0) for h in heads]
    x = [eye + n0[h] for h in heads]
    pw = [n0[h].astype(BF16) for h in heads]
    for _ in range(int(math.log2(blk)) - 1):
        p32 = [_dot(pw[h], pw[h]) for h in heads]
        pw = [p32[h].astype(BF16) for h in heads]
        x = [x[h] + _dot(x[h].astype(BF16), pw[h]) for h in heads]
    while blk < t:
        in_big = same_block(2 * blk)
        n_off = [jnp.where(in_big, jnp.where(in_blk, 0.0, nmat[h]), 0.0).astype(BF16)
                 for h in heads]
        xb = [x[h].astype(BF16) for h in heads]
        xn = [_dot(xb[h], n_off[h]).astype(BF16) for h in heads]
        x = [x[h] + _dot(xn[h], xb[h]) for h in heads]
        in_blk = in_big
        blk *= 2
    sol = [_dot(x[h].astype(BF16), rhs[h]) for h in heads]

    s_prev = [st_scr[h] for h in heads]
    ws = [_dot(jnp.concatenate([sol[h][:, GDN_DV:], qe[h]], axis=0).astype(BF16),
               s_prev[h].astype(BF16)) for h in heads]
    v_new = [(sol[h][:, :GDN_DV] - ws[h][:t]).astype(BF16) for h in heads]
    o_l = [ws[h][t:] + _dot(amat[h], v_new[h]) for h in heads]
    for h in heads:
        st_scr[h] = s_prev[h] * eg_last[:, h:h + 1] + _dot(kd_t[h], v_new[h])
    nw = nw_ref[...]
    for h in heads:
        o = o_l[h]
        o = o * lax.rsqrt(jnp.mean(o * o, axis=-1, keepdims=True) + NORM_EPS) * nw
        gate = _silu(gz_ref[:, h * GDN_DV:(h + 1) * GDN_DV].astype(F32))
        o_ref[:, h * GDN_DV:(h + 1) * GDN_DV] = (o * gate).astype(o_ref.dtype)


def _gdn(gq, gk, gv, hab, gz, alog_row, dtb_row, norm_w, batch, seq):
    t = GDN_T
    nt = seq // t
    row = lambda width: pl.BlockSpec((t, width), lambda b, s: (b * nt + s, 0))
    full = lambda a: pl.BlockSpec(a.shape, lambda b, s: (0, 0))
    return pl.pallas_call(
        _gdn_kernel,
        grid=(batch, nt),
        in_specs=[row(GDN_QK), row(GDN_QK), row(GDN_WIDTH), row(LANES), row(GDN_WIDTH),
                  full(alog_row), full(dtb_row), full(norm_w)],
        out_specs=row(GDN_WIDTH),
        out_shape=jax.ShapeDtypeStruct((batch * seq, GDN_WIDTH), BF16),
        scratch_shapes=[pltpu.VMEM((GDN_HEADS, GDN_DK, GDN_DV), F32)],
        compiler_params=pltpu.CompilerParams(
            dimension_semantics=("arbitrary", "arbitrary"), vmem_limit_bytes=VMEM_LIMIT),
        name="gated_deltanet",
    )(gq, gk, gv, hab, gz, alog_row, dtb_row, norm_w)


def _attn_kernel(q_ref, k_ref, v_ref, dz_ref, nw_ref, lq1_ref, lk1_ref, lq2_ref, lk2_ref,
                 o_ref, ke_scr, m_scr, acc_scr, p_scr, alpha_scr, kmax_scr):
    tq, tk, rows = ATT_TQ, ATT_TK, ATT_ROWS
    n_half = tq // rows
    seq = k_ref.shape[0]
    h = pl.program_id(1)
    i = pl.program_id(2)
    slope = jnp.float32(ALIBI_SLOPES[DIFF_HEADS - 1] * LOG2E)
    for hh in range(DIFF_HEADS - 1):
        slope = jnp.where(h == hh, jnp.float32(ALIBI_SLOPES[hh] * LOG2E), slope)

    lane = lax.broadcasted_iota(jnp.int32, (tk, LANES), 1)
    feat_lane = (lane - DIFF_DH, lane)

    def half_norm_max(vf):
        sq = vf * vf
        lo = jnp.sum(jnp.where(lane < DIFF_DH, sq, 0.0), axis=-1, keepdims=True)
        hi = jnp.sum(jnp.where(lane >= DIFF_DH, sq, 0.0), axis=-1, keepdims=True)
        return jnp.sqrt(jnp.max(jnp.maximum(lo, hi), axis=0, keepdims=True))

    @pl.when(i == 0)
    def _():
        kmax_scr[...] = jnp.zeros_like(kmax_scr)

        def build(blk, carry):
            r0 = pl.multiple_of(blk * tk, tk)
            kf = k_ref[pl.ds(r0, tk), :].astype(F32)
            kmax_scr[...] = jnp.maximum(kmax_scr[...], half_norm_max(kf))
            pos = (lax.broadcasted_iota(jnp.int32, (tk, LANES), 0) + r0).astype(F32) * slope
            p_hi = pos.astype(BF16).astype(F32)
            r1 = pos - p_hi
            p_mid = r1.astype(BF16).astype(F32)
            p_lo = r1 - p_mid
            for mp in range(2):
                fl = feat_lane[mp]
                feat = jnp.where(fl == 0, p_hi, jnp.where(fl == 1, p_mid,
                                                          jnp.where(fl == 2, p_lo, 0.0)))
                own = (lane < DIFF_DH) if mp == 0 else (lane >= DIFF_DH)
                ke_scr[mp, pl.ds(r0, tk), :] = jnp.where(own, kf, feat).astype(BF16)
            return carry

        lax.fori_loop(0, seq // tk, build, 0)

    qf = q_ref[...].astype(F32)
    qk_bound = half_norm_max(qf) * kmax_scr[0:1, 0:1]
    qmaps = []
    for mp in range(2):
        fl = feat_lane[mp]
        own = (lane < DIFF_DH) if mp == 0 else (lane >= DIFF_DH)
        ones = jnp.where((fl >= 0) & (fl < 3), 1.0, 0.0)
        qmaps.append(jnp.where(own, qf, ones).astype(BF16))
    chains = [(mp, r) for mp in range(2) for r in range(n_half)]
    n_chains = len(chains)

    m_scr[...] = jnp.full(m_scr.shape, MASK_NEG, F32)
    acc_scr[...] = jnp.zeros_like(acc_scr)

    row_t = lax.broadcasted_iota(jnp.int32, (rows, LANES), 0)
    col_t = lax.broadcasted_iota(jnp.int32, (rows, LANES), 1)

    def load_v_ext(j):
        vb = v_ref[pl.ds(pl.multiple_of(j * tk, tk), tk), :]
        return jnp.concatenate([vb, jnp.ones((tk, LANES), BF16)], axis=1)

    def scores_step(c, j, slot, masked):
        mp, r = chains[c]
        nt = tk // LANES
        qc = qmaps[mp][r * rows:(r + 1) * rows]
        kb = ke_scr[mp, pl.ds(pl.multiple_of(j * tk, tk), tk), :]
        s = _dot_nt(qc, kb)
        tiles = []
        for tl in range(nt):
            st = s[:, tl * LANES:(tl + 1) * LANES]
            if masked and (tl + 1) * LANES - 1 > r * rows:
                st = jnp.where(col_t + tl * LANES <= row_t + r * rows, st, MASK_NEG)
            tiles.append(st)
        mx = tiles[0]
        for st in tiles[1:]:
            mx = jnp.maximum(mx, st)
        m_prev = m_scr[c]
        m_new = jnp.maximum(m_prev, jnp.max(mx, axis=-1, keepdims=True))
        for tl in range(nt):
            p_scr[slot, c, :, tl * LANES:(tl + 1) * LANES] = (
                jnp.exp2(tiles[tl] - m_new).astype(BF16))
        alpha_scr[slot, c] = jnp.exp2(m_prev - m_new)
        m_scr[c] = m_new

    def pv_step(c, v_ext, slot):
        alpha = alpha_scr[slot, c]
        pv = _dot(p_scr[slot, c], v_ext)
        acc_scr[c] = acc_scr[c] * jnp.concatenate([alpha, alpha], axis=1) + pv

    def scores_block(j, slot, masked=False):
        for c in range(n_chains):
            scores_step(c, j, slot, masked)

    def pv_block(j, slot):
        v_ext = load_v_ext(j)
        for c in range(n_chains):
            pv_step(c, v_ext, slot)

    scores_block(i, 0, masked=True)
    m_min = jnp.min(jnp.min(m_scr[...], axis=0), axis=0, keepdims=True)[:, 0:1]
    need = m_min - (ATT_SKIP_EXP2 + ATT_SKIP_SLACK) - qk_bound
    j_first = jnp.ceil(need / (slope * tk) - (tk - 1) / tk)
    n_arr = jnp.clip(i.astype(F32) - jnp.maximum(j_first, 0.0), 0.0, i.astype(F32))
    n = jnp.max(n_arr).astype(jnp.int32)
    last = i - n

    def pair(u, carry):
        j = i - 2 * u - 1
        scores_block(j, 1)
        pv_block(j + 1, 0)
        scores_block(j - 1, 0)
        pv_block(j, 1)
        return carry

    lax.fori_loop(0, n // 2, pair, 0)

    @pl.when(n % 2 == 1)
    def _():
        scores_block(last, 1)
        pv_block(last + 1, 0)

    slot_last = n % 2
    v_last = load_v_ext(last)
    acc_fin = []
    for c in range(n_chains):
        alpha = alpha_scr[slot_last, c]
        pv = _dot(p_scr[slot_last, c], v_last)
        acc_fin.append(acc_scr[c] * jnp.concatenate([alpha, alpha], axis=1) + pv)

    def map_out(mp):
        parts = []
        for r in range(n_half):
            acc = acc_fin[mp * n_half + r]
            parts.append(acc[:, :DIFF_DV] / acc[:, DIFF_DV:])
        return jnp.concatenate(parts, axis=0)

    lam = (jnp.exp(jnp.sum(lq1_ref[...] * lk1_ref[...], axis=-1, keepdims=True))
           - jnp.exp(jnp.sum(lq2_ref[...] * lk2_ref[...], axis=-1, keepdims=True))
           + LAM_INIT)
    o = map_out(0) - lam * map_out(1)
    o = o * lax.rsqrt(jnp.mean(o * o, axis=-1, keepdims=True) + SUBLN_EPS) * nw_ref[...]
    o = o * (1.0 - LAM_INIT) * _silu(dz_ref[...].astype(F32))
    o_ref[...] = o.astype(o_ref.dtype)


def _diff_attention(dq, dk, dv, dz, norm_w, lq1, lk1, lq2, lk2, batch, seq):
    tq = ATT_TQ
    nq = seq // tq
    n_chains = 2 * (tq // ATT_ROWS)
    qspec = pl.BlockSpec((tq, DIFF_DV), lambda b, h, i: (b * nq + i, h))
    kvspec = pl.BlockSpec((seq, DIFF_DV), lambda b, h, i: (b, h))
    full = lambda a: pl.BlockSpec(a.shape, lambda b, h, i: (0, 0))
    return pl.pallas_call(
        _attn_kernel,
        grid=(batch, DIFF_HEADS, nq),
        in_specs=[qspec, kvspec, kvspec, qspec, full(norm_w), full(lq1), full(lk1),
                  full(lq2), full(lk2)],
        out_specs=qspec,
        out_shape=jax.ShapeDtypeStruct((batch * seq, DIFF_WIDTH), BF16),
        scratch_shapes=[pltpu.VMEM((2, seq, DIFF_DV), BF16),
                        pltpu.VMEM((n_chains, ATT_ROWS, LANES), F32),
                        pltpu.VMEM((n_chains, ATT_ROWS, 2 * LANES), F32),
                        pltpu.VMEM((2, n_chains, ATT_ROWS, ATT_TK), BF16),
                        pltpu.VMEM((2, n_chains, ATT_ROWS, LANES), F32),
                        pltpu.VMEM((8, LANES), F32)],
        compiler_params=pltpu.CompilerParams(
            dimension_semantics=("arbitrary", "arbitrary", "arbitrary"),
            vmem_limit_bytes=VMEM_LIMIT),
        name="diff_attention",
    )(dq, dk, dv, dz, norm_w, lq1, lk1, lq2, lk2)


def _out_kernel(x_ref, oa_ref, ob_ref, mab_ref, wa_ref, wb_ref, wo_ref, g_ref, b_ref, y_ref):
    ya = _dot(oa_ref[...], wa_ref[...])
    yb = _dot(ob_ref[...], wb_ref[...])
    merged = (_sigmoid(mab_ref[:, :D_MODEL].astype(F32)) * ya
              + _sigmoid(mab_ref[:, D_MODEL:].astype(F32)) * yb)
    y = _dot(merged.astype(BF16), wo_ref[...])
    z = DEEPNORM_ALPHA * x_ref[...] + y
    mu = jnp.mean(z, axis=-1, keepdims=True)
    zc = z - mu
    var = jnp.mean(zc * zc, axis=-1, keepdims=True)
    y_ref[...] = zc * lax.rsqrt(var + LN_EPS) * g_ref[...] + b_ref[...]


def _output(x2, o_a, o_b, mab, wa, wb, wo, ln_g, ln_b):
    m = x2.shape[0]
    tm = OUT_TM
    row = lambda width: pl.BlockSpec((tm, width), lambda i: (i, 0))
    full = lambda a: pl.BlockSpec(a.shape, lambda i: (0, 0))
    return pl.pallas_call(
        _out_kernel,
        grid=(m // tm,),
        in_specs=[row(D_MODEL), row(GDN_WIDTH), row(DIFF_WIDTH), row(2 * D_MODEL),
                  full(wa), full(wb), full(wo), full(ln_g), full(ln_b)],
        out_specs=row(D_MODEL),
        out_shape=jax.ShapeDtypeStruct((m, D_MODEL), F32),
        compiler_params=pltpu.CompilerParams(
            dimension_semantics=("arbitrary",), vmem_limit_bytes=VMEM_LIMIT),
        name="merge_out_ln",
    )(x2, o_a, o_b, mab, wa, wb, wo, ln_g, ln_b)


def _lane_row(v):
    return jnp.zeros((1, LANES), F32).at[0, :v.shape[0]].set(v.astype(F32))


def kernel(x, w_in, conv_w, a_log, dt_bias, gdn_norm_w, w_up_a, lambda_q1, lambda_k1,
           lambda_q2, lambda_k2, diff_norm_w, w_up_b, w_out, ln_g, ln_b):
    batch, seq, d = x.shape
    x2 = x.reshape(batch * seq, d)
    layer = 0
    w = w_in[layer].astype(BF16)
    w_tail = w[:, PROJ_HEAD + 2 * GDN_HEADS:]

    gq, gk, gv, hab, gz, dq, dk, dv, dz, mab = _project(
        x2, w, w_tail, conv_w[layer].astype(F32), seq)

    o_a = _gdn(gq, gk, gv, hab, gz, _lane_row(a_log[layer]), _lane_row(dt_bias[layer]),
               gdn_norm_w[layer].reshape(1, GDN_DV).astype(F32), batch, seq)

    o_b = _diff_attention(
        dq, dk, dv, dz, diff_norm_w[layer].reshape(1, DIFF_DV).astype(F32),
        lambda_q1[layer].reshape(1, DIFF_DH).astype(F32),
        lambda_k1[layer].reshape(1, DIFF_DH).astype(F32),
        lambda_q2[layer].reshape(1, DIFF_DH).astype(F32),
        lambda_k2[layer].reshape(1, DIFF_DH).astype(F32), batch, seq)

    y = _output(x2, o_a, o_b, mab, w_up_a[layer].astype(BF16), w_up_b[layer].astype(BF16),
                w_out[layer].astype(BF16), ln_g[layer].reshape(1, d).astype(F32),
                ln_b[layer].reshape(1, d).astype(F32))
    return y.reshape(batch, seq, d)
```

```python
import functools
import math

import jax
import jax.numpy as jnp
from jax import lax
from jax.experimental import pallas as pl
from jax.experimental.pallas import tpu as pltpu

F32 = jnp.float32
BF16 = jnp.bfloat16

D_MODEL = 1024
GDN_HEADS = 4
GDN_DK = 128
GDN_DV = 128
GDN_QK = GDN_HEADS * GDN_DK
GDN_WIDTH = GDN_HEADS * GDN_DV
CONV_K = 4
DIFF_HEADS = 4
DIFF_DH = 64
DIFF_DV = 2 * DIFF_DH
DIFF_QK = DIFF_HEADS * 2 * DIFF_DH
DIFF_WIDTH = DIFF_HEADS * DIFF_DV
NORM_EPS = 1e-6
SUBLN_EPS = 1e-5
LN_EPS = 1e-5
DEPTH = 1
DEEPNORM_ALPHA = (2.0 * DEPTH) ** 0.25
LAM_INIT = 0.8 - 0.6 * math.exp(-0.3 * 0)
ALIBI_SLOPES = tuple(2.0 ** (-8.0 * (i + 1) / DIFF_HEADS) for i in range(DIFF_HEADS))
LOG2E = math.log2(math.e)

LANES = 128
CONV_HIST = 8
VMEM_LIMIT = 56 * 1024 * 1024

PROJ_TM = 256
PROJ_HEAD = 2 * GDN_QK + GDN_WIDTH
GDN_T = 256
GDN_INV_BASE = 16
ATT_TQ = 512
ATT_TK = 512
ATT_ROWS = 256
ATT_ONES = 16
OUT_TM = 512
MASK_NEG = -1e30
ATT_SKIP_EXP2 = 150.0
ATT_SKIP_SLACK = 1.0
ATT_FIXED_REF_MAX = 48.0


def _fixed_ref_steps(head):
    slope = ALIBI_SLOPES[head] * LOG2E
    return max(math.ceil(ATT_SKIP_EXP2 / (slope * ATT_TK) + (ATT_TK - 1) / ATT_TK) - 1, 0)


def _sigmoid(x):
    return 1.0 / (1.0 + jnp.exp(-x))


def _silu(x):
    return x * _sigmoid(x)


def _dot(a, b):
    return jnp.dot(a, b, preferred_element_type=F32)


def _dot_nt(a, b):
    return lax.dot_general(a, b, (((1,), (1,)), ((), ())), preferred_element_type=F32)


def _proj_kernel(tiles_per_seq, x_ref, wh_ref, wt_ref, cw_ref, gq_ref, gk_ref, gv_ref,
                 hab_ref, gz_ref, dq_ref, dk_ref, dv_ref, dz_ref, mab_ref, hist_scr, xb_scr):
    tm = PROJ_TM

    @pl.when(pl.program_id(0) % tiles_per_seq == 0)
    def _():
        hist_scr[...] = jnp.zeros_like(hist_scr)

    xb_scr[...] = x_ref[...].astype(BF16)

    def mm(c0, width):
        if c0 < PROJ_HEAD:
            return _dot(xb_scr[...], wh_ref[:, c0:c0 + width])
        return _dot(xb_scr[...], wt_ref[:, c0 - PROJ_HEAD:c0 - PROJ_HEAD + width])

    piece = 2 * GDN_DK

    def conv_silu(c0):
        cols = slice(c0, c0 + piece)
        acc = mm(c0, piece)
        ext = jnp.concatenate([hist_scr[:, cols], acc], axis=0)
        hist_scr[:, cols] = acc[tm - CONV_HIST:]
        y = None
        for j in range(CONV_K):
            r0 = CONV_HIST - (CONV_K - 1) + j
            term = cw_ref[j:j + 1, cols] * ext[r0:r0 + tm]
            y = term if y is None else y + term
        return _silu(y)

    def l2norm_heads(y, scale):
        parts = []
        for h in range(piece // GDN_DK):
            v = y[:, h * GDN_DK:(h + 1) * GDN_DK]
            inv = lax.rsqrt(jnp.sum(v * v, axis=-1, keepdims=True) + NORM_EPS)
            parts.append(v * (inv * scale) if scale != 1.0 else v * inv)
        return jnp.concatenate(parts, axis=1)

    def plain(out_ref, c0, scale=None):
        for j in range(out_ref.shape[1] // 512):
            acc = mm(c0 + j * 512, 512)
            if scale is not None:
                acc = acc * scale
            out_ref[:, j * 512:(j + 1) * 512] = acc.astype(out_ref.dtype)

    for half in range(2):
        c0 = half * piece
        gq_ref[:, c0:c0 + piece] = l2norm_heads(conv_silu(c0), GDN_DK ** -0.5).astype(BF16)
        if half == 0:
            plain(gz_ref, 1536)
        else:
            plain(dq_ref, 2048, DIFF_DH ** -0.5 * LOG2E)
    for half in range(2):
        c0 = half * piece
        gk_ref[:, c0:c0 + piece] = l2norm_heads(conv_silu(GDN_QK + c0), 1.0).astype(BF16)
        plain(dk_ref if half == 0 else dv_ref, 2560 + half * 512)
    for half in range(2):
        c0 = half * piece
        gv_ref[:, c0:c0 + piece] = conv_silu(2 * GDN_QK + c0).astype(BF16)
        if half == 0:
            plain(dz_ref, 3584)
    hab_ref[...] = _dot(xb_scr[...], wh_ref[:, PROJ_HEAD:PROJ_HEAD + LANES])
    plain(mab_ref, 4096)


def _project(x2, w_bf, w_tail, conv_w, seq):
    m = x2.shape[0]
    tm = PROJ_TM
    row = lambda width: pl.BlockSpec((tm, width), lambda i: (i, 0))
    full = lambda a: pl.BlockSpec(a.shape, lambda i: (0, 0))
    out_shape = (
        jax.ShapeDtypeStruct((m, GDN_QK), BF16),
        jax.ShapeDtypeStruct((m, GDN_QK), BF16),
        jax.ShapeDtypeStruct((m, GDN_WIDTH), BF16),
        jax.ShapeDtypeStruct((m, LANES), F32),
        jax.ShapeDtypeStruct((m, 512), BF16),
        jax.ShapeDtypeStruct((m, 512), BF16),
        jax.ShapeDtypeStruct((m, 512), BF16),
        jax.ShapeDtypeStruct((m, 512), BF16),
        jax.ShapeDtypeStruct((m, 512), BF16),
        jax.ShapeDtypeStruct((m, 2048), BF16),
    )
    return pl.pallas_call(
        functools.partial(_proj_kernel, seq // tm),
        grid=(m // tm,),
        in_specs=[row(D_MODEL),
                  pl.BlockSpec((D_MODEL, PROJ_HEAD + LANES), lambda i: (0, 0)),
                  full(w_tail), full(conv_w)],
        out_specs=tuple(row(s.shape[1]) for s in out_shape),
        out_shape=out_shape,
        scratch_shapes=[pltpu.VMEM((CONV_HIST, 3 * GDN_QK), F32),
                        pltpu.VMEM((tm, D_MODEL), BF16)],
        compiler_params=pltpu.CompilerParams(
            dimension_semantics=("arbitrary",), vmem_limit_bytes=VMEM_LIMIT),
        name="in_proj",
    )(x2, w_bf, w_tail, conv_w)


def _gdn_kernel(gq_ref, gk_ref, gv_ref, hab_ref, gz_ref, alog_ref, dtb_ref, nw_ref, o_ref,
                st_scr):
    t = GDN_T
    heads = range(GDN_HEADS)

    @pl.when(pl.program_id(1) == 0)
    def _():
        st_scr[...] = jnp.zeros_like(st_scr)

    hab = hab_ref[...]
    xg = hab + dtb_ref[...]
    y_sp = jnp.exp(-jnp.abs(xg))
    u_sp = 1.0 + y_sp
    softplus = jnp.maximum(xg, 0.0) + (jnp.log(u_sp) - ((u_sp - 1.0) - y_sp) / u_sp)
    g_full = -jnp.exp(alog_ref[...]) * softplus
    beta_full = _sigmoid(hab)

    ri = lax.broadcasted_iota(jnp.int32, (t, t), 0)
    ci = lax.broadcasted_iota(jnp.int32, (t, t), 1)
    causal = ri >= ci
    strict = ri > ci
    eye = jnp.where(ri == ci, 1.0, 0.0).astype(F32)

    tri = jnp.where(causal, 1.0, 0.0).astype(BF16)
    g_hi = g_full.astype(BF16)
    g_r1 = g_full - g_hi.astype(F32)
    g_mid = g_r1.astype(BF16)
    g_lo = (g_r1 - g_mid.astype(F32)).astype(BF16)
    gc = _dot(tri, g_hi) + _dot(tri, g_mid) + _dot(tri, g_lo)
    gc_t = gc.T
    eg = jnp.exp(gc)
    g_last = gc[t - 1:t, :]
    k_dec = jnp.exp(g_last - gc)
    eg_last = jnp.exp(g_last)

    qn_b = [gq_ref[:, h * GDN_DK:(h + 1) * GDN_DK] for h in heads]
    kn_b = [gk_ref[:, h * GDN_DK:(h + 1) * GDN_DK] for h in heads]
    qn = [qn_b[h].astype(F32) for h in heads]
    kn = [kn_b[h].astype(F32) for h in heads]
    vv = [gv_ref[:, h * GDN_DV:(h + 1) * GDN_DV].astype(F32) for h in heads]

    bcol = [beta_full[:, GDN_HEADS + h:GDN_HEADS + h + 1] for h in heads]
    egc = [eg[:, h:h + 1] for h in heads]
    kb = [kn[h] * bcol[h] for h in heads]
    kq = [_dot_nt(jnp.concatenate([kb[h].astype(BF16), qn_b[h]], axis=0), kn_b[h])
          for h in heads]
    decay = []
    for h in heads:
        gd = gc[:, h:h + 1] - gc_t[h:h + 1, :]
        decay.append(jnp.where(causal, jnp.exp(jnp.where(causal, gd, 0.0)), 0.0))
    nmat = [jnp.where(strict, -(kq[h][:t] * decay[h]), 0.0) for h in heads]
    amat = [(kq[h][t:] * decay[h]).astype(BF16) for h in heads]
    rhs = [jnp.concatenate([vv[h] * bcol[h], kb[h] * egc[h]], axis=1).astype(BF16)
           for h in heads]
    qe = [qn[h] * egc[h] for h in heads]
    kd_t = [(kn[h] * k_dec[:, h:h + 1]).T.astype(BF16) for h in heads]

    def same_block(b):
        sh = int(math.log2(b))
        return (ri >> sh) == (ci >> sh)

    blk = GDN_INV_BASE
    in_blk = same_block(blk)
    n0 = [jnp.where(in_blk, nmat[h], 0.0) for h in heads]
    x = [eye + n0[h] for h in heads]
    pw = [n0[h].astype(BF16) for h in heads]
    for _ in range(int(math.log2(blk)) - 1):
        p32 = [_dot(pw[h], pw[h]) for h in heads]
        pw = [p32[h].astype(BF16) for h in heads]
        x = [x[h] + _dot(x[h].astype(BF16), pw[h]) for h in heads]
    while blk < t:
        in_big = same_block(2 * blk)
        n_off = [jnp.where(in_big, jnp.where(in_blk, 0.0, nmat[h]), 0.0).astype(BF16)
                 for h in heads]
        xb = [x[h].astype(BF16) for h in heads]
        xn = [_dot(xb[h], n_off[h]).astype(BF16) for h in heads]
        x = [x[h] + _dot(xn[h], xb[h]) for h in heads]
        in_blk = in_big
        blk *= 2
    sol = [_dot(x[h].astype(BF16), rhs[h]) for h in heads]

    s_prev = [st_scr[h] for h in heads]
    ws = [_dot(jnp.concatenate([sol[h][:, GDN_DV:], qe[h]], axis=0).astype(BF16),
               s_prev[h].astype(BF16)) for h in heads]
    v_new = [(sol[h][:, :GDN_DV] - ws[h][:t]).astype(BF16) for h in heads]
    o_l = [ws[h][t:] + _dot(amat[h], v_new[h]) for h in heads]
    for h in heads:
        st_scr[h] = s_prev[h] * eg_last[:, h:h + 1] + _dot(kd_t[h], v_new[h])
    nw = nw_ref[...]
    for h in heads:
        o = o_l[h]
        o = o * lax.rsqrt(jnp.mean(o * o, axis=-1, keepdims=True) + NORM_EPS) * nw
        gate = _silu(gz_ref[:, h * GDN_DV:(h + 1) * GDN_DV].astype(F32))
        o_ref[:, h * GDN_DV:(h + 1) * GDN_DV] = (o * gate).astype(o_ref.dtype)


def _gdn(gq, gk, gv, hab, gz, alog_row, dtb_row, norm_w, batch, seq):
    t = GDN_T
    nt = seq // t
    row = lambda width: pl.BlockSpec((t, width), lambda b, s: (b * nt + s, 0))
    full = lambda a: pl.BlockSpec(a.shape, lambda b, s: (0, 0))
    return pl.pallas_call(
        _gdn_kernel,
        grid=(batch, nt),
        in_specs=[row(GDN_QK), row(GDN_QK), row(GDN_WIDTH), row(LANES), row(GDN_WIDTH),
                  full(alog_row), full(dtb_row), full(norm_w)],
        out_specs=row(GDN_WIDTH),
        out_shape=jax.ShapeDtypeStruct((batch * seq, GDN_WIDTH), BF16),
        scratch_shapes=[pltpu.VMEM((GDN_HEADS, GDN_DK, GDN_DV), F32)],
        compiler_params=pltpu.CompilerParams(
            dimension_semantics=("arbitrary", "arbitrary"), vmem_limit_bytes=VMEM_LIMIT),
        name="gated_deltanet",
    )(gq, gk, gv, hab, gz, alog_row, dtb_row, norm_w)


def _attn_kernel(q_ref, k_ref, v_ref, dz_ref, nw_ref, lq1_ref, lk1_ref, lq2_ref, lk2_ref,
                 o_ref, ke_scr, vt_scr, m_scr, acc_scr, p_scr, alpha_scr, kmax_scr):
    tq, tk, cols = ATT_TQ, ATT_TK, ATT_ROWS
    n_half = tq // cols
    seq = k_ref.shape[0]
    h = pl.program_id(1)
    i = pl.program_id(2)
    slope = jnp.float32(ALIBI_SLOPES[DIFF_HEADS - 1] * LOG2E)
    n_cap = jnp.int32(_fixed_ref_steps(DIFF_HEADS - 1))
    for hh in range(DIFF_HEADS - 1):
        slope = jnp.where(h == hh, jnp.float32(ALIBI_SLOPES[hh] * LOG2E), slope)
        n_cap = jnp.where(h == hh, jnp.int32(_fixed_ref_steps(hh)), n_cap)

    lane = lax.broadcasted_iota(jnp.int32, (tk, LANES), 1)
    feat_lane = (lane - DIFF_DH, lane)

    def half_norm_max(vf):
        sq = vf * vf
        lo = jnp.sum(jnp.where(lane < DIFF_DH, sq, 0.0), axis=-1, keepdims=True)
        hi = jnp.sum(jnp.where(lane >= DIFF_DH, sq, 0.0), axis=-1, keepdims=True)
        return jnp.sqrt(jnp.max(jnp.maximum(lo, hi), axis=0, keepdims=True))

    @pl.when(i == 0)
    def _():
        kmax_scr[...] = jnp.zeros_like(kmax_scr)

        def build(blk, carry):
            r0 = pl.multiple_of(blk * tk, tk)
            kf = k_ref[pl.ds(r0, tk), :].astype(F32)
            kmax_scr[...] = jnp.maximum(kmax_scr[...], half_norm_max(kf))
            pos = (lax.broadcasted_iota(jnp.int32, (tk, LANES), 0) + r0).astype(F32) * slope
            p_hi = pos.astype(BF16).astype(F32)
            r1 = pos - p_hi
            p_mid = r1.astype(BF16).astype(F32)
            p_lo = r1 - p_mid
            for mp in range(2):
                fl = feat_lane[mp]
                feat = jnp.where(fl == 0, p_hi, jnp.where(fl == 1, p_mid, jnp.where(
                    fl == 2, p_lo, jnp.where((fl >= 3) & (fl < 6), 1.0, 0.0))))
                own = (lane < DIFF_DH) if mp == 0 else (lane >= DIFF_DH)
                ke_scr[mp, pl.ds(r0, tk), :] = jnp.where(own, kf, feat).astype(BF16)
            vt = v_ref[pl.ds(r0, tk), :].astype(F32).T
            vt_scr[blk, 0:DIFF_DV, :] = vt.astype(BF16)
            vt_scr[blk, DIFF_DV:, :] = jnp.ones((ATT_ONES, tk), BF16)
            return carry

        lax.fori_loop(0, seq // tk, build, 0)

    qf = q_ref[...].astype(F32)
    qk_bound = half_norm_max(qf) * kmax_scr[0:1, 0:1]
    chains = [(mp, r) for mp in range(2) for r in range(n_half)]
    n_chains = len(chains)

    def query_maps(ref_terms):
        out = []
        for mp in range(2):
            fl = feat_lane[mp]
            own = (lane < DIFF_DH) if mp == 0 else (lane >= DIFF_DH)
            feat = jnp.where((fl >= 0) & (fl < 3), 1.0, 0.0)
            if ref_terms is not None:
                r_hi, r_mid, r_lo = ref_terms
                feat = jnp.where(fl == 3, r_hi, jnp.where(fl == 4, r_mid,
                                                          jnp.where(fl == 5, r_lo, feat)))
            out.append(jnp.where(own, qf, feat).astype(BF16))
        return out

    key_l = lax.broadcasted_iota(jnp.int32, (tk, cols), 0)
    qry_l = lax.broadcasted_iota(jnp.int32, (tk, cols), 1)

    def scores_t(qmaps, c, j, masked):
        mp, r = chains[c]
        qc = qmaps[mp][r * cols:(r + 1) * cols]
        kb = ke_scr[mp, pl.ds(pl.multiple_of(j * tk, tk), tk), :]
        st = _dot_nt(kb, qc)
        if masked:
            st = jnp.where(key_l <= qry_l + r * cols, st, MASK_NEG)
        return st

    def finish(acc_fin):
        def map_out(mp):
            parts = []
            for r in range(n_half):
                acc = acc_fin[mp * n_half + r]
                parts.append(acc[:DIFF_DV] / acc[DIFF_DV:DIFF_DV + 1])
            return jnp.concatenate(parts, axis=1)

        lam = (jnp.exp(jnp.sum(lq1_ref[...] * lk1_ref[...], axis=-1, keepdims=True))
               - jnp.exp(jnp.sum(lq2_ref[...] * lk2_ref[...], axis=-1, keepdims=True))
               + LAM_INIT)
        o = (map_out(0) - lam * map_out(1)).T
        o = o * lax.rsqrt(jnp.mean(o * o, axis=-1, keepdims=True) + SUBLN_EPS) * nw_ref[...]
        o = o * (1.0 - LAM_INIT) * _silu(dz_ref[...].astype(F32))
        o_ref[...] = o.astype(o_ref.dtype)

    def pipeline(scores_step, pv_value, steps_after_diag):
        def scores_block(j, slot, masked=False):
            for c in range(n_chains):
                scores_step(c, j, slot, masked)

        def pv_block(j, slot):
            for c in range(n_chains):
                acc_scr[c] = pv_value(c, j, slot)

        acc_scr[...] = jnp.zeros_like(acc_scr)
        scores_block(i, 0, masked=True)
        n = steps_after_diag()
        last = i - n

        def pair(u, carry):
            j = i - 2 * u - 1
            scores_block(j, 1)
            pv_block(j + 1, 0)
            scores_block(j - 1, 0)
            pv_block(j, 1)
            return carry

        lax.fori_loop(0, n // 2, pair, 0)

        @pl.when(n % 2 == 1)
        def _():
            scores_block(last, 1)
            pv_block(last + 1, 0)

        finish([pv_value(c, last, n % 2) for c in range(n_chains)])

    use_fixed_ref = jnp.max(qk_bound) <= ATT_FIXED_REF_MAX

    @pl.when(use_fixed_ref)
    def _():
        pos_q = (lax.broadcasted_iota(jnp.int32, (tq, 1), 0) + i * tq).astype(F32) * slope
        neg_ref = -(qk_bound + ATT_SKIP_SLACK + pos_q)
        r_hi = neg_ref.astype(BF16).astype(F32)
        r1 = neg_ref - r_hi
        r_mid = r1.astype(BF16).astype(F32)
        qmaps = query_maps((r_hi, r_mid, r1 - r_mid))

        def scores_step(c, j, slot, masked):
            p_scr[slot, c] = jnp.exp2(scores_t(qmaps, c, j, masked)).astype(BF16)

        def pv_value(c, j, slot):
            return acc_scr[c] + _dot(vt_scr[j], p_scr[slot, c])

        pipeline(scores_step, pv_value, lambda: jnp.minimum(i, n_cap))

    @pl.when(jnp.logical_not(use_fixed_ref))
    def _():
        qmaps = query_maps(None)
        m_scr[...] = jnp.full(m_scr.shape, MASK_NEG, F32)

        def scores_step(c, j, slot, masked):
            st = scores_t(qmaps, c, j, masked)
            m_prev = m_scr[c]
            m_new = jnp.maximum(m_prev, jnp.max(st, axis=0, keepdims=True))
            p_scr[slot, c] = jnp.exp2(st - m_new).astype(BF16)
            alpha_scr[slot, c] = jnp.exp2(m_prev - m_new)
            m_scr[c] = m_new

        def pv_value(c, j, slot):
            pv = _dot(vt_scr[j], p_scr[slot, c])
            return acc_scr[c] * alpha_scr[slot, c] + pv

        def steps_after_diag():
            m_min = jnp.min(jnp.min(m_scr[...], axis=0), axis=-1, keepdims=True)
            need = m_min - (ATT_SKIP_EXP2 + ATT_SKIP_SLACK) - qk_bound
            j_first = jnp.ceil(need / (slope * tk) - (tk - 1) / tk)
            n_arr = jnp.clip(i.astype(F32) - jnp.maximum(j_first, 0.0), 0.0, i.astype(F32))
            return jnp.max(n_arr).astype(jnp.int32)

        pipeline(scores_step, pv_value, steps_after_diag)


def _diff_attention(dq, dk, dv, dz, norm_w, lq1, lk1, lq2, lk2, batch, seq):
    tq, tk = ATT_TQ, ATT_TK
    nq = seq // tq
    n_chains = 2 * (tq // ATT_ROWS)
    qspec = pl.BlockSpec((tq, DIFF_DV), lambda b, h, i: (b * nq + i, h))
    kvspec = pl.BlockSpec((seq, DIFF_DV), lambda b, h, i: (b, h))
    full = lambda a: pl.BlockSpec(a.shape, lambda b, h, i: (0, 0))
    return pl.pallas_call(
        _attn_kernel,
        grid=(batch, DIFF_HEADS, nq),
        in_specs=[qspec, kvspec, kvspec, qspec, full(norm_w), full(lq1), full(lk1),
                  full(lq2), full(lk2)],
        out_specs=qspec,
        out_shape=jax.ShapeDtypeStruct((batch * seq, DIFF_WIDTH), BF16),
        scratch_shapes=[pltpu.VMEM((2, seq, DIFF_DV), BF16),
                        pltpu.VMEM((seq // tk, DIFF_DV + ATT_ONES, tk), BF16),
                        pltpu.VMEM((n_chains, 1, ATT_ROWS), F32),
                        pltpu.VMEM((n_chains, DIFF_DV + ATT_ONES, ATT_ROWS), F32),
                        pltpu.VMEM((2, n_chains, tk, ATT_ROWS), BF16),
                        pltpu.VMEM((2, n_chains, 1, ATT_ROWS), F32),
                        pltpu.VMEM((8, LANES), F32)],
        compiler_params=pltpu.CompilerParams(
            dimension_semantics=("arbitrary", "arbitrary", "arbitrary"),
            vmem_limit_bytes=VMEM_LIMIT),
        name="diff_attention",
    )(dq, dk, dv, dz, norm_w, lq1, lk1, lq2, lk2)


def _out_kernel(x_ref, oa_ref, ob_ref, mab_ref, wa_ref, wb_ref, wo_ref, g_ref, b_ref, y_ref):
    ya = _dot(oa_ref[...], wa_ref[...])
    yb = _dot(ob_ref[...], wb_ref[...])
    merged = (_sigmoid(mab_ref[:, :D_MODEL].astype(F32)) * ya
              + _sigmoid(mab_ref[:, D_MODEL:].astype(F32)) * yb)
    y = _dot(merged.astype(BF16), wo_ref[...])
    z = DEEPNORM_ALPHA * x_ref[...] + y
    mu = jnp.mean(z, axis=-1, keepdims=True)
    zc = z - mu
    var = jnp.mean(zc * zc, axis=-1, keepdims=True)
    y_ref[...] = zc * lax.rsqrt(var + LN_EPS) * g_ref[...] + b_ref[...]


def _output(x2, o_a, o_b, mab, wa, wb, wo, ln_g, ln_b):
    m = x2.shape[0]
    tm = OUT_TM
    row = lambda width: pl.BlockSpec((tm, width), lambda i: (i, 0))
    full = lambda a: pl.BlockSpec(a.shape, lambda i: (0, 0))
    return pl.pallas_call(
        _out_kernel,
        grid=(m // tm,),
        in_specs=[row(D_MODEL), row(GDN_WIDTH), row(DIFF_WIDTH), row(2 * D_MODEL),
                  full(wa), full(wb), full(wo), full(ln_g), full(ln_b)],
        out_specs=row(D_MODEL),
        out_shape=jax.ShapeDtypeStruct((m, D_MODEL), F32),
        compiler_params=pltpu.CompilerParams(
            dimension_semantics=("arbitrary",), vmem_limit_bytes=VMEM_LIMIT),
        name="merge_out_ln",
    )(x2, o_a, o_b, mab, wa, wb, wo, ln_g, ln_b)


def _lane_row(v):
    return jnp.zeros((1, LANES), F32).at[0, :v.shape[0]].set(v.astype(F32))


def kernel(x, w_in, conv_w, a_log, dt_bias, gdn_norm_w, w_up_a, lambda_q1, lambda_k1,
           lambda_q2, lambda_k2, diff_norm_w, w_up_b, w_out, ln_g, ln_b):
    batch, seq, d = x.shape
    x2 = x.reshape(batch * seq, d)
    layer = 0
    w = w_in[layer].astype(BF16)
    w_tail = w[:, PROJ_HEAD + 2 * GDN_HEADS:]

    gq, gk, gv, hab, gz, dq, dk, dv, dz, mab = _project(
        x2, w, w_tail, conv_w[layer].astype(F32), seq)

    o_a = _gdn(gq, gk, gv, hab, gz, _lane_row(a_log[layer]), _lane_row(dt_bias[layer]),
               gdn_norm_w[layer].reshape(1, GDN_DV).astype(F32), batch, seq)

    o_b = _diff_attention(
        dq, dk, dv, dz, diff_norm_w[layer].reshape(1, DIFF_DV).astype(F32),
        lambda_q1[layer].reshape(1, DIFF_DH).astype(F32),
        lambda_k1[layer].reshape(1, DIFF_DH).astype(F32),
        lambda_q2[layer].reshape(1, DIFF_DH).astype(F32),
        lambda_k2[layer].reshape(1, DIFF_DH).astype(F32), batch, seq)

    y = _output(x2, o_a, o_b, mab, w_up_a[layer].astype(BF16), w_up_b[layer].astype(BF16),
                w_out[layer].astype(BF16), ln_g[layer].reshape(1, d).astype(F32),
                ln_b[layer].reshape(1, d).astype(F32))
    return y.reshape(batch, seq, d)
```

```python
import functools
import math

import jax
import jax.numpy as jnp
from jax import lax
from jax.experimental import pallas as pl
from jax.experimental.pallas import tpu as pltpu

F32 = jnp.float32
BF16 = jnp.bfloat16

D_MODEL = 1024
GDN_HEADS = 4
GDN_DK = 128
GDN_DV = 128
GDN_QK = GDN_HEADS * GDN_DK
GDN_WIDTH = GDN_HEADS * GDN_DV
CONV_K = 4
DIFF_HEADS = 4
DIFF_DH = 64
DIFF_DV = 2 * DIFF_DH
DIFF_QK = DIFF_HEADS * 2 * DIFF_DH
DIFF_WIDTH = DIFF_HEADS * DIFF_DV
NORM_EPS = 1e-6
SUBLN_EPS = 1e-5
LN_EPS = 1e-5
DEPTH = 1
DEEPNORM_ALPHA = (2.0 * DEPTH) ** 0.25
LAM_INIT = 0.8 - 0.6 * math.exp(-0.3 * 0)
ALIBI_SLOPES = tuple(2.0 ** (-8.0 * (i + 1) / DIFF_HEADS) for i in range(DIFF_HEADS))
LOG2E = math.log2(math.e)

LANES = 128
CONV_HIST = 8
VMEM_LIMIT = 56 * 1024 * 1024

PROJ_TM = 256
PROJ_HEAD = 2 * GDN_QK + GDN_WIDTH
GDN_T = 256
GDN_INV_BASE = 16
ATT_TQ = 512
ATT_TK = 512
ATT_ROWS = 256
ATT_ONES = 16
OUT_TM = 512
OUT_CHUNK = 256
OUT_ROWS = 256
MASK_NEG = -1e30
ATT_SKIP_EXP2 = 150.0
ATT_SKIP_SLACK = 1.0
ATT_FIXED_REF_MAX = 48.0


def _fixed_ref_steps(head):
    slope = ALIBI_SLOPES[head] * LOG2E
    return max(math.ceil(ATT_SKIP_EXP2 / (slope * ATT_TK) + (ATT_TK - 1) / ATT_TK) - 1, 0)


def _sigmoid(x):
    return 1.0 / (1.0 + jnp.exp(-x))


def _silu(x):
    return x * _sigmoid(x)


def _dot(a, b):
    return jnp.dot(a, b, preferred_element_type=F32)


def _dot_nt(a, b):
    return lax.dot_general(a, b, (((1,), (1,)), ((), ())), preferred_element_type=F32)


def _proj_kernel(tiles_per_seq, x_ref, wh_ref, wt_ref, cw_ref, gq_ref, gk_ref, gv_ref,
                 hab_ref, gz_ref, dq_ref, dk_ref, dv_ref, dz_ref, mab_ref, hist_scr, xb_scr):
    tm = PROJ_TM

    @pl.when(pl.program_id(0) % tiles_per_seq == 0)
    def _():
        hist_scr[...] = jnp.zeros_like(hist_scr)

    xb_scr[...] = x_ref[...].astype(BF16)

    def mm(c0, width):
        if c0 < PROJ_HEAD:
            return _dot(xb_scr[...], wh_ref[:, c0:c0 + width])
        return _dot(xb_scr[...], wt_ref[:, c0 - PROJ_HEAD:c0 - PROJ_HEAD + width])

    piece = 2 * GDN_DK

    def conv_silu(c0):
        cols = slice(c0, c0 + piece)
        acc = mm(c0, piece)
        ext = jnp.concatenate([hist_scr[:, cols], acc], axis=0)
        hist_scr[:, cols] = acc[tm - CONV_HIST:]
        y = None
        for j in range(CONV_K):
            r0 = CONV_HIST - (CONV_K - 1) + j
            term = cw_ref[j:j + 1, cols] * ext[r0:r0 + tm]
            y = term if y is None else y + term
        return _silu(y)

    def l2norm_heads(y, scale):
        parts = []
        for h in range(piece // GDN_DK):
            v = y[:, h * GDN_DK:(h + 1) * GDN_DK]
            inv = lax.rsqrt(jnp.sum(v * v, axis=-1, keepdims=True) + NORM_EPS)
            parts.append(v * (inv * scale) if scale != 1.0 else v * inv)
        return jnp.concatenate(parts, axis=1)

    def plain(out_ref, c0, scale=None):
        for j in range(out_ref.shape[1] // 512):
            acc = mm(c0 + j * 512, 512)
            if scale is not None:
                acc = acc * scale
            out_ref[:, j * 512:(j + 1) * 512] = acc.astype(out_ref.dtype)

    for half in range(2):
        c0 = half * piece
        gq_ref[:, c0:c0 + piece] = l2norm_heads(conv_silu(c0), GDN_DK ** -0.5).astype(BF16)
        if half == 0:
            plain(gz_ref, 1536)
        else:
            plain(dq_ref, 2048, DIFF_DH ** -0.5 * LOG2E)
    for half in range(2):
        c0 = half * piece
        gk_ref[:, c0:c0 + piece] = l2norm_heads(conv_silu(GDN_QK + c0), 1.0).astype(BF16)
        plain(dk_ref if half == 0 else dv_ref, 2560 + half * 512)
    for half in range(2):
        c0 = half * piece
        gv_ref[:, c0:c0 + piece] = conv_silu(2 * GDN_QK + c0).astype(BF16)
        if half == 0:
            plain(dz_ref, 3584)
    hab_ref[...] = _dot(xb_scr[...], wh_ref[:, PROJ_HEAD:PROJ_HEAD + LANES])
    plain(mab_ref, 4096)


def _project(x2, w_bf, w_tail, conv_w, seq):
    m = x2.shape[0]
    tm = PROJ_TM
    row = lambda width: pl.BlockSpec((tm, width), lambda i: (i, 0))
    full = lambda a: pl.BlockSpec(a.shape, lambda i: (0, 0))
    out_shape = (
        jax.ShapeDtypeStruct((m, GDN_QK), BF16),
        jax.ShapeDtypeStruct((m, GDN_QK), BF16),
        jax.ShapeDtypeStruct((m, GDN_WIDTH), BF16),
        jax.ShapeDtypeStruct((m, LANES), F32),
        jax.ShapeDtypeStruct((m, 512), BF16),
        jax.ShapeDtypeStruct((m, 512), BF16),
        jax.ShapeDtypeStruct((m, 512), BF16),
        jax.ShapeDtypeStruct((m, 512), BF16),
        jax.ShapeDtypeStruct((m, 512), BF16),
        jax.ShapeDtypeStruct((m, 2048), BF16),
    )
    return pl.pallas_call(
        functools.partial(_proj_kernel, seq // tm),
        grid=(m // tm,),
        in_specs=[row(D_MODEL),
                  pl.BlockSpec((D_MODEL, PROJ_HEAD + LANES), lambda i: (0, 0)),
                  full(w_tail), full(conv_w)],
        out_specs=tuple(row(s.shape[1]) for s in out_shape),
        out_shape=out_shape,
        scratch_shapes=[pltpu.VMEM((CONV_HIST, 3 * GDN_QK), F32),
                        pltpu.VMEM((tm, D_MODEL), BF16)],
        compiler_params=pltpu.CompilerParams(
            dimension_semantics=("arbitrary",), vmem_limit_bytes=VMEM_LIMIT),
        name="in_proj",
    )(x2, w_bf, w_tail, conv_w)


def _gdn_kernel(gq_ref, gk_ref, gv_ref, hab_ref, gz_ref, alog_ref, dtb_ref, nw_ref, o_ref,
                st_scr):
    t = GDN_T
    heads = range(GDN_HEADS)

    @pl.when(pl.program_id(1) == 0)
    def _():
        st_scr[...] = jnp.zeros_like(st_scr)

    hab = hab_ref[...]
    xg = hab + dtb_ref[...]
    y_sp = jnp.exp(-jnp.abs(xg))
    u_sp = 1.0 + y_sp
    softplus = jnp.maximum(xg, 0.0) + (jnp.log(u_sp) - ((u_sp - 1.0) - y_sp) / u_sp)
    g_full = -jnp.exp(alog_ref[...]) * softplus
    beta_full = _sigmoid(hab)

    ri = lax.broadcasted_iota(jnp.int32, (t, t), 0)
    ci = lax.broadcasted_iota(jnp.int32, (t, t), 1)
    causal = ri >= ci
    strict = ri > ci
    eye = jnp.where(ri == ci, 1.0, 0.0).astype(F32)

    tri = jnp.where(causal, 1.0, 0.0).astype(BF16)
    g_hi = g_full.astype(BF16)
    g_r1 = g_full - g_hi.astype(F32)
    g_mid = g_r1.astype(BF16)
    g_lo = (g_r1 - g_mid.astype(F32)).astype(BF16)
    gc = _dot(tri, g_hi) + _dot(tri, g_mid) + _dot(tri, g_lo)
    gc_t = gc.T
    eg = jnp.exp(gc)
    g_last = gc[t - 1:t, :]
    k_dec = jnp.exp(g_last - gc)
    eg_last = jnp.exp(g_last)

    qn_b = [gq_ref[:, h * GDN_DK:(h + 1) * GDN_DK] for h in heads]
    kn_b = [gk_ref[:, h * GDN_DK:(h + 1) * GDN_DK] for h in heads]
    qn = [qn_b[h].astype(F32) for h in heads]
    kn = [kn_b[h].astype(F32) for h in heads]
    vv = [gv_ref[:, h * GDN_DV:(h + 1) * GDN_DV].astype(F32) for h in heads]

    bcol = [beta_full[:, GDN_HEADS + h:GDN_HEADS + h + 1] for h in heads]
    egc = [eg[:, h:h + 1] for h in heads]
    kb = [kn[h] * bcol[h] for h in heads]
    kq = [_dot_nt(jnp.concatenate([kb[h].astype(BF16), qn_b[h]], axis=0), kn_b[h])
          for h in heads]
    decay = []
    for h in heads:
        gd = gc[:, h:h + 1] - gc_t[h:h + 1, :]
        decay.append(jnp.where(causal, jnp.exp(jnp.where(causal, gd, 0.0)), 0.0))
    nmat = [jnp.where(strict, -(kq[h][:t] * decay[h]), 0.0) for h in heads]
    amat = [(kq[h][t:] * decay[h]).astype(BF16) for h in heads]
    rhs = [jnp.concatenate([vv[h] * bcol[h], kb[h] * egc[h]], axis=1).astype(BF16)
           for h in heads]
    qe = [qn[h] * egc[h] for h in heads]
    kd_t = [(kn[h] * k_dec[:, h:h + 1]).T.astype(BF16) for h in heads]

    def same_block(b):
        sh = int(math.log2(b))
        return (ri >> sh) == (ci >> sh)

    blk = GDN_INV_BASE
    in_blk = same_block(blk)
    n0 = [jnp.where(in_blk, nmat[h], 0.0) for h in heads]
    x = [eye + n0[h] for h in heads]
    pw = [n0[h].astype(BF16) for h in heads]
    for _ in range(int(math.log2(blk)) - 1):
        p32 = [_dot(pw[h], pw[h]) for h in heads]
        pw = [p32[h].astype(BF16) for h in heads]
        x = [x[h] + _dot(x[h].astype(BF16), pw[h]) for h in heads]
    while blk < t:
        in_big = same_block(2 * blk)
        n_off = [jnp.where(in_big, jnp.where(in_blk, 0.0, nmat[h]), 0.0).astype(BF16)
                 for h in heads]
        xb = [x[h].astype(BF16) for h in heads]
        xn = [_dot(xb[h], n_off[h]).astype(BF16) for h in heads]
        x = [x[h] + _dot(xn[h], xb[h]) for h in heads]
        in_blk = in_big
        blk *= 2
    sol = [_dot(x[h].astype(BF16), rhs[h]) for h in heads]

    s_prev = [st_scr[h] for h in heads]
    ws = [_dot(jnp.concatenate([sol[h][:, GDN_DV:], qe[h]], axis=0).astype(BF16),
               s_prev[h].astype(BF16)) for h in heads]
    v_new = [(sol[h][:, :GDN_DV] - ws[h][:t]).astype(BF16) for h in heads]
    o_l = [ws[h][t:] + _dot(amat[h], v_new[h]) for h in heads]
    for h in heads:
        st_scr[h] = s_prev[h] * eg_last[:, h:h + 1] + _dot(kd_t[h], v_new[h])
    nw = nw_ref[...]
    for h in heads:
        o = o_l[h]
        o = o * lax.rsqrt(jnp.mean(o * o, axis=-1, keepdims=True) + NORM_EPS) * nw
        gate = _silu(gz_ref[:, h * GDN_DV:(h + 1) * GDN_DV].astype(F32))
        o_ref[:, h * GDN_DV:(h + 1) * GDN_DV] = (o * gate).astype(o_ref.dtype)


def _gdn(gq, gk, gv, hab, gz, alog_row, dtb_row, norm_w, batch, seq):
    t = GDN_T
    nt = seq // t
    row = lambda width: pl.BlockSpec((t, width), lambda b, s: (b * nt + s, 0))
    full = lambda a: pl.BlockSpec(a.shape, lambda b, s: (0, 0))
    return pl.pallas_call(
        _gdn_kernel,
        grid=(batch, nt),
        in_specs=[row(GDN_QK), row(GDN_QK), row(GDN_WIDTH), row(LANES), row(GDN_WIDTH),
                  full(alog_row), full(dtb_row), full(norm_w)],
        out_specs=row(GDN_WIDTH),
        out_shape=jax.ShapeDtypeStruct((batch * seq, GDN_WIDTH), BF16),
        scratch_shapes=[pltpu.VMEM((GDN_HEADS, GDN_DK, GDN_DV), F32)],
        compiler_params=pltpu.CompilerParams(
            dimension_semantics=("arbitrary", "arbitrary"), vmem_limit_bytes=VMEM_LIMIT),
        name="gated_deltanet",
    )(gq, gk, gv, hab, gz, alog_row, dtb_row, norm_w)


def _attn_kernel(q_ref, qall_ref, k_ref, v_ref, dz_ref, nw_ref, lq1_ref, lk1_ref, lq2_ref,
                 lk2_ref, o_ref, ke_scr, vt_scr, m_scr, acc_scr, p_scr, alpha_scr, bound_scr,
                 flag_smem):
    tq, tk, cols = ATT_TQ, ATT_TK, ATT_ROWS
    n_half = tq // cols
    seq = k_ref.shape[0]
    h = pl.program_id(1)
    i = pl.program_id(2)
    slope = jnp.float32(ALIBI_SLOPES[DIFF_HEADS - 1] * LOG2E)
    n_cap = jnp.int32(_fixed_ref_steps(DIFF_HEADS - 1))
    for hh in range(DIFF_HEADS - 1):
        slope = jnp.where(h == hh, jnp.float32(ALIBI_SLOPES[hh] * LOG2E), slope)
        n_cap = jnp.where(h == hh, jnp.int32(_fixed_ref_steps(hh)), n_cap)

    lane = lax.broadcasted_iota(jnp.int32, (tk, LANES), 1)
    feat_lane = (lane - DIFF_DH, lane)

    def half_norm_max(vf):
        hl = lax.broadcasted_iota(jnp.int32, (LANES, LANES), 0)
        hc = lax.broadcasted_iota(jnp.int32, (LANES, LANES), 1)
        pick = jnp.where(hc == jnp.where(hl < DIFF_DH, 0, 1), 1.0, 0.0).astype(BF16)
        sums = jnp.max(_dot((vf * vf).astype(BF16), pick), axis=0, keepdims=True)
        return jnp.sqrt(jnp.maximum(sums[:, 0:1], sums[:, 1:2])) * (1.0 + 2.0 ** -6)

    @pl.when(i == 0)
    def _():
        bound_scr[...] = jnp.zeros_like(bound_scr)

        def build(blk, carry):
            r0 = pl.multiple_of(blk * tk, tk)
            kf = k_ref[pl.ds(r0, tk), :].astype(F32)
            bound_scr[0:1, :] = jnp.maximum(bound_scr[0:1, :], half_norm_max(kf))
            bound_scr[1:2, :] = jnp.maximum(
                bound_scr[1:2, :], half_norm_max(qall_ref[pl.ds(r0, tk), :].astype(F32)))
            pos = (lax.broadcasted_iota(jnp.int32, (tk, LANES), 0) + r0).astype(F32) * slope
            p_hi = pos.astype(BF16).astype(F32)
            r1 = pos - p_hi
            p_mid = r1.astype(BF16).astype(F32)
            p_lo = r1 - p_mid
            for mp in range(2):
                fl = feat_lane[mp]
                feat = jnp.where(fl == 0, p_hi, jnp.where(fl == 1, p_mid, jnp.where(
                    fl == 2, p_lo, jnp.where((fl >= 3) & (fl < 6), 1.0, 0.0))))
                own = (lane < DIFF_DH) if mp == 0 else (lane >= DIFF_DH)
                ke_scr[mp, pl.ds(r0, tk), :] = jnp.where(own, kf, feat).astype(BF16)
            vt = v_ref[pl.ds(r0, tk), :].astype(F32).T
            vt_scr[blk, 0:DIFF_DV, :] = vt.astype(BF16)
            vt_scr[blk, DIFF_DV:, :] = jnp.ones((ATT_ONES, tk), BF16)
            return carry

        lax.fori_loop(0, seq // tk, build, 0, unroll=2)
        qk_all = bound_scr[0:1, :] * bound_scr[1:2, :]
        bound_scr[2:3, :] = qk_all
        flag_smem[0] = jnp.where(jnp.max(qk_all) <= ATT_FIXED_REF_MAX, 1, 0).astype(jnp.int32)

    qf = q_ref[...].astype(F32)
    qk_bound = bound_scr[2:3, 0:1]
    chains = [(mp, r) for mp in range(2) for r in range(n_half)]
    n_chains = len(chains)

    def query_maps(ref_terms):
        out = []
        for mp in range(2):
            fl = feat_lane[mp]
            own = (lane < DIFF_DH) if mp == 0 else (lane >= DIFF_DH)
            feat = jnp.where((fl >= 0) & (fl < 3), 1.0, 0.0)
            if ref_terms is not None:
                r_hi, r_mid, r_lo = ref_terms
                feat = jnp.where(fl == 3, r_hi, jnp.where(fl == 4, r_mid,
                                                          jnp.where(fl == 5, r_lo, feat)))
            out.append(jnp.where(own, qf, feat).astype(BF16))
        return out

    def scores_t(qmaps, c, j, masked):
        mp, r = chains[c]
        nk = (r + 1) * cols if masked else tk
        qc = qmaps[mp][r * cols:(r + 1) * cols]
        kb = ke_scr[mp, pl.ds(pl.multiple_of(j * tk, tk), nk), :]
        st = _dot_nt(kb, qc)
        if masked:
            key_l = lax.broadcasted_iota(jnp.int32, (nk, cols), 0)
            qry_l = lax.broadcasted_iota(jnp.int32, (nk, cols), 1)
            st = jnp.where(key_l <= qry_l + r * cols, st, MASK_NEG)
        return st

    def store_p(slot, c, p):
        nk = p.shape[0]
        p_scr[slot, c, 0:nk, :] = p
        if nk < tk:
            p_scr[slot, c, nk:, :] = jnp.zeros((tk - nk, cols), BF16)

    def finish(acc_fin):
        def map_out(mp):
            parts = []
            for r in range(n_half):
                acc = acc_fin[mp * n_half + r]
                parts.append(acc[:DIFF_DV] / acc[DIFF_DV:DIFF_DV + 1])
            return jnp.concatenate(parts, axis=1)

        lam = (jnp.exp(jnp.sum(lq1_ref[...] * lk1_ref[...], axis=-1, keepdims=True))
               - jnp.exp(jnp.sum(lq2_ref[...] * lk2_ref[...], axis=-1, keepdims=True))
               + LAM_INIT)
        o = (map_out(0) - lam * map_out(1)).T
        o = o * lax.rsqrt(jnp.mean(o * o, axis=-1, keepdims=True) + SUBLN_EPS) * nw_ref[...]
        o = o * (1.0 - LAM_INIT) * _silu(dz_ref[...].astype(F32))
        o_ref[...] = o.astype(o_ref.dtype)

    def pipeline(scores_step, pv_value, steps_after_diag):
        def scores_block(j, slot, masked=False):
            for c in range(n_chains):
                scores_step(c, j, slot, masked)

        def pv_block(j, slot):
            for c in range(n_chains):
                acc_scr[c] = pv_value(c, j, slot)

        acc_scr[...] = jnp.zeros_like(acc_scr)
        scores_block(i, 0, masked=True)
        n = steps_after_diag()
        last = i - n

        def pair(u, carry):
            j = i - 2 * u - 1
            scores_block(j, 1)
            pv_block(j + 1, 0)
            scores_block(j - 1, 0)
            pv_block(j, 1)
            return carry

        lax.fori_loop(0, n // 2, pair, 0)

        @pl.when(n % 2 == 1)
        def _():
            scores_block(last, 1)
            pv_block(last + 1, 0)

        finish([pv_value(c, last, n % 2) for c in range(n_chains)])

    use_fixed_ref = flag_smem[0] == 1

    @pl.when(use_fixed_ref)
    def _():
        pos_q = (lax.broadcasted_iota(jnp.int32, (tq, 1), 0) + i * tq).astype(F32) * slope
        neg_ref = -(qk_bound + ATT_SKIP_SLACK + pos_q)
        r_hi = neg_ref.astype(BF16).astype(F32)
        r1 = neg_ref - r_hi
        r_mid = r1.astype(BF16).astype(F32)
        qmaps = query_maps((r_hi, r_mid, r1 - r_mid))

        def scores_step(c, j, slot, masked):
            store_p(slot, c, jnp.exp2(scores_t(qmaps, c, j, masked)).astype(BF16))

        def pv_value(c, j, slot):
            return acc_scr[c] + _dot(vt_scr[j], p_scr[slot, c])

        pipeline(scores_step, pv_value, lambda: jnp.minimum(i, n_cap))

    @pl.when(jnp.logical_not(use_fixed_ref))
    def _():
        qmaps = query_maps(None)
        m_scr[...] = jnp.full(m_scr.shape, MASK_NEG, F32)

        def scores_step(c, j, slot, masked):
            st = scores_t(qmaps, c, j, masked)
            m_prev = m_scr[c]
            m_new = jnp.maximum(m_prev, jnp.max(st, axis=0, keepdims=True))
            store_p(slot, c, jnp.exp2(st - m_new).astype(BF16))
            alpha_scr[slot, c] = jnp.exp2(m_prev - m_new)
            m_scr[c] = m_new

        def pv_value(c, j, slot):
            pv = _dot(vt_scr[j], p_scr[slot, c])
            return acc_scr[c] * alpha_scr[slot, c] + pv

        def steps_after_diag():
            m_min = jnp.min(jnp.min(m_scr[...], axis=0), axis=-1, keepdims=True)
            need = m_min - (ATT_SKIP_EXP2 + ATT_SKIP_SLACK) - qk_bound
            j_first = jnp.ceil(need / (slope * tk) - (tk - 1) / tk)
            n_arr = jnp.clip(i.astype(F32) - jnp.maximum(j_first, 0.0), 0.0, i.astype(F32))
            return jnp.max(n_arr).astype(jnp.int32)

        pipeline(scores_step, pv_value, steps_after_diag)


def _diff_attention(dq, dk, dv, dz, norm_w, lq1, lk1, lq2, lk2, batch, seq):
    tq, tk = ATT_TQ, ATT_TK
    nq = seq // tq
    n_chains = 2 * (tq // ATT_ROWS)
    qspec = pl.BlockSpec((tq, DIFF_DV), lambda b, h, i: (b * nq + i, h))
    kvspec = pl.BlockSpec((seq, DIFF_DV), lambda b, h, i: (b, h))
    full = lambda a: pl.BlockSpec(a.shape, lambda b, h, i: (0, 0))
    return pl.pallas_call(
        _attn_kernel,
        grid=(batch, DIFF_HEADS, nq),
        in_specs=[qspec, kvspec, kvspec, kvspec, qspec, full(norm_w), full(lq1), full(lk1),
                  full(lq2), full(lk2)],
        out_specs=qspec,
        out_shape=jax.ShapeDtypeStruct((batch * seq, DIFF_WIDTH), BF16),
        scratch_shapes=[pltpu.VMEM((2, seq, DIFF_DV), BF16),
                        pltpu.VMEM((seq // tk, DIFF_DV + ATT_ONES, tk), BF16),
                        pltpu.VMEM((n_chains, 1, ATT_ROWS), F32),
                        pltpu.VMEM((n_chains, DIFF_DV + ATT_ONES, ATT_ROWS), F32),
                        pltpu.VMEM((2, n_chains, tk, ATT_ROWS), BF16),
                        pltpu.VMEM((2, n_chains, 1, ATT_ROWS), F32),
                        pltpu.VMEM((8, LANES), F32),
                        pltpu.SMEM((1,), jnp.int32)],
        compiler_params=pltpu.CompilerParams(
            dimension_semantics=("arbitrary", "arbitrary", "arbitrary"),
            vmem_limit_bytes=VMEM_LIMIT),
        name="diff_attention",
    )(dq, dq, dk, dv, dz, norm_w, lq1, lk1, lq2, lk2)


def _out_kernel(x_ref, oa_ref, ob_ref, mab_ref, wa_ref, wb_ref, wo_ref, g_ref, b_ref, y_ref,
                merged_scr):
    for c0 in range(0, D_MODEL, OUT_CHUNK):
        cs = slice(c0, c0 + OUT_CHUNK)
        ya = _dot(oa_ref[...], wa_ref[:, cs])
        yb = _dot(ob_ref[...], wb_ref[:, cs])
        ga = _sigmoid(mab_ref[:, c0:c0 + OUT_CHUNK].astype(F32))
        gb = _sigmoid(mab_ref[:, D_MODEL + c0:D_MODEL + c0 + OUT_CHUNK].astype(F32))
        merged_scr[:, cs] = (ga * ya + gb * yb).astype(BF16)
    tm = x_ref.shape[0]
    for r0 in range(0, tm, OUT_ROWS):
        rs = slice(r0, r0 + OUT_ROWS)
        y = _dot(merged_scr[rs, :], wo_ref[...])
        z = DEEPNORM_ALPHA * x_ref[rs, :] + y
        mu = jnp.mean(z, axis=-1, keepdims=True)
        zc = z - mu
        var = jnp.mean(zc * zc, axis=-1, keepdims=True)
        y_ref[rs, :] = zc * lax.rsqrt(var + LN_EPS) * g_ref[...] + b_ref[...]


def _output(x2, o_a, o_b, mab, wa, wb, wo, ln_g, ln_b):
    m = x2.shape[0]
    tm = OUT_TM
    row = lambda width: pl.BlockSpec((tm, width), lambda i: (i, 0))
    full = lambda a: pl.BlockSpec(a.shape, lambda i: (0, 0))
    return pl.pallas_call(
        _out_kernel,
        grid=(m // tm,),
        in_specs=[row(D_MODEL), row(GDN_WIDTH), row(DIFF_WIDTH), row(2 * D_MODEL),
                  full(wa), full(wb), full(wo), full(ln_g), full(ln_b)],
        out_specs=row(D_MODEL),
        out_shape=jax.ShapeDtypeStruct((m, D_MODEL), F32),
        scratch_shapes=[pltpu.VMEM((tm, D_MODEL), BF16)],
        compiler_params=pltpu.CompilerParams(
            dimension_semantics=("arbitrary",), vmem_limit_bytes=VMEM_LIMIT),
        name="merge_out_ln",
    )(x2, o_a, o_b, mab, wa, wb, wo, ln_g, ln_b)


def _lane_row(v):
    return jnp.zeros((1, LANES), F32).at[0, :v.shape[0]].set(v.astype(F32))


def kernel(x, w_in, conv_w, a_log, dt_bias, gdn_norm_w, w_up_a, lambda_q1, lambda_k1,
           lambda_q2, lambda_k2, diff_norm_w, w_up_b, w_out, ln_g, ln_b):
    batch, seq, d = x.shape
    x2 = x.reshape(batch * seq, d)
    layer = 0
    w = w_in[layer].astype(BF16)
    w_tail = w[:, PROJ_HEAD + 2 * GDN_HEADS:]

    gq, gk, gv, hab, gz, dq, dk, dv, dz, mab = _project(
        x2, w, w_tail, conv_w[layer].astype(F32), seq)

    o_a = _gdn(gq, gk, gv, hab, gz, _lane_row(a_log[layer]), _lane_row(dt_bias[layer]),
               gdn_norm_w[layer].reshape(1, GDN_DV).astype(F32), batch, seq)

    o_b = _diff_attention(
        dq, dk, dv, dz, diff_norm_w[layer].reshape(1, DIFF_DV).astype(F32),
        lambda_q1[layer].reshape(1, DIFF_DH).astype(F32),
        lambda_k1[layer].reshape(1, DIFF_DH).astype(F32),
        lambda_q2[layer].reshape(1, DIFF_DH).astype(F32),
        lambda_k2[layer].reshape(1, DIFF_DH).astype(F32), batch, seq)

    y = _output(x2, o_a, o_b, mab, w_up_a[layer].astype(BF16), w_up_b[layer].astype(BF16),
                w_out[layer].astype(BF16), ln_g[layer].reshape(1, d).astype(F32),
                ln_b[layer].reshape(1, d).astype(F32))
    return y.reshape(batch, seq, d)
```

```python
import functools
import math

import jax
import jax.numpy as jnp
from jax import lax
from jax.experimental import pallas as pl
from jax.experimental.pallas import tpu as pltpu

F32 = jnp.float32
BF16 = jnp.bfloat16

D_MODEL = 1024
GDN_HEADS = 4
GDN_DK = 128
GDN_DV = 128
GDN_QK = GDN_HEADS * GDN_DK
GDN_WIDTH = GDN_HEADS * GDN_DV
CONV_K = 4
DIFF_HEADS = 4
DIFF_DH = 64
DIFF_DV = 2 * DIFF_DH
DIFF_QK = DIFF_HEADS * 2 * DIFF_DH
DIFF_WIDTH = DIFF_HEADS * DIFF_DV
NORM_EPS = 1e-6
SUBLN_EPS = 1e-5
LN_EPS = 1e-5
DEPTH = 1
DEEPNORM_ALPHA = (2.0 * DEPTH) ** 0.25
LAM_INIT = 0.8 - 0.6 * math.exp(-0.3 * 0)
ALIBI_SLOPES = tuple(2.0 ** (-8.0 * (i + 1) / DIFF_HEADS) for i in range(DIFF_HEADS))
LOG2E = math.log2(math.e)

LANES = 128
CONV_HIST = 8
VMEM_LIMIT = 56 * 1024 * 1024

PROJ_TM = 256
PROJ_HEAD = 2 * GDN_QK + GDN_WIDTH
GDN_T = 256
GDN_STEP_CHUNKS = 2
GDN_INV_BASE = 16
ATT_TQ = 512
ATT_TK = 512
ATT_ROWS = 256
ATT_ONES = 16
OUT_TM = 512
OUT_CHUNK = 256
OUT_ROW_GROUPS = (256, 256)
assert sum(OUT_ROW_GROUPS) == OUT_TM
MASK_NEG = -1e30
ATT_SKIP_EXP2 = 150.0
ATT_SKIP_SLACK = 1.0
ATT_FIXED_REF_MAX = 48.0


def _fixed_ref_steps(head):
    slope = ALIBI_SLOPES[head] * LOG2E
    return max(math.ceil(ATT_SKIP_EXP2 / (slope * ATT_TK) + (ATT_TK - 1) / ATT_TK) - 1, 0)


def _sigmoid(x):
    return 1.0 / (1.0 + jnp.exp(-x))


def _silu(x):
    return x * _sigmoid(x)


def _dot(a, b):
    return jnp.dot(a, b, preferred_element_type=F32)


def _dot_nt(a, b):
    return lax.dot_general(a, b, (((1,), (1,)), ((), ())), preferred_element_type=F32)


def _proj_kernel(tiles_per_seq, x_ref, wh_ref, wt_ref, cw_ref, gq_ref, gk_ref, gv_ref,
                 hab_ref, gz_ref, dq_ref, dk_ref, dv_ref, dz_ref, mab_ref, hist_scr, xb_scr):
    tm = PROJ_TM

    @pl.when(pl.program_id(0) % tiles_per_seq == 0)
    def _():
        hist_scr[...] = jnp.zeros_like(hist_scr)

    xb_scr[...] = x_ref[...].astype(BF16)

    def mm(c0, width):
        if c0 < PROJ_HEAD:
            return _dot(xb_scr[...], wh_ref[:, c0:c0 + width])
        return _dot(xb_scr[...], wt_ref[:, c0 - PROJ_HEAD:c0 - PROJ_HEAD + width])

    piece = 2 * GDN_DK

    def conv_silu(c0):
        cols = slice(c0, c0 + piece)
        acc = mm(c0, piece)
        ext = jnp.concatenate([hist_scr[:, cols], acc], axis=0)
        hist_scr[:, cols] = acc[tm - CONV_HIST:]
        y = None
        for j in range(CONV_K):
            r0 = CONV_HIST - (CONV_K - 1) + j
            term = cw_ref[j:j + 1, cols] * ext[r0:r0 + tm]
            y = term if y is None else y + term
        return _silu(y)

    def l2norm_heads(y, scale):
        parts = []
        for h in range(piece // GDN_DK):
            v = y[:, h * GDN_DK:(h + 1) * GDN_DK]
            inv = lax.rsqrt(jnp.sum(v * v, axis=-1, keepdims=True) + NORM_EPS)
            parts.append(v * (inv * scale) if scale != 1.0 else v * inv)
        return jnp.concatenate(parts, axis=1)

    def plain(out_ref, c0, scale=None):
        for j in range(out_ref.shape[1] // 512):
            acc = mm(c0 + j * 512, 512)
            if scale is not None:
                acc = acc * scale
            out_ref[:, j * 512:(j + 1) * 512] = acc.astype(out_ref.dtype)

    for half in range(2):
        c0 = half * piece
        gq_ref[:, c0:c0 + piece] = l2norm_heads(conv_silu(c0), GDN_DK ** -0.5).astype(BF16)
        if half == 0:
            plain(gz_ref, 1536)
        else:
            plain(dq_ref, 2048, DIFF_DH ** -0.5 * LOG2E)
    for half in range(2):
        c0 = half * piece
        gk_ref[:, c0:c0 + piece] = l2norm_heads(conv_silu(GDN_QK + c0), 1.0).astype(BF16)
        plain(dk_ref if half == 0 else dv_ref, 2560 + half * 512)
    for half in range(2):
        c0 = half * piece
        gv_ref[:, c0:c0 + piece] = conv_silu(2 * GDN_QK + c0).astype(BF16)
        if half == 0:
            plain(dz_ref, 3584)
    hab_ref[...] = _dot(xb_scr[...], wh_ref[:, PROJ_HEAD:PROJ_HEAD + LANES])
    plain(mab_ref, 4096)


def _project(x2, w_bf, w_tail, conv_w, seq):
    m = x2.shape[0]
    tm = PROJ_TM
    row = lambda width: pl.BlockSpec((tm, width), lambda i: (i, 0))
    full = lambda a: pl.BlockSpec(a.shape, lambda i: (0, 0))
    out_shape = (
        jax.ShapeDtypeStruct((m, GDN_QK), BF16),
        jax.ShapeDtypeStruct((m, GDN_QK), BF16),
        jax.ShapeDtypeStruct((m, GDN_WIDTH), BF16),
        jax.ShapeDtypeStruct((m, LANES), F32),
        jax.ShapeDtypeStruct((m, 512), BF16),
        jax.ShapeDtypeStruct((m, 512), BF16),
        jax.ShapeDtypeStruct((m, 512), BF16),
        jax.ShapeDtypeStruct((m, 512), BF16),
        jax.ShapeDtypeStruct((m, 512), BF16),
        jax.ShapeDtypeStruct((m, 2048), BF16),
    )
    return pl.pallas_call(
        functools.partial(_proj_kernel, seq // tm),
        grid=(m // tm,),
        in_specs=[row(D_MODEL),
                  pl.BlockSpec((D_MODEL, PROJ_HEAD + LANES), lambda i: (0, 0)),
                  full(w_tail), full(conv_w)],
        out_specs=tuple(row(s.shape[1]) for s in out_shape),
        out_shape=out_shape,
        scratch_shapes=[pltpu.VMEM((CONV_HIST, 3 * GDN_QK), F32),
                        pltpu.VMEM((tm, D_MODEL), BF16)],
        compiler_params=pltpu.CompilerParams(
            dimension_semantics=("arbitrary",), vmem_limit_bytes=VMEM_LIMIT),
        name="in_proj",
    )(x2, w_bf, w_tail, conv_w)


def _gdn_kernel(gq_ref, gk_ref, gv_ref, hab_ref, gz_ref, alog_ref, dtb_ref, nw_ref, o_ref,
                st_scr):
    @pl.when(pl.program_id(1) == 0)
    def _():
        st_scr[...] = jnp.zeros_like(st_scr)

    for ck in range(GDN_STEP_CHUNKS):
        r = pl.ds(ck * GDN_T, GDN_T)
        prep = _gdn_prep(gq_ref.at[r], gk_ref.at[r], gv_ref.at[r], hab_ref.at[r], alog_ref,
                         dtb_ref)
        _gdn_apply(prep, gz_ref.at[r], nw_ref, o_ref.at[r], st_scr)


def _gdn_prep(gq_ref, gk_ref, gv_ref, hab_ref, alog_ref, dtb_ref):
    t = GDN_T
    heads = range(GDN_HEADS)

    hab = hab_ref[...]
    xg = hab + dtb_ref[...]
    y_sp = jnp.exp(-jnp.abs(xg))
    u_sp = 1.0 + y_sp
    softplus = jnp.maximum(xg, 0.0) + (jnp.log(u_sp) - ((u_sp - 1.0) - y_sp) / u_sp)
    g_full = -jnp.exp(alog_ref[...]) * softplus
    beta_full = _sigmoid(hab)

    ri = lax.broadcasted_iota(jnp.int32, (t, t), 0)
    ci = lax.broadcasted_iota(jnp.int32, (t, t), 1)
    causal = ri >= ci
    strict = ri > ci
    eye = jnp.where(ri == ci, 1.0, 0.0).astype(F32)

    tri = jnp.where(causal, 1.0, 0.0).astype(BF16)
    g_hi = g_full.astype(BF16)
    g_r1 = g_full - g_hi.astype(F32)
    g_mid = g_r1.astype(BF16)
    g_lo = (g_r1 - g_mid.astype(F32)).astype(BF16)
    gc = _dot(tri, g_hi) + _dot(tri, g_mid) + _dot(tri, g_lo)
    gc_t = gc.T
    eg = jnp.exp(gc)
    g_last = gc[t - 1:t, :]
    k_dec = jnp.exp(g_last - gc)
    eg_last = jnp.exp(g_last)

    qn_b = [gq_ref[:, h * GDN_DK:(h + 1) * GDN_DK] for h in heads]
    kn_b = [gk_ref[:, h * GDN_DK:(h + 1) * GDN_DK] for h in heads]
    qn = [qn_b[h].astype(F32) for h in heads]
    kn = [kn_b[h].astype(F32) for h in heads]
    vv = [gv_ref[:, h * GDN_DV:(h + 1) * GDN_DV].astype(F32) for h in heads]

    bcol = [beta_full[:, GDN_HEADS + h:GDN_HEADS + h + 1] for h in heads]
    egc = [eg[:, h:h + 1] for h in heads]
    kb = [kn[h] * bcol[h] for h in heads]
    kq = [_dot_nt(jnp.concatenate([kb[h].astype(BF16), qn_b[h]], axis=0), kn_b[h])
          for h in heads]
    decay = []
    for h in heads:
        gd = gc[:, h:h + 1] - gc_t[h:h + 1, :]
        decay.append(jnp.where(causal, jnp.exp(jnp.where(causal, gd, 0.0)), 0.0))
    nmat = [jnp.where(strict, -(kq[h][:t] * decay[h]), 0.0) for h in heads]
    amat = [(kq[h][t:] * decay[h]).astype(BF16) for h in heads]
    rhs = [jnp.concatenate([vv[h] * bcol[h], kb[h] * egc[h]], axis=1).astype(BF16)
           for h in heads]
    qe = [qn[h] * egc[h] for h in heads]
    kd_t = [(kn[h] * k_dec[:, h:h + 1]).T.astype(BF16) for h in heads]

    def same_block(b):
        sh = int(math.log2(b))
        return (ri >> sh) == (ci >> sh)

    blk = GDN_INV_BASE
    in_blk = same_block(blk)
    n0 = [jnp.where(in_blk, nmat[h], 0.0) for h in heads]
    x = [eye + n0[h] for h in heads]
    pw = [n0[h].astype(BF16) for h in heads]
    for _ in range(int(math.log2(blk)) - 1):
        p32 = [_dot(pw[h], pw[h]) for h in heads]
        pw = [p32[h].astype(BF16) for h in heads]
        x = [x[h] + _dot(x[h].astype(BF16), pw[h]) for h in heads]
    while blk < t:
        in_big = same_block(2 * blk)
        n_off = [jnp.where(in_big, jnp.where(in_blk, 0.0, nmat[h]), 0.0).astype(BF16)
                 for h in heads]
        xb = [x[h].astype(BF16) for h in heads]
        xn = [_dot(xb[h], n_off[h]).astype(BF16) for h in heads]
        x = [x[h] + _dot(xn[h], xb[h]) for h in heads]
        in_blk = in_big
        blk *= 2
    sol = [_dot(x[h].astype(BF16), rhs[h]) for h in heads]
    return sol, qe, amat, kd_t, eg_last


def _gdn_apply(prep, gz_ref, nw_ref, o_ref, st_scr):
    t = GDN_T
    heads = range(GDN_HEADS)
    sol, qe, amat, kd_t, eg_last = prep
    s_prev = [st_scr[h] for h in heads]
    ws = [_dot(jnp.concatenate([sol[h][:, GDN_DV:], qe[h]], axis=0).astype(BF16),
               s_prev[h].astype(BF16)) for h in heads]
    v_new = [(sol[h][:, :GDN_DV] - ws[h][:t]).astype(BF16) for h in heads]
    o_l = [ws[h][t:] + _dot(amat[h], v_new[h]) for h in heads]
    for h in heads:
        st_scr[h] = s_prev[h] * eg_last[:, h:h + 1] + _dot(kd_t[h], v_new[h])
    nw = nw_ref[...]
    for h in heads:
        o = o_l[h]
        o = o * lax.rsqrt(jnp.mean(o * o, axis=-1, keepdims=True) + NORM_EPS) * nw
        gate = _silu(gz_ref[:, h * GDN_DV:(h + 1) * GDN_DV].astype(F32))
        o_ref[:, h * GDN_DV:(h + 1) * GDN_DV] = (o * gate).astype(o_ref.dtype)


def _gdn(gq, gk, gv, hab, gz, alog_row, dtb_row, norm_w, batch, seq):
    t = GDN_T * GDN_STEP_CHUNKS
    nt = seq // t
    row = lambda width: pl.BlockSpec((t, width), lambda b, s: (b * nt + s, 0))
    full = lambda a: pl.BlockSpec(a.shape, lambda b, s: (0, 0))
    return pl.pallas_call(
        _gdn_kernel,
        grid=(batch, nt),
        in_specs=[row(GDN_QK), row(GDN_QK), row(GDN_WIDTH), row(LANES), row(GDN_WIDTH),
                  full(alog_row), full(dtb_row), full(norm_w)],
        out_specs=row(GDN_WIDTH),
        out_shape=jax.ShapeDtypeStruct((batch * seq, GDN_WIDTH), BF16),
        scratch_shapes=[pltpu.VMEM((GDN_HEADS, GDN_DK, GDN_DV), F32)],
        compiler_params=pltpu.CompilerParams(
            dimension_semantics=("arbitrary", "arbitrary"), vmem_limit_bytes=VMEM_LIMIT),
        name="gated_deltanet",
    )(gq, gk, gv, hab, gz, alog_row, dtb_row, norm_w)


def _attn_kernel(q_ref, qall_ref, k_ref, v_ref, dz_ref, nw_ref, lq1_ref, lk1_ref, lq2_ref,
                 lk2_ref, o_ref, ke_scr, vt_scr, m_scr, acc_scr, p_scr, alpha_scr, bound_scr,
                 flag_smem):
    tq, tk, cols = ATT_TQ, ATT_TK, ATT_ROWS
    n_half = tq // cols
    seq = k_ref.shape[0]
    h = pl.program_id(1)
    i = pl.program_id(2)
    slope = jnp.float32(ALIBI_SLOPES[DIFF_HEADS - 1] * LOG2E)
    n_cap = jnp.int32(_fixed_ref_steps(DIFF_HEADS - 1))
    for hh in range(DIFF_HEADS - 1):
        slope = jnp.where(h == hh, jnp.float32(ALIBI_SLOPES[hh] * LOG2E), slope)
        n_cap = jnp.where(h == hh, jnp.int32(_fixed_ref_steps(hh)), n_cap)

    lane = lax.broadcasted_iota(jnp.int32, (tk, LANES), 1)
    feat_lane = (lane - DIFF_DH, lane)

    def half_norm_max(vf):
        hl = lax.broadcasted_iota(jnp.int32, (LANES, LANES), 0)
        hc = lax.broadcasted_iota(jnp.int32, (LANES, LANES), 1)
        pick = jnp.where(hc == jnp.where(hl < DIFF_DH, 0, 1), 1.0, 0.0).astype(BF16)
        sums = jnp.max(_dot((vf * vf).astype(BF16), pick), axis=0, keepdims=True)
        return jnp.sqrt(jnp.maximum(sums[:, 0:1], sums[:, 1:2])) * (1.0 + 2.0 ** -6)

    @pl.when(i == 0)
    def _():
        bound_scr[...] = jnp.zeros_like(bound_scr)

        def build(blk, carry):
            r0 = pl.multiple_of(blk * tk, tk)
            kf = k_ref[pl.ds(r0, tk), :].astype(F32)
            bound_scr[0:1, :] = jnp.maximum(bound_scr[0:1, :], half_norm_max(kf))
            bound_scr[1:2, :] = jnp.maximum(
                bound_scr[1:2, :], half_norm_max(qall_ref[pl.ds(r0, tk), :].astype(F32)))
            pos = (lax.broadcasted_iota(jnp.int32, (tk, LANES), 0) + r0).astype(F32) * slope
            p_hi = pos.astype(BF16).astype(F32)
            r1 = pos - p_hi
            p_mid = r1.astype(BF16).astype(F32)
            p_lo = r1 - p_mid
            for mp in range(2):
                fl = feat_lane[mp]
                feat = jnp.where(fl == 0, p_hi, jnp.where(fl == 1, p_mid, jnp.where(
                    fl == 2, p_lo, jnp.where((fl >= 3) & (fl < 6), 1.0, 0.0))))
                own = (lane < DIFF_DH) if mp == 0 else (lane >= DIFF_DH)
                ke_scr[mp, pl.ds(r0, tk), :] = jnp.where(own, kf, feat).astype(BF16)
            vt = v_ref[pl.ds(r0, tk), :].astype(F32).T
            vt_scr[blk, 0:DIFF_DV, :] = vt.astype(BF16)
            vt_scr[blk, DIFF_DV:, :] = jnp.ones((ATT_ONES, tk), BF16)
            return carry

        lax.fori_loop(0, seq // tk, build, 0, unroll=2)
        qk_all = bound_scr[0:1, :] * bound_scr[1:2, :]
        bound_scr[2:3, :] = qk_all
        flag_smem[0] = jnp.where(jnp.max(qk_all) <= ATT_FIXED_REF_MAX, 1, 0).astype(jnp.int32)

    qf = q_ref[...].astype(F32)
    qk_bound = bound_scr[2:3, 0:1]
    chains = [(mp, r) for mp in range(2) for r in range(n_half)]
    n_chains = len(chains)

    def query_maps(ref_terms):
        out = []
        for mp in range(2):
            fl = feat_lane[mp]
            own = (lane < DIFF_DH) if mp == 0 else (lane >= DIFF_DH)
            feat = jnp.where((fl >= 0) & (fl < 3), 1.0, 0.0)
            if ref_terms is not None:
                r_hi, r_mid, r_lo = ref_terms
                feat = jnp.where(fl == 3, r_hi, jnp.where(fl == 4, r_mid,
                                                          jnp.where(fl == 5, r_lo, feat)))
            out.append(jnp.where(own, qf, feat).astype(BF16))
        return out

    def scores_t(qmaps, c, j, masked):
        mp, r = chains[c]
        nk = (r + 1) * cols if masked else tk
        qc = qmaps[mp][r * cols:(r + 1) * cols]
        kb = ke_scr[mp, pl.ds(pl.multiple_of(j * tk, tk), nk), :]
        st = _dot_nt(kb, qc)
        if masked:
            key_l = lax.broadcasted_iota(jnp.int32, (nk, cols), 0)
            qry_l = lax.broadcasted_iota(jnp.int32, (nk, cols), 1)
            st = jnp.where(key_l <= qry_l + r * cols, st, MASK_NEG)
        return st

    def store_p(slot, c, p):
        nk = p.shape[0]
        p_scr[slot, c, 0:nk, :] = p
        if nk < tk:
            p_scr[slot, c, nk:, :] = jnp.zeros((tk - nk, cols), BF16)

    def finish(acc_fin):
        def map_out(mp):
            parts = []
            for r in range(n_half):
                acc = acc_fin[mp * n_half + r]
                parts.append(acc[:DIFF_DV] * (1.0 / acc[DIFF_DV:DIFF_DV + 1]))
            return jnp.concatenate(parts, axis=1)

        lam = (jnp.exp(jnp.sum(lq1_ref[...] * lk1_ref[...], axis=-1, keepdims=True))
               - jnp.exp(jnp.sum(lq2_ref[...] * lk2_ref[...], axis=-1, keepdims=True))
               + LAM_INIT)
        o_t = map_out(0) - lam * map_out(1)
        o_t = o_t * lax.rsqrt(jnp.mean(o_t * o_t, axis=0, keepdims=True) + SUBLN_EPS)
        o = o_t.T * (nw_ref[...] * (1.0 - LAM_INIT)) * _silu(dz_ref[...].astype(F32))
        o_ref[...] = o.astype(o_ref.dtype)

    def pipeline(scores_step, pv_value, steps_after_diag):
        def scores_block(j, slot, masked=False):
            for c in range(n_chains):
                scores_step(c, j, slot, masked)

        def pv_block(j, slot):
            for c in range(n_chains):
                acc_scr[c] = pv_value(c, j, slot)

        acc_scr[...] = jnp.zeros_like(acc_scr)
        scores_block(i, 0, masked=True)
        n = steps_after_diag()
        last = i - n

        def pair(u, carry):
            j = i - 2 * u - 1
            scores_block(j, 1)
            pv_block(j + 1, 0)
            scores_block(j - 1, 0)
            pv_block(j, 1)
            return carry

        lax.fori_loop(0, n // 2, pair, 0)

        @pl.when(n % 2 == 1)
        def _():
            scores_block(last, 1)
            pv_block(last + 1, 0)

        finish([pv_value(c, last, n % 2) for c in range(n_chains)])

    use_fixed_ref = flag_smem[0] == 1

    @pl.when(use_fixed_ref)
    def _():
        pos_q = (lax.broadcasted_iota(jnp.int32, (tq, LANES), 0) + i * tq).astype(F32) * slope
        neg_ref = -(qk_bound + ATT_SKIP_SLACK + pos_q)
        r_hi = neg_ref.astype(BF16).astype(F32)
        r1 = neg_ref - r_hi
        r_mid = r1.astype(BF16).astype(F32)
        qmaps = query_maps((r_hi, r_mid, r1 - r_mid))

        def scores_step(c, j, slot, masked):
            store_p(slot, c, jnp.exp2(scores_t(qmaps, c, j, masked)).astype(BF16))

        def pv_value(c, j, slot):
            return acc_scr[c] + _dot(vt_scr[j], p_scr[slot, c])

        pipeline(scores_step, pv_value, lambda: jnp.minimum(i, n_cap))

    @pl.when(jnp.logical_not(use_fixed_ref))
    def _():
        qmaps = query_maps(None)
        m_scr[...] = jnp.full(m_scr.shape, MASK_NEG, F32)

        def scores_step(c, j, slot, masked):
            st = scores_t(qmaps, c, j, masked)
            m_prev = m_scr[c]
            m_new = jnp.maximum(m_prev, jnp.max(st, axis=0, keepdims=True))
            store_p(slot, c, jnp.exp2(st - m_new).astype(BF16))
            alpha_scr[slot, c] = jnp.exp2(m_prev - m_new)
            m_scr[c] = m_new

        def pv_value(c, j, slot):
            pv = _dot(vt_scr[j], p_scr[slot, c])
            return acc_scr[c] * alpha_scr[slot, c] + pv

        def steps_after_diag():
            m_min = jnp.min(jnp.min(m_scr[...], axis=0), axis=-1, keepdims=True)
            need = m_min - (ATT_SKIP_EXP2 + ATT_SKIP_SLACK) - qk_bound
            j_first = jnp.ceil(need / (slope * tk) - (tk - 1) / tk)
            n_arr = jnp.clip(i.astype(F32) - jnp.maximum(j_first, 0.0), 0.0, i.astype(F32))
            return jnp.max(n_arr).astype(jnp.int32)

        pipeline(scores_step, pv_value, steps_after_diag)


def _diff_attention(dq, dk, dv, dz, norm_w, lq1, lk1, lq2, lk2, batch, seq):
    tq, tk = ATT_TQ, ATT_TK
    nq = seq // tq
    n_chains = 2 * (tq // ATT_ROWS)
    qspec = pl.BlockSpec((tq, DIFF_DV), lambda b, h, i: (b * nq + i, h))
    kvspec = pl.BlockSpec((seq, DIFF_DV), lambda b, h, i: (b, h))
    full = lambda a: pl.BlockSpec(a.shape, lambda b, h, i: (0, 0))
    return pl.pallas_call(
        _attn_kernel,
        grid=(batch, DIFF_HEADS, nq),
        in_specs=[qspec, kvspec, kvspec, kvspec, qspec, full(norm_w), full(lq1), full(lk1),
                  full(lq2), full(lk2)],
        out_specs=qspec,
        out_shape=jax.ShapeDtypeStruct((batch * seq, DIFF_WIDTH), BF16),
        scratch_shapes=[pltpu.VMEM((2, seq, DIFF_DV), BF16),
                        pltpu.VMEM((seq // tk, DIFF_DV + ATT_ONES, tk), BF16),
                        pltpu.VMEM((n_chains, 1, ATT_ROWS), F32),
                        pltpu.VMEM((n_chains, DIFF_DV + ATT_ONES, ATT_ROWS), F32),
                        pltpu.VMEM((2, n_chains, tk, ATT_ROWS), BF16),
                        pltpu.VMEM((2, n_chains, 1, ATT_ROWS), F32),
                        pltpu.VMEM((8, LANES), F32),
                        pltpu.SMEM((1,), jnp.int32)],
        compiler_params=pltpu.CompilerParams(
            dimension_semantics=("arbitrary", "arbitrary", "arbitrary"),
            vmem_limit_bytes=VMEM_LIMIT),
        name="diff_attention",
    )(dq, dq, dk, dv, dz, norm_w, lq1, lk1, lq2, lk2)


def _out_kernel(x_ref, oa_ref, ob_ref, mab_ref, wa_ref, wb_ref, wo_ref, g_ref, b_ref, y_ref,
                merged_scr):
    for c0 in range(0, D_MODEL, OUT_CHUNK):
        cs = slice(c0, c0 + OUT_CHUNK)
        ya = _dot(oa_ref[...], wa_ref[:, cs])
        yb = _dot(ob_ref[...], wb_ref[:, cs])
        ga = _sigmoid(mab_ref[:, c0:c0 + OUT_CHUNK].astype(F32))
        gb = _sigmoid(mab_ref[:, D_MODEL + c0:D_MODEL + c0 + OUT_CHUNK].astype(F32))
        merged_scr[:, cs] = (ga * ya + gb * yb).astype(BF16)
    r0 = 0
    for nrows in OUT_ROW_GROUPS:
        rs = slice(r0, r0 + nrows)
        r0 += nrows
        y = _dot(merged_scr[rs, :], wo_ref[...])
        z = DEEPNORM_ALPHA * x_ref[rs, :] + y
        mu = jnp.mean(z, axis=-1, keepdims=True)
        zc = z - mu
        var = jnp.mean(zc * zc, axis=-1, keepdims=True)
        y_ref[rs, :] = zc * lax.rsqrt(var + LN_EPS) * g_ref[...] + b_ref[...]


def _output(x2, o_a, o_b, mab, wa, wb, wo, ln_g, ln_b):
    m = x2.shape[0]
    tm = OUT_TM
    row = lambda width: pl.BlockSpec((tm, width), lambda i: (i, 0))
    full = lambda a: pl.BlockSpec(a.shape, lambda i: (0, 0))
    return pl.pallas_call(
        _out_kernel,
        grid=(m // tm,),
        in_specs=[row(D_MODEL), row(GDN_WIDTH), row(DIFF_WIDTH), row(2 * D_MODEL),
                  full(wa), full(wb), full(wo), full(ln_g), full(ln_b)],
        out_specs=row(D_MODEL),
        out_shape=jax.ShapeDtypeStruct((m, D_MODEL), F32),
        scratch_shapes=[pltpu.VMEM((tm, D_MODEL), BF16)],
        compiler_params=pltpu.CompilerParams(
            dimension_semantics=("arbitrary",), vmem_limit_bytes=VMEM_LIMIT),
        name="merge_out_ln",
    )(x2, o_a, o_b, mab, wa, wb, wo, ln_g, ln_b)


def _lane_row(v):
    return jnp.zeros((1, LANES), F32).at[0, :v.shape[0]].set(v.astype(F32))


def kernel(x, w_in, conv_w, a_log, dt_bias, gdn_norm_w, w_up_a, lambda_q1, lambda_k1,
           lambda_q2, lambda_k2, diff_norm_w, w_up_b, w_out, ln_g, ln_b):
    batch, seq, d = x.shape
    x2 = x.reshape(batch * seq, d)
    layer = 0
    w = w_in[layer].astype(BF16)
    w_tail = w[:, PROJ_HEAD + 2 * GDN_HEADS:]

    gq, gk, gv, hab, gz, dq, dk, dv, dz, mab = _project(
        x2, w, w_tail, conv_w[layer].astype(F32), seq)

    o_a = _gdn(gq, gk, gv, hab, gz, _lane_row(a_log[layer]), _lane_row(dt_bias[layer]),
               gdn_norm_w[layer].reshape(1, GDN_DV).astype(F32), batch, seq)

    o_b = _diff_attention(
        dq, dk, dv, dz, diff_norm_w[layer].reshape(1, DIFF_DV).astype(F32),
        lambda_q1[layer].reshape(1, DIFF_DH).astype(F32),
        lambda_k1[layer].reshape(1, DIFF_DH).astype(F32),
        lambda_q2[layer].reshape(1, DIFF_DH).astype(F32),
        lambda_k2[layer].reshape(1, DIFF_DH).astype(F32), batch, seq)

    y = _output(x2, o_a, o_b, mab, w_up_a[layer].astype(BF16), w_up_b[layer].astype(BF16),
                w_out[layer].astype(BF16), ln_g[layer].reshape(1, d).astype(F32),
                ln_b[layer].reshape(1, d).astype(F32))
    return y.reshape(batch, seq, d)
```

```python
import functools
import math

import jax
import jax.numpy as jnp
from jax import lax
from jax.experimental import pallas as pl
from jax.experimental.pallas import tpu as pltpu

F32 = jnp.float32
BF16 = jnp.bfloat16

D_MODEL = 1024
GDN_HEADS = 4
GDN_DK = 128
GDN_DV = 128
GDN_QK = GDN_HEADS * GDN_DK
GDN_WIDTH = GDN_HEADS * GDN_DV
CONV_K = 4
DIFF_HEADS = 4
DIFF_DH = 64
DIFF_DV = 2 * DIFF_DH
DIFF_QK = DIFF_HEADS * 2 * DIFF_DH
DIFF_WIDTH = DIFF_HEADS * DIFF_DV
NORM_EPS = 1e-6
SUBLN_EPS = 1e-5
LN_EPS = 1e-5
DEPTH = 1
DEEPNORM_ALPHA = (2.0 * DEPTH) ** 0.25
LAM_INIT = 0.8 - 0.6 * math.exp(-0.3 * 0)
ALIBI_SLOPES = tuple(2.0 ** (-8.0 * (i + 1) / DIFF_HEADS) for i in range(DIFF_HEADS))
LOG2E = math.log2(math.e)

LANES = 128
CONV_HIST = 8
VMEM_LIMIT = 56 * 1024 * 1024

PROJ_TM = 256
PROJ_HEAD = 2 * GDN_QK + GDN_WIDTH
GDN_T = 256
GDN_STEP_CHUNKS = 2
GDN_INV_BASE = 16
ATT_TQ = 512
ATT_TK = 512
ATT_ROWS = 256
ATT_ONES = 16
OUT_TM = 512
OUT_CHUNK = 256
OUT_ROW_GROUPS = (256, 256)
assert sum(OUT_ROW_GROUPS) == OUT_TM
MASK_NEG = -1e30
ATT_SKIP_EXP2 = 150.0
ATT_SKIP_SLACK = 1.0
ATT_FIXED_REF_MAX = 48.0


def _fixed_ref_steps(head):
    slope = ALIBI_SLOPES[head] * LOG2E
    return max(math.ceil(ATT_SKIP_EXP2 / (slope * ATT_TK) + (ATT_TK - 1) / ATT_TK) - 1, 0)


def _sigmoid(x):
    return 1.0 / (1.0 + jnp.exp(-x))


def _silu(x):
    return x * _sigmoid(x)


def _dot(a, b):
    return jnp.dot(a, b, preferred_element_type=F32)


def _dot_nt(a, b):
    return lax.dot_general(a, b, (((1,), (1,)), ((), ())), preferred_element_type=F32)


def _proj_kernel(tiles_per_seq, x_ref, wh_ref, wt_ref, cw_ref, gq_ref, gk_ref, gv_ref,
                 hab_ref, gz_ref, dq_ref, dk_ref, dv_ref, dz_ref, mab_ref, hist_scr, xb_scr):
    tm = PROJ_TM

    @pl.when(pl.program_id(0) % tiles_per_seq == 0)
    def _():
        hist_scr[...] = jnp.zeros_like(hist_scr)

    xb_scr[...] = x_ref[...].astype(BF16)

    def mm(c0, width):
        if c0 < PROJ_HEAD:
            return _dot(xb_scr[...], wh_ref[:, c0:c0 + width])
        return _dot(xb_scr[...], wt_ref[:, c0 - PROJ_HEAD:c0 - PROJ_HEAD + width])

    piece = 2 * GDN_DK

    def conv_silu(c0):
        cols = slice(c0, c0 + piece)
        acc = mm(c0, piece)
        ext = jnp.concatenate([hist_scr[:, cols], acc], axis=0)
        hist_scr[:, cols] = acc[tm - CONV_HIST:]
        y = None
        for j in range(CONV_K):
            r0 = CONV_HIST - (CONV_K - 1) + j
            term = cw_ref[j:j + 1, cols] * ext[r0:r0 + tm]
            y = term if y is None else y + term
        return _silu(y)

    def l2norm_heads(y, scale):
        parts = []
        for h in range(piece // GDN_DK):
            v = y[:, h * GDN_DK:(h + 1) * GDN_DK]
            inv = lax.rsqrt(jnp.sum(v * v, axis=-1, keepdims=True) + NORM_EPS)
            parts.append(v * (inv * scale) if scale != 1.0 else v * inv)
        return jnp.concatenate(parts, axis=1)

    def plain(out_ref, c0, scale=None):
        for j in range(out_ref.shape[1] // 512):
            acc = mm(c0 + j * 512, 512)
            if scale is not None:
                acc = acc * scale
            out_ref[:, j * 512:(j + 1) * 512] = acc.astype(out_ref.dtype)

    for half in range(2):
        c0 = half * piece
        gq_ref[:, c0:c0 + piece] = l2norm_heads(conv_silu(c0), GDN_DK ** -0.5).astype(BF16)
        if half == 0:
            plain(gz_ref, 1536)
        else:
            plain(dq_ref, 2048, DIFF_DH ** -0.5 * LOG2E)
    for half in range(2):
        c0 = half * piece
        gk_ref[:, c0:c0 + piece] = l2norm_heads(conv_silu(GDN_QK + c0), 1.0).astype(BF16)
        plain(dk_ref if half == 0 else dv_ref, 2560 + half * 512)
    for half in range(2):
        c0 = half * piece
        gv_ref[:, c0:c0 + piece] = conv_silu(2 * GDN_QK + c0).astype(BF16)
        if half == 0:
            plain(dz_ref, 3584)
    hab_ref[...] = _dot(xb_scr[...], wh_ref[:, PROJ_HEAD:PROJ_HEAD + LANES])
    plain(mab_ref, 4096)


def _project(x2, w_bf, w_tail, conv_w, seq):
    m = x2.shape[0]
    tm = PROJ_TM
    row = lambda width: pl.BlockSpec((tm, width), lambda i: (i, 0))
    full = lambda a: pl.BlockSpec(a.shape, lambda i: (0, 0))
    out_shape = (
        jax.ShapeDtypeStruct((m, GDN_QK), BF16),
        jax.ShapeDtypeStruct((m, GDN_QK), BF16),
        jax.ShapeDtypeStruct((m, GDN_WIDTH), BF16),
        jax.ShapeDtypeStruct((m, LANES), F32),
        jax.ShapeDtypeStruct((m, 512), BF16),
        jax.ShapeDtypeStruct((m, 512), BF16),
        jax.ShapeDtypeStruct((m, 512), BF16),
        jax.ShapeDtypeStruct((m, 512), BF16),
        jax.ShapeDtypeStruct((m, 512), BF16),
        jax.ShapeDtypeStruct((m, 2048), BF16),
    )
    return pl.pallas_call(
        functools.partial(_proj_kernel, seq // tm),
        grid=(m // tm,),
        in_specs=[row(D_MODEL),
                  pl.BlockSpec((D_MODEL, PROJ_HEAD + LANES), lambda i: (0, 0)),
                  full(w_tail), full(conv_w)],
        out_specs=tuple(row(s.shape[1]) for s in out_shape),
        out_shape=out_shape,
        scratch_shapes=[pltpu.VMEM((CONV_HIST, 3 * GDN_QK), F32),
                        pltpu.VMEM((tm, D_MODEL), BF16)],
        compiler_params=pltpu.CompilerParams(
            dimension_semantics=("arbitrary",), vmem_limit_bytes=VMEM_LIMIT),
        name="in_proj",
    )(x2, w_bf, w_tail, conv_w)


def _gdn_kernel(gq_ref, gk_ref, gv_ref, hab_ref, gz_ref, alog_ref, dtb_ref, nw_ref, o_ref,
                st_scr):
    @pl.when(pl.program_id(1) == 0)
    def _():
        st_scr[...] = jnp.zeros_like(st_scr)

    for ck in range(GDN_STEP_CHUNKS):
        r = pl.ds(ck * GDN_T, GDN_T)
        prep = _gdn_prep(gq_ref.at[r], gk_ref.at[r], gv_ref.at[r], hab_ref.at[r], alog_ref,
                         dtb_ref)
        _gdn_apply(prep, gz_ref.at[r], nw_ref, o_ref.at[r], st_scr)


def _gdn_prep(gq_ref, gk_ref, gv_ref, hab_ref, alog_ref, dtb_ref):
    t = GDN_T
    heads = range(GDN_HEADS)

    hab = hab_ref[...]
    xg = hab + dtb_ref[...]
    y_sp = jnp.exp(-jnp.abs(xg))
    u_sp = 1.0 + y_sp
    softplus = jnp.maximum(xg, 0.0) + (jnp.log(u_sp) - ((u_sp - 1.0) - y_sp) / u_sp)
    g_full = -jnp.exp(alog_ref[...]) * softplus
    beta_full = _sigmoid(hab)

    ri = lax.broadcasted_iota(jnp.int32, (t, t), 0)
    ci = lax.broadcasted_iota(jnp.int32, (t, t), 1)
    causal = ri >= ci
    strict = ri > ci
    eye = jnp.where(ri == ci, 1.0, 0.0).astype(F32)

    tri = jnp.where(causal, 1.0, 0.0).astype(BF16)
    g_hi = g_full.astype(BF16)
    g_r1 = g_full - g_hi.astype(F32)
    g_mid = g_r1.astype(BF16)
    g_lo = (g_r1 - g_mid.astype(F32)).astype(BF16)
    gc = _dot(tri, g_hi) + _dot(tri, g_mid) + _dot(tri, g_lo)
    gc_t = gc.T
    eg = jnp.exp(gc)
    g_last = gc[t - 1:t, :]
    k_dec = jnp.exp(g_last - gc)
    eg_last = jnp.exp(g_last)

    qn_b = [gq_ref[:, h * GDN_DK:(h + 1) * GDN_DK] for h in heads]
    kn_b = [gk_ref[:, h * GDN_DK:(h + 1) * GDN_DK] for h in heads]
    qn = [qn_b[h].astype(F32) for h in heads]
    kn = [kn_b[h].astype(F32) for h in heads]
    vv = [gv_ref[:, h * GDN_DV:(h + 1) * GDN_DV].astype(F32) for h in heads]

    bcol = [beta_full[:, GDN_HEADS + h:GDN_HEADS + h + 1] for h in heads]
    egc = [eg[:, h:h + 1] for h in heads]
    kb = [kn[h] * bcol[h] for h in heads]
    kq = [_dot_nt(jnp.concatenate([kb[h].astype(BF16), qn_b[h]], axis=0), kn_b[h])
          for h in heads]
    decay = []
    for h in heads:
        gd = gc[:, h:h + 1] - gc_t[h:h + 1, :]
        decay.append(jnp.where(causal, jnp.exp(jnp.where(causal, gd, 0.0)), 0.0))
    nmat = [jnp.where(strict, -(kq[h][:t] * decay[h]), 0.0) for h in heads]
    amat = [(kq[h][t:] * decay[h]).astype(BF16) for h in heads]
    rhs = [jnp.concatenate([vv[h] * bcol[h], kb[h] * egc[h]], axis=1).astype(BF16)
           for h in heads]
    qe = [qn[h] * egc[h] for h in heads]
    kd_t = [(kn[h] * k_dec[:, h:h + 1]).T.astype(BF16) for h in heads]

    def same_block(b):
        sh = int(math.log2(b))
        return (ri >> sh) == (ci >> sh)

    blk = GDN_INV_BASE
    in_blk = same_block(blk)
    n0 = [jnp.where(in_blk, nmat[h], 0.0) for h in heads]
    x = [eye + n0[h] for h in heads]
    pw = [n0[h].astype(BF16) for h in heads]
    for _ in range(int(math.log2(blk)) - 1):
        p32 = [_dot(pw[h], pw[h]) for h in heads]
        pw = [p32[h].astype(BF16) for h in heads]
        x = [x[h] + _dot(x[h].astype(BF16), pw[h]) for h in heads]
    while blk < t:
        in_big = same_block(2 * blk)
        n_off = [jnp.where(in_big, jnp.where(in_blk, 0.0, nmat[h]), 0.0).astype(BF16)
                 for h in heads]
        xb = [x[h].astype(BF16) for h in heads]
        xn = [_dot(xb[h], n_off[h]).astype(BF16) for h in heads]
        x = [x[h] + _dot(xn[h], xb[h]) for h in heads]
        in_blk = in_big
        blk *= 2
    sol = [_dot(x[h].astype(BF16), rhs[h]) for h in heads]
    return sol, qe, amat, kd_t, eg_last


def _gdn_apply(prep, gz_ref, nw_ref, o_ref, st_scr):
    t = GDN_T
    heads = range(GDN_HEADS)
    sol, qe, amat, kd_t, eg_last = prep
    s_prev = [st_scr[h] for h in heads]
    ws = [_dot(jnp.concatenate([sol[h][:, GDN_DV:], qe[h]], axis=0).astype(BF16),
               s_prev[h].astype(BF16)) for h in heads]
    v_new = [(sol[h][:, :GDN_DV] - ws[h][:t]).astype(BF16) for h in heads]
    o_l = [ws[h][t:] + _dot(amat[h], v_new[h]) for h in heads]
    for h in heads:
        st_scr[h] = s_prev[h] * eg_last[:, h:h + 1] + _dot(kd_t[h], v_new[h])
    nw = nw_ref[...]
    for h in heads:
        o = o_l[h]
        o = o * lax.rsqrt(jnp.mean(o * o, axis=-1, keepdims=True) + NORM_EPS) * nw
        gate = _silu(gz_ref[:, h * GDN_DV:(h + 1) * GDN_DV].astype(F32))
        o_ref[:, h * GDN_DV:(h + 1) * GDN_DV] = (o * gate).astype(o_ref.dtype)


def _gdn(gq, gk, gv, hab, gz, alog_row, dtb_row, norm_w, batch, seq):
    t = GDN_T * GDN_STEP_CHUNKS
    nt = seq // t
    row = lambda width: pl.BlockSpec((t, width), lambda b, s: (b * nt + s, 0))
    full = lambda a: pl.BlockSpec(a.shape, lambda b, s: (0, 0))
    return pl.pallas_call(
        _gdn_kernel,
        grid=(batch, nt),
        in_specs=[row(GDN_QK), row(GDN_QK), row(GDN_WIDTH), row(LANES), row(GDN_WIDTH),
                  full(alog_row), full(dtb_row), full(norm_w)],
        out_specs=row(GDN_WIDTH),
        out_shape=jax.ShapeDtypeStruct((batch * seq, GDN_WIDTH), BF16),
        scratch_shapes=[pltpu.VMEM((GDN_HEADS, GDN_DK, GDN_DV), F32)],
        compiler_params=pltpu.CompilerParams(
            dimension_semantics=("arbitrary", "arbitrary"), vmem_limit_bytes=VMEM_LIMIT),
        name="gated_deltanet",
    )(gq, gk, gv, hab, gz, alog_row, dtb_row, norm_w)


def _attn_kernel(q_ref, qall_ref, k_ref, v_ref, dz_ref, qtab_ref, ktab_ref, nw_ref, lq1_ref,
                 lk1_ref, lq2_ref, lk2_ref, o_ref, ke_scr, vt_scr, m_scr, acc_scr, p_scr, alpha_scr, bound_scr,
                 flag_smem):
    tq, tk, cols = ATT_TQ, ATT_TK, ATT_ROWS
    n_half = tq // cols
    seq = k_ref.shape[0]
    h = pl.program_id(1)
    i = pl.program_id(2)
    slope = jnp.float32(ALIBI_SLOPES[DIFF_HEADS - 1] * LOG2E)
    n_cap = jnp.int32(_fixed_ref_steps(DIFF_HEADS - 1))
    for hh in range(DIFF_HEADS - 1):
        slope = jnp.where(h == hh, jnp.float32(ALIBI_SLOPES[hh] * LOG2E), slope)
        n_cap = jnp.where(h == hh, jnp.int32(_fixed_ref_steps(hh)), n_cap)

    lane = lax.broadcasted_iota(jnp.int32, (tk, LANES), 1)

    def half_norm_max(vf):
        hl = lax.broadcasted_iota(jnp.int32, (LANES, LANES), 0)
        hc = lax.broadcasted_iota(jnp.int32, (LANES, LANES), 1)
        pick = jnp.where(hc == jnp.where(hl < DIFF_DH, 0, 1), 1.0, 0.0).astype(BF16)
        sums = jnp.max(_dot((vf * vf).astype(BF16), pick), axis=0, keepdims=True)
        return jnp.sqrt(jnp.maximum(sums[:, 0:1], sums[:, 1:2])) * (1.0 + 2.0 ** -6)

    @pl.when(i == 0)
    def _():
        bound_scr[...] = jnp.zeros_like(bound_scr)

        def build(blk, carry):
            r0 = pl.multiple_of(blk * tk, tk)
            kf = k_ref[pl.ds(r0, tk), :].astype(F32)
            bound_scr[0:1, :] = jnp.maximum(bound_scr[0:1, :], half_norm_max(kf))
            bound_scr[1:2, :] = jnp.maximum(
                bound_scr[1:2, :], half_norm_max(qall_ref[pl.ds(r0, tk), :].astype(F32)))
            feat = ktab_ref[pl.ds(r0, tk), :].astype(F32)
            for mp in range(2):
                own = (lane < DIFF_DH) if mp == 0 else (lane >= DIFF_DH)
                ke_scr[mp, pl.ds(r0, tk), :] = jnp.where(own, kf, feat).astype(BF16)
            vt = v_ref[pl.ds(r0, tk), :].astype(F32).T
            vt_scr[blk, 0:DIFF_DV, :] = vt.astype(BF16)
            vt_scr[blk, DIFF_DV:, :] = jnp.ones((ATT_ONES, tk), BF16)
            return carry

        lax.fori_loop(0, seq // tk, build, 0, unroll=2)
        qk_all = bound_scr[0:1, :] * bound_scr[1:2, :]
        bound_scr[2:3, :] = qk_all
        flag_smem[0] = jnp.where(jnp.max(qk_all) <= ATT_FIXED_REF_MAX, 1, 0).astype(jnp.int32)

    qf = q_ref[...].astype(F32)
    qk_bound = bound_scr[2:3, 0:1]
    chains = [(mp, r) for mp in range(2) for r in range(n_half)]
    n_chains = len(chains)

    def query_maps(feat):
        return [jnp.where((lane < DIFF_DH) if mp == 0 else (lane >= DIFF_DH), qf, feat)
                .astype(BF16) for mp in range(2)]

    def scores_t(qmaps, c, j, masked):
        mp, r = chains[c]
        nk = (r + 1) * cols if masked else tk
        qc = qmaps[mp][r * cols:(r + 1) * cols]
        kb = ke_scr[mp, pl.ds(pl.multiple_of(j * tk, tk), nk), :]
        st = _dot_nt(kb, qc)
        if masked:
            key_l = lax.broadcasted_iota(jnp.int32, (nk, cols), 0)
            qry_l = lax.broadcasted_iota(jnp.int32, (nk, cols), 1)
            st = jnp.where(key_l <= qry_l + r * cols, st, MASK_NEG)
        return st

    def store_p(slot, c, p):
        nk = p.shape[0]
        p_scr[slot, c, 0:nk, :] = p
        if nk < tk:
            p_scr[slot, c, nk:, :] = jnp.zeros((tk - nk, cols), BF16)

    def finish(acc_fin):
        def map_out(mp):
            parts = []
            for r in range(n_half):
                acc = acc_fin[mp * n_half + r]
                parts.append(acc[:DIFF_DV] * (1.0 / acc[DIFF_DV:DIFF_DV + 1]))
            return jnp.concatenate(parts, axis=1)

        lam = (jnp.exp(jnp.sum(lq1_ref[...] * lk1_ref[...], axis=-1, keepdims=True))
               - jnp.exp(jnp.sum(lq2_ref[...] * lk2_ref[...], axis=-1, keepdims=True))
               + LAM_INIT)
        o_t = map_out(0) - lam * map_out(1)
        o_t = o_t * lax.rsqrt(jnp.mean(o_t * o_t, axis=0, keepdims=True) + SUBLN_EPS)
        o = o_t.T * (nw_ref[...] * (1.0 - LAM_INIT)) * _silu(dz_ref[...].astype(F32))
        o_ref[...] = o.astype(o_ref.dtype)

    def pipeline(scores_step, pv_value, steps_after_diag):
        def scores_block(j, slot, masked=False):
            for c in range(n_chains):
                scores_step(c, j, slot, masked)

        def pv_block(j, slot):
            for c in range(n_chains):
                acc_scr[c] = pv_value(c, j, slot)

        acc_scr[...] = jnp.zeros_like(acc_scr)
        scores_block(i, 0, masked=True)
        n = steps_after_diag()
        last = i - n

        def pair(u, carry):
            j = i - 2 * u - 1
            scores_block(j, 1)
            pv_block(j + 1, 0)
            scores_block(j - 1, 0)
            pv_block(j, 1)
            return carry

        lax.fori_loop(0, n // 2, pair, 0)

        @pl.when(n % 2 == 1)
        def _():
            scores_block(last, 1)
            pv_block(last + 1, 0)

        finish([pv_value(c, last, n % 2) for c in range(n_chains)])

    use_fixed_ref = flag_smem[0] == 1

    @pl.when(use_fixed_ref)
    def _():
        c_ref = -(qk_bound + ATT_SKIP_SLACK)
        c_hi = c_ref.astype(BF16).astype(F32)
        c1 = c_ref - c_hi
        c_mid = c1.astype(BF16).astype(F32)
        fl = lax.broadcasted_iota(jnp.int32, (1, LANES), 1) % DIFF_DH
        c_row = jnp.where(fl == 6, c_hi, jnp.where(fl == 7, c_mid,
                                                   jnp.where(fl == 8, c1 - c_mid, 0.0)))
        qmaps = query_maps(qtab_ref[...].astype(F32) + c_row)

        def scores_step(c, j, slot, masked):
            store_p(slot, c, jnp.exp2(scores_t(qmaps, c, j, masked)).astype(BF16))

        def pv_value(c, j, slot):
            return acc_scr[c] + _dot(vt_scr[j], p_scr[slot, c])

        pipeline(scores_step, pv_value, lambda: jnp.minimum(i, n_cap))

    @pl.when(jnp.logical_not(use_fixed_ref))
    def _():
        qmaps = query_maps(jnp.where(lane % DIFF_DH < 3, 1.0, 0.0))
        m_scr[...] = jnp.full(m_scr.shape, MASK_NEG, F32)

        def scores_step(c, j, slot, masked):
            st = scores_t(qmaps, c, j, masked)
            m_prev = m_scr[c]
            m_new = jnp.maximum(m_prev, jnp.max(st, axis=0, keepdims=True))
            store_p(slot, c, jnp.exp2(st - m_new).astype(BF16))
            alpha_scr[slot, c] = jnp.exp2(m_prev - m_new)
            m_scr[c] = m_new

        def pv_value(c, j, slot):
            pv = _dot(vt_scr[j], p_scr[slot, c])
            return acc_scr[c] * alpha_scr[slot, c] + pv

        def steps_after_diag():
            m_min = jnp.min(jnp.min(m_scr[...], axis=0), axis=-1, keepdims=True)
            need = m_min - (ATT_SKIP_EXP2 + ATT_SKIP_SLACK) - qk_bound
            j_first = jnp.ceil(need / (slope * tk) - (tk - 1) / tk)
            n_arr = jnp.clip(i.astype(F32) - jnp.maximum(j_first, 0.0), 0.0, i.astype(F32))
            return jnp.max(n_arr).astype(jnp.int32)

        pipeline(scores_step, pv_value, steps_after_diag)


def _alibi_tables(seq):
    pos = jnp.arange(seq, dtype=F32)[None, :, None]
    slopes = jnp.asarray([s * LOG2E for s in ALIBI_SLOPES], F32)[:, None, None]
    sp = slopes * pos

    def top_bits(v):
        bits = lax.bitcast_convert_type(v, jnp.uint32) & jnp.uint32(0xFFFF0000)
        return lax.bitcast_convert_type(bits, F32)

    hi = top_bits(sp)
    r1 = sp - hi
    mid = top_bits(r1)
    lo = r1 - mid
    fl = (jnp.arange(LANES) % DIFF_DH)[None, None, :]
    ktab = jnp.where(fl == 0, hi, jnp.where(fl == 1, mid, jnp.where(
        fl == 2, lo, jnp.where(fl < 9, 1.0, 0.0))))
    qtab = jnp.where(fl < 3, 1.0, jnp.where(fl == 3, -hi, jnp.where(
        fl == 4, -mid, jnp.where(fl == 5, -lo, 0.0))))
    shape = (DIFF_HEADS * seq, LANES)
    return qtab.astype(BF16).reshape(shape), ktab.astype(BF16).reshape(shape)


def _diff_attention(dq, dk, dv, dz, norm_w, lq1, lk1, lq2, lk2, batch, seq):
    tq, tk = ATT_TQ, ATT_TK
    nq = seq // tq
    n_chains = 2 * (tq // ATT_ROWS)
    qtab, ktab = _alibi_tables(seq)
    qspec = pl.BlockSpec((tq, DIFF_DV), lambda b, h, i: (b * nq + i, h))
    kvspec = pl.BlockSpec((seq, DIFF_DV), lambda b, h, i: (b, h))
    qtab_spec = pl.BlockSpec((tq, LANES), lambda b, h, i: (h * nq + i, 0))
    ktab_spec = pl.BlockSpec((seq, LANES), lambda b, h, i: (h, 0))
    full = lambda a: pl.BlockSpec(a.shape, lambda b, h, i: (0, 0))
    return pl.pallas_call(
        _attn_kernel,
        grid=(batch, DIFF_HEADS, nq),
        in_specs=[qspec, kvspec, kvspec, kvspec, qspec, qtab_spec, ktab_spec, full(norm_w),
                  full(lq1), full(lk1), full(lq2), full(lk2)],
        out_specs=qspec,
        out_shape=jax.ShapeDtypeStruct((batch * seq, DIFF_WIDTH), BF16),
        scratch_shapes=[pltpu.VMEM((2, seq, DIFF_DV), BF16),
                        pltpu.VMEM((seq // tk, DIFF_DV + ATT_ONES, tk), BF16),
                        pltpu.VMEM((n_chains, 1, ATT_ROWS), F32),
                        pltpu.VMEM((n_chains, DIFF_DV + ATT_ONES, ATT_ROWS), F32),
                        pltpu.VMEM((2, n_chains, tk, ATT_ROWS), BF16),
                        pltpu.VMEM((2, n_chains, 1, ATT_ROWS), F32),
                        pltpu.VMEM((8, LANES), F32),
                        pltpu.SMEM((1,), jnp.int32)],
        compiler_params=pltpu.CompilerParams(
            dimension_semantics=("arbitrary", "arbitrary", "arbitrary"),
            vmem_limit_bytes=VMEM_LIMIT),
        name="diff_attention",
    )(dq, dq, dk, dv, dz, qtab, ktab, norm_w, lq1, lk1, lq2, lk2)


def _out_kernel(x_ref, oa_ref, ob_ref, mab_ref, wa_ref, wb_ref, wo_ref, g_ref, b_ref, y_ref,
                merged_scr):
    for c0 in range(0, D_MODEL, OUT_CHUNK):
        cs = slice(c0, c0 + OUT_CHUNK)
        ya = _dot(oa_ref[...], wa_ref[:, cs])
        yb = _dot(ob_ref[...], wb_ref[:, cs])
        ga = _sigmoid(mab_ref[:, c0:c0 + OUT_CHUNK].astype(F32))
        gb = _sigmoid(mab_ref[:, D_MODEL + c0:D_MODEL + c0 + OUT_CHUNK].astype(F32))
        merged_scr[:, cs] = (ga * ya + gb * yb).astype(BF16)
    r0 = 0
    for nrows in OUT_ROW_GROUPS:
        rs = slice(r0, r0 + nrows)
        r0 += nrows
        y = _dot(merged_scr[rs, :], wo_ref[...])
        z = DEEPNORM_ALPHA * x_ref[rs, :] + y
        mu = jnp.mean(z, axis=-1, keepdims=True)
        zc = z - mu
        var = jnp.mean(zc * zc, axis=-1, keepdims=True)
        y_ref[rs, :] = zc * lax.rsqrt(var + LN_EPS) * g_ref[...] + b_ref[...]


def _output(x2, o_a, o_b, mab, wa, wb, wo, ln_g, ln_b):
    m = x2.shape[0]
    tm = OUT_TM
    row = lambda width: pl.BlockSpec((tm, width), lambda i: (i, 0))
    full = lambda a: pl.BlockSpec(a.shape, lambda i: (0, 0))
    return pl.pallas_call(
        _out_kernel,
        grid=(m // tm,),
        in_specs=[row(D_MODEL), row(GDN_WIDTH), row(DIFF_WIDTH), row(2 * D_MODEL),
                  full(wa), full(wb), full(wo), full(ln_g), full(ln_b)],
        out_specs=row(D_MODEL),
        out_shape=jax.ShapeDtypeStruct((m, D_MODEL), F32),
        scratch_shapes=[pltpu.VMEM((tm, D_MODEL), BF16)],
        compiler_params=pltpu.CompilerParams(
            dimension_semantics=("arbitrary",), vmem_limit_bytes=VMEM_LIMIT),
        name="merge_out_ln",
    )(x2, o_a, o_b, mab, wa, wb, wo, ln_g, ln_b)


def _lane_row(v):
    return jnp.zeros((1, LANES), F32).at[0, :v.shape[0]].set(v.astype(F32))


def kernel(x, w_in, conv_w, a_log, dt_bias, gdn_norm_w, w_up_a, lambda_q1, lambda_k1,
           lambda_q2, lambda_k2, diff_norm_w, w_up_b, w_out, ln_g, ln_b):
    batch, seq, d = x.shape
    x2 = x.reshape(batch * seq, d)
    layer = 0
    w = w_in[layer].astype(BF16)
    w_tail = w[:, PROJ_HEAD + 2 * GDN_HEADS:]

    gq, gk, gv, hab, gz, dq, dk, dv, dz, mab = _project(
        x2, w, w_tail, conv_w[layer].astype(F32), seq)

    o_a = _gdn(gq, gk, gv, hab, gz, _lane_row(a_log[layer]), _lane_row(dt_bias[layer]),
               gdn_norm_w[layer].reshape(1, GDN_DV).astype(F32), batch, seq)

    o_b = _diff_attention(
        dq, dk, dv, dz, diff_norm_w[layer].reshape(1, DIFF_DV).astype(F32),
        lambda_q1[layer].reshape(1, DIFF_DH).astype(F32),
        lambda_k1[layer].reshape(1, DIFF_DH).astype(F32),
        lambda_q2[layer].reshape(1, DIFF_DH).astype(F32),
        lambda_k2[layer].reshape(1, DIFF_DH).astype(F32), batch, seq)

    y = _output(x2, o_a, o_b, mab, w_up_a[layer].astype(BF16), w_up_b[layer].astype(BF16),
                w_out[layer].astype(BF16), ln_g[layer].reshape(1, d).astype(F32),
                ln_b[layer].reshape(1, d).astype(F32))
    return y.reshape(batch, seq, d)
```

```python
import functools
import math

import jax
import jax.numpy as jnp
from jax import lax
from jax.experimental import pallas as pl
from jax.experimental.pallas import tpu as pltpu

F32 = jnp.float32
BF16 = jnp.bfloat16

D_MODEL = 1024
GDN_HEADS = 4
GDN_DK = 128
GDN_DV = 128
GDN_QK = GDN_HEADS * GDN_DK
GDN_WIDTH = GDN_HEADS * GDN_DV
CONV_K = 4
DIFF_HEADS = 4
DIFF_DH = 64
DIFF_DV = 2 * DIFF_DH
DIFF_QK = DIFF_HEADS * 2 * DIFF_DH
DIFF_WIDTH = DIFF_HEADS * DIFF_DV
NORM_EPS = 1e-6
SUBLN_EPS = 1e-5
LN_EPS = 1e-5
DEPTH = 1
DEEPNORM_ALPHA = (2.0 * DEPTH) ** 0.25
LAM_INIT = 0.8 - 0.6 * math.exp(-0.3 * 0)
ALIBI_SLOPES = tuple(2.0 ** (-8.0 * (i + 1) / DIFF_HEADS) for i in range(DIFF_HEADS))
LOG2E = math.log2(math.e)

LANES = 128
CONV_HIST = 8
VMEM_LIMIT = 56 * 1024 * 1024

PROJ_TM = 256
PROJ_HEAD = 2 * GDN_QK + GDN_WIDTH
GDN_T = 256
GDN_STEP_CHUNKS = 2
GDN_INV_BASE = 16
ATT_TQ = 512
ATT_TK = 512
ATT_ROWS = 256
OUT_TM = 512
OUT_CHUNK = 256
OUT_ROW_GROUPS = (256, 256)
assert sum(OUT_ROW_GROUPS) == OUT_TM
MASK_NEG = -1e30
ATT_SKIP_EXP2 = 150.0
ATT_SKIP_SLACK = 1.0
ATT_FIXED_REF_MAX = 48.0


def _fixed_ref_steps(head):
    slope = ALIBI_SLOPES[head] * LOG2E
    return max(math.ceil(ATT_SKIP_EXP2 / (slope * ATT_TK) + (ATT_TK - 1) / ATT_TK) - 1, 0)


def _sigmoid(x):
    return 1.0 / (1.0 + jnp.exp(-x))


def _silu(x):
    return x * _sigmoid(x)


def _dot(a, b):
    return jnp.dot(a, b, preferred_element_type=F32)


def _dot_nt(a, b):
    return lax.dot_general(a, b, (((1,), (1,)), ((), ())), preferred_element_type=F32)


def _proj_kernel(tiles_per_seq, x_ref, wh_ref, wt_ref, cw_ref, gq_ref, gk_ref, gv_ref,
                 hab_ref, gz_ref, dq_ref, dk_ref, dv_ref, dz_ref, mab_ref, hist_scr, xb_scr):
    tm = PROJ_TM

    @pl.when(pl.program_id(0) % tiles_per_seq == 0)
    def _():
        hist_scr[...] = jnp.zeros_like(hist_scr)

    xb_scr[...] = x_ref[...].astype(BF16)

    def mm(c0, width):
        if c0 < PROJ_HEAD:
            return _dot(xb_scr[...], wh_ref[:, c0:c0 + width])
        return _dot(xb_scr[...], wt_ref[:, c0 - PROJ_HEAD:c0 - PROJ_HEAD + width])

    piece = 2 * GDN_DK

    def conv_silu(c0):
        cols = slice(c0, c0 + piece)
        acc = mm(c0, piece)
        ext = jnp.concatenate([hist_scr[:, cols], acc], axis=0)
        hist_scr[:, cols] = acc[tm - CONV_HIST:]
        y = None
        for j in range(CONV_K):
            r0 = CONV_HIST - (CONV_K - 1) + j
            term = cw_ref[j:j + 1, cols] * ext[r0:r0 + tm]
            y = term if y is None else y + term
        return _silu(y)

    def l2norm_heads(y, scale):
        parts = []
        for h in range(piece // GDN_DK):
            v = y[:, h * GDN_DK:(h + 1) * GDN_DK]
            inv = lax.rsqrt(jnp.sum(v * v, axis=-1, keepdims=True) + NORM_EPS)
            parts.append(v * (inv * scale) if scale != 1.0 else v * inv)
        return jnp.concatenate(parts, axis=1)

    def plain(out_ref, c0, scale=None):
        for j in range(out_ref.shape[1] // 512):
            acc = mm(c0 + j * 512, 512)
            if scale is not None:
                acc = acc * scale
            out_ref[:, j * 512:(j + 1) * 512] = acc.astype(out_ref.dtype)

    for half in range(2):
        c0 = half * piece
        gq_ref[:, c0:c0 + piece] = l2norm_heads(conv_silu(c0), GDN_DK ** -0.5).astype(BF16)
        if half == 0:
            plain(gz_ref, 1536)
        else:
            plain(dq_ref, 2048, DIFF_DH ** -0.5 * LOG2E)
    for half in range(2):
        c0 = half * piece
        gk_ref[:, c0:c0 + piece] = l2norm_heads(conv_silu(GDN_QK + c0), 1.0).astype(BF16)
        plain(dk_ref if half == 0 else dv_ref, 2560 + half * 512)
    for half in range(2):
        c0 = half * piece
        gv_ref[:, c0:c0 + piece] = conv_silu(2 * GDN_QK + c0).astype(BF16)
        if half == 0:
            plain(dz_ref, 3584)
    hab_ref[...] = _dot(xb_scr[...], wh_ref[:, PROJ_HEAD:PROJ_HEAD + LANES])
    plain(mab_ref, 4096)


def _project(x2, w_bf, w_tail, conv_w, seq):
    m = x2.shape[0]
    tm = PROJ_TM
    row = lambda width: pl.BlockSpec((tm, width), lambda i: (i, 0))
    full = lambda a: pl.BlockSpec(a.shape, lambda i: (0, 0))
    out_shape = (
        jax.ShapeDtypeStruct((m, GDN_QK), BF16),
        jax.ShapeDtypeStruct((m, GDN_QK), BF16),
        jax.ShapeDtypeStruct((m, GDN_WIDTH), BF16),
        jax.ShapeDtypeStruct((m, LANES), F32),
        jax.ShapeDtypeStruct((m, 512), BF16),
        jax.ShapeDtypeStruct((m, 512), BF16),
        jax.ShapeDtypeStruct((m, 512), BF16),
        jax.ShapeDtypeStruct((m, 512), BF16),
        jax.ShapeDtypeStruct((m, 512), BF16),
        jax.ShapeDtypeStruct((m, 2048), BF16),
    )
    return pl.pallas_call(
        functools.partial(_proj_kernel, seq // tm),
        grid=(m // tm,),
        in_specs=[row(D_MODEL),
                  pl.BlockSpec((D_MODEL, PROJ_HEAD + LANES), lambda i: (0, 0)),
                  full(w_tail), full(conv_w)],
        out_specs=tuple(row(s.shape[1]) for s in out_shape),
        out_shape=out_shape,
        scratch_shapes=[pltpu.VMEM((CONV_HIST, 3 * GDN_QK), F32),
                        pltpu.VMEM((tm, D_MODEL), BF16)],
        compiler_params=pltpu.CompilerParams(
            dimension_semantics=("arbitrary",), vmem_limit_bytes=VMEM_LIMIT),
        name="in_proj",
    )(x2, w_bf, w_tail, conv_w)


def _gdn_kernel(gq_ref, gk_ref, gv_ref, hab_ref, gz_ref, alog_ref, dtb_ref, nw_ref, o_ref,
                st_scr):
    @pl.when(pl.program_id(1) == 0)
    def _():
        st_scr[...] = jnp.zeros_like(st_scr)

    for ck in range(GDN_STEP_CHUNKS):
        r = pl.ds(ck * GDN_T, GDN_T)
        prep = _gdn_prep(gq_ref.at[r], gk_ref.at[r], gv_ref.at[r], hab_ref.at[r], alog_ref,
                         dtb_ref)
        _gdn_apply(prep, gz_ref.at[r], nw_ref, o_ref.at[r], st_scr)


def _gdn_prep(gq_ref, gk_ref, gv_ref, hab_ref, alog_ref, dtb_ref):
    t = GDN_T
    heads = range(GDN_HEADS)

    hab = hab_ref[...]
    xg = hab + dtb_ref[...]
    y_sp = jnp.exp(-jnp.abs(xg))
    u_sp = 1.0 + y_sp
    softplus = jnp.maximum(xg, 0.0) + (jnp.log(u_sp) - ((u_sp - 1.0) - y_sp) / u_sp)
    g_full = -jnp.exp(alog_ref[...]) * softplus
    beta_full = _sigmoid(hab)

    ri = lax.broadcasted_iota(jnp.int32, (t, t), 0)
    ci = lax.broadcasted_iota(jnp.int32, (t, t), 1)
    causal = ri >= ci
    strict = ri > ci
    eye = jnp.where(ri == ci, 1.0, 0.0).astype(F32)

    tri = jnp.where(causal, 1.0, 0.0).astype(BF16)
    g_hi = g_full.astype(BF16)
    g_r1 = g_full - g_hi.astype(F32)
    g_mid = g_r1.astype(BF16)
    g_lo = (g_r1 - g_mid.astype(F32)).astype(BF16)
    gc = _dot(tri, g_hi) + _dot(tri, g_mid) + _dot(tri, g_lo)
    gc_t = gc.T
    eg = jnp.exp(gc)
    g_last = gc[t - 1:t, :]
    k_dec = jnp.exp(g_last - gc)
    eg_last = jnp.exp(g_last)

    qn_b = [gq_ref[:, h * GDN_DK:(h + 1) * GDN_DK] for h in heads]
    kn_b = [gk_ref[:, h * GDN_DK:(h + 1) * GDN_DK] for h in heads]
    qn = [qn_b[h].astype(F32) for h in heads]
    kn = [kn_b[h].astype(F32) for h in heads]
    vv = [gv_ref[:, h * GDN_DV:(h + 1) * GDN_DV].astype(F32) for h in heads]

    bcol = [beta_full[:, GDN_HEADS + h:GDN_HEADS + h + 1] for h in heads]
    egc = [eg[:, h:h + 1] for h in heads]
    kb = [kn[h] * bcol[h] for h in heads]
    kq = [_dot_nt(jnp.concatenate([kb[h].astype(BF16), qn_b[h]], axis=0), kn_b[h])
          for h in heads]
    decay = []
    for h in heads:
        gd = gc[:, h:h + 1] - gc_t[h:h + 1, :]
        decay.append(jnp.where(causal, jnp.exp(jnp.where(causal, gd, 0.0)), 0.0))
    nmat = [jnp.where(strict, -(kq[h][:t] * decay[h]), 0.0) for h in heads]
    amat = [(kq[h][t:] * decay[h]).astype(BF16) for h in heads]
    rhs = [jnp.concatenate([vv[h] * bcol[h], kb[h] * egc[h]], axis=1).astype(BF16)
           for h in heads]
    qe = [qn[h] * egc[h] for h in heads]
    kd_t = [(kn[h] * k_dec[:, h:h + 1]).T.astype(BF16) for h in heads]

    def same_block(b):
        sh = int(math.log2(b))
        return (ri >> sh) == (ci >> sh)

    blk = GDN_INV_BASE
    in_blk = same_block(blk)
    n0 = [jnp.where(in_blk, nmat[h], 0.0) for h in heads]
    x = [eye + n0[h] for h in heads]
    pw = [n0[h].astype(BF16) for h in heads]
    for _ in range(int(math.log2(blk)) - 1):
        p32 = [_dot(pw[h], pw[h]) for h in heads]
        pw = [p32[h].astype(BF16) for h in heads]
        x = [x[h] + _dot(x[h].astype(BF16), pw[h]) for h in heads]
    while blk < t:
        in_big = same_block(2 * blk)
        n_off = [jnp.where(in_big, jnp.where(in_blk, 0.0, nmat[h]), 0.0).astype(BF16)
                 for h in heads]
        xb = [x[h].astype(BF16) for h in heads]
        xn = [_dot(xb[h], n_off[h]).astype(BF16) for h in heads]
        x = [x[h] + _dot(xn[h], xb[h]) for h in heads]
        in_blk = in_big
        blk *= 2
    sol = [_dot(x[h].astype(BF16), rhs[h]) for h in heads]
    return sol, qe, amat, kd_t, eg_last


def _gdn_apply(prep, gz_ref, nw_ref, o_ref, st_scr):
    t = GDN_T
    heads = range(GDN_HEADS)
    sol, qe, amat, kd_t, eg_last = prep
    s_prev = [st_scr[h] for h in heads]
    ws = [_dot(jnp.concatenate([sol[h][:, GDN_DV:], qe[h]], axis=0).astype(BF16),
               s_prev[h].astype(BF16)) for h in heads]
    v_new = [(sol[h][:, :GDN_DV] - ws[h][:t]).astype(BF16) for h in heads]
    o_l = [ws[h][t:] + _dot(amat[h], v_new[h]) for h in heads]
    for h in heads:
        st_scr[h] = s_prev[h] * eg_last[:, h:h + 1] + _dot(kd_t[h], v_new[h])
    nw = nw_ref[...]
    for h in heads:
        o = o_l[h]
        o = o * lax.rsqrt(jnp.mean(o * o, axis=-1, keepdims=True) + NORM_EPS) * nw
        gate = _silu(gz_ref[:, h * GDN_DV:(h + 1) * GDN_DV].astype(F32))
        o_ref[:, h * GDN_DV:(h + 1) * GDN_DV] = (o * gate).astype(o_ref.dtype)


def _gdn(gq, gk, gv, hab, gz, alog_row, dtb_row, norm_w, batch, seq):
    t = GDN_T * GDN_STEP_CHUNKS
    nt = seq // t
    row = lambda width: pl.BlockSpec((t, width), lambda b, s: (b * nt + s, 0))
    full = lambda a: pl.BlockSpec(a.shape, lambda b, s: (0, 0))
    return pl.pallas_call(
        _gdn_kernel,
        grid=(batch, nt),
        in_specs=[row(GDN_QK), row(GDN_QK), row(GDN_WIDTH), row(LANES), row(GDN_WIDTH),
                  full(alog_row), full(dtb_row), full(norm_w)],
        out_specs=row(GDN_WIDTH),
        out_shape=jax.ShapeDtypeStruct((batch * seq, GDN_WIDTH), BF16),
        scratch_shapes=[pltpu.VMEM((GDN_HEADS, GDN_DK, GDN_DV), F32)],
        compiler_params=pltpu.CompilerParams(
            dimension_semantics=("arbitrary", "arbitrary"), vmem_limit_bytes=VMEM_LIMIT),
        name="gated_deltanet",
    )(gq, gk, gv, hab, gz, alog_row, dtb_row, norm_w)


def _attn_kernel(q_ref, qall_ref, k_ref, v_ref, dz_ref, nw_ref, lq1_ref, lk1_ref, lq2_ref,
                 lk2_ref, o_ref, ke_scr, vt_scr, m_scr, acc_scr, l_scr, p_scr, alpha_scr,
                 bound_scr, flag_smem):
    tq, tk, cols = ATT_TQ, ATT_TK, ATT_ROWS
    n_half = tq // cols
    seq = k_ref.shape[0]
    h = pl.program_id(1)
    i = pl.program_id(2)
    slope = jnp.float32(ALIBI_SLOPES[DIFF_HEADS - 1] * LOG2E)
    n_cap = jnp.int32(_fixed_ref_steps(DIFF_HEADS - 1))
    for hh in range(DIFF_HEADS - 1):
        slope = jnp.where(h == hh, jnp.float32(ALIBI_SLOPES[hh] * LOG2E), slope)
        n_cap = jnp.where(h == hh, jnp.int32(_fixed_ref_steps(hh)), n_cap)

    lane = lax.broadcasted_iota(jnp.int32, (tk, LANES), 1)
    feat_lane = (lane - DIFF_DH, lane)

    def half_norm_max(vf):
        hl = lax.broadcasted_iota(jnp.int32, (LANES, LANES), 0)
        hc = lax.broadcasted_iota(jnp.int32, (LANES, LANES), 1)
        pick = jnp.where(hc == jnp.where(hl < DIFF_DH, 0, 1), 1.0, 0.0).astype(BF16)
        sums = jnp.max(_dot((vf * vf).astype(BF16), pick), axis=0, keepdims=True)
        return jnp.sqrt(jnp.maximum(sums[:, 0:1], sums[:, 1:2])) * (1.0 + 2.0 ** -6)

    @pl.when(i == 0)
    def _():
        bound_scr[...] = jnp.zeros_like(bound_scr)

        def build(blk, carry):
            r0 = pl.multiple_of(blk * tk, tk)
            kf = k_ref[pl.ds(r0, tk), :].astype(F32)
            bound_scr[0:1, :] = jnp.maximum(bound_scr[0:1, :], half_norm_max(kf))
            bound_scr[1:2, :] = jnp.maximum(
                bound_scr[1:2, :], half_norm_max(qall_ref[pl.ds(r0, tk), :].astype(F32)))
            pos = (lax.broadcasted_iota(jnp.int32, (tk, LANES), 0) + r0).astype(F32) * slope
            p_hi = pos.astype(BF16).astype(F32)
            r1 = pos - p_hi
            p_mid = r1.astype(BF16).astype(F32)
            p_lo = r1 - p_mid
            for mp in range(2):
                fl = feat_lane[mp]
                feat = jnp.where(fl == 0, p_hi, jnp.where(fl == 1, p_mid, jnp.where(
                    fl == 2, p_lo, jnp.where((fl >= 3) & (fl < 6), 1.0, 0.0))))
                own = (lane < DIFF_DH) if mp == 0 else (lane >= DIFF_DH)
                ke_scr[mp, pl.ds(r0, tk), :] = jnp.where(own, kf, feat).astype(BF16)
            vt = v_ref[pl.ds(r0, tk), :].astype(F32).T
            vt_scr[blk] = vt.astype(BF16)
            return carry

        lax.fori_loop(0, seq // tk, build, 0, unroll=2)
        qk_all = bound_scr[0:1, :] * bound_scr[1:2, :]
        bound_scr[2:3, :] = qk_all
        flag_smem[0] = jnp.where(jnp.max(qk_all) <= ATT_FIXED_REF_MAX, 1, 0).astype(jnp.int32)

    qf = q_ref[...].astype(F32)
    qk_bound = bound_scr[2:3, 0:1]
    chains = [(mp, r) for mp in range(2) for r in range(n_half)]
    n_chains = len(chains)

    def query_maps(ref_terms):
        out = []
        for mp in range(2):
            fl = feat_lane[mp]
            own = (lane < DIFF_DH) if mp == 0 else (lane >= DIFF_DH)
            feat = jnp.where((fl >= 0) & (fl < 3), 1.0, 0.0)
            if ref_terms is not None:
                r_hi, r_mid, r_lo = ref_terms
                feat = jnp.where(fl == 3, r_hi, jnp.where(fl == 4, r_mid,
                                                          jnp.where(fl == 5, r_lo, feat)))
            out.append(jnp.where(own, qf, feat).astype(BF16))
        return out

    def scores_t(qmaps, c, j, masked):
        mp, r = chains[c]
        nk = (r + 1) * cols if masked else tk
        qc = qmaps[mp][r * cols:(r + 1) * cols]
        kb = ke_scr[mp, pl.ds(pl.multiple_of(j * tk, tk), nk), :]
        st = _dot_nt(kb, qc)
        if masked:
            key_l = lax.broadcasted_iota(jnp.int32, (nk, cols), 0)
            qry_l = lax.broadcasted_iota(jnp.int32, (nk, cols), 1)
            st = jnp.where(key_l <= qry_l + r * cols, st, MASK_NEG)
        return st

    def key_sums(p32):
        nk = p32.shape[0]
        return jnp.sum(p32.reshape(nk // 8, 8, cols), axis=0)

    def store_p(slot, c, p32):
        p = p32.astype(BF16)
        nk = p.shape[0]
        p_scr[slot, c, 0:nk, :] = p
        if nk < tk:
            p_scr[slot, c, nk:, :] = jnp.zeros((tk - nk, cols), BF16)

    def finish(acc_fin):
        def map_out(mp):
            parts = []
            for r in range(n_half):
                c = mp * n_half + r
                denom = jnp.sum(l_scr[c], axis=0, keepdims=True)
                parts.append(acc_fin[c] * (1.0 / denom))
            return jnp.concatenate(parts, axis=1)

        lam = (jnp.exp(jnp.sum(lq1_ref[...] * lk1_ref[...], axis=-1, keepdims=True))
               - jnp.exp(jnp.sum(lq2_ref[...] * lk2_ref[...], axis=-1, keepdims=True))
               + LAM_INIT)
        o_t = map_out(0) - lam * map_out(1)
        o_t = o_t * lax.rsqrt(jnp.mean(o_t * o_t, axis=0, keepdims=True) + SUBLN_EPS)
        o = o_t.T * (nw_ref[...] * (1.0 - LAM_INIT)) * _silu(dz_ref[...].astype(F32))
        o_ref[...] = o.astype(o_ref.dtype)

    def pipeline(scores_step, pv_value, steps_after_diag):
        def scores_block(j, slot, masked=False):
            for c in range(n_chains):
                scores_step(c, j, slot, masked)

        def pv_block(j, slot):
            for c in range(n_chains):
                acc_scr[c] = pv_value(c, j, slot)

        acc_scr[...] = jnp.zeros_like(acc_scr)
        l_scr[...] = jnp.zeros_like(l_scr)
        scores_block(i, 0, masked=True)
        n = steps_after_diag()
        last = i - n

        def pair(u, carry):
            j = i - 2 * u - 1
            scores_block(j, 1)
            pv_block(j + 1, 0)
            scores_block(j - 1, 0)
            pv_block(j, 1)
            return carry

        lax.fori_loop(0, n // 2, pair, 0)

        @pl.when(n % 2 == 1)
        def _():
            scores_block(last, 1)
            pv_block(last + 1, 0)
            finish([pv_value(c, last, 1) for c in range(n_chains)])

        @pl.when(n % 2 == 0)
        def _():
            finish([pv_value(c, last, 0) for c in range(n_chains)])

    use_fixed_ref = flag_smem[0] == 1

    @pl.when(use_fixed_ref)
    def _():
        pos_q = (lax.broadcasted_iota(jnp.int32, (tq, LANES), 0) + i * tq).astype(F32) * slope
        neg_ref = -(qk_bound + ATT_SKIP_SLACK + pos_q)
        r_hi = neg_ref.astype(BF16).astype(F32)
        r1 = neg_ref - r_hi
        r_mid = r1.astype(BF16).astype(F32)
        qmaps = query_maps((r_hi, r_mid, r1 - r_mid))

        def scores_step(c, j, slot, masked):
            p32 = jnp.exp2(scores_t(qmaps, c, j, masked))
            l_scr[c] = l_scr[c] + key_sums(p32)
            store_p(slot, c, p32)

        def pv_value(c, j, slot):
            return acc_scr[c] + _dot(vt_scr[j], p_scr[slot, c])

        pipeline(scores_step, pv_value, lambda: jnp.minimum(i, n_cap))

    @pl.when(jnp.logical_not(use_fixed_ref))
    def _():
        qmaps = query_maps(None)
        m_scr[...] = jnp.full(m_scr.shape, MASK_NEG, F32)

        def scores_step(c, j, slot, masked):
            st = scores_t(qmaps, c, j, masked)
            m_prev = m_scr[c]
            m_new = jnp.maximum(m_prev, jnp.max(st, axis=0, keepdims=True))
            p32 = jnp.exp2(st - m_new)
            alpha = jnp.exp2(m_prev - m_new)
            l_scr[c] = l_scr[c] * alpha + key_sums(p32)
            store_p(slot, c, p32)
            alpha_scr[slot, c] = alpha
            m_scr[c] = m_new

        def pv_value(c, j, slot):
            pv = _dot(vt_scr[j], p_scr[slot, c])
            return acc_scr[c] * alpha_scr[slot, c] + pv

        def steps_after_diag():
            m_min = jnp.min(jnp.min(m_scr[...], axis=0), axis=-1, keepdims=True)
            need = m_min - (ATT_SKIP_EXP2 + ATT_SKIP_SLACK) - qk_bound
            j_first = jnp.ceil(need / (slope * tk) - (tk - 1) / tk)
            n_arr = jnp.clip(i.astype(F32) - jnp.maximum(j_first, 0.0), 0.0, i.astype(F32))
            return jnp.max(n_arr).astype(jnp.int32)

        pipeline(scores_step, pv_value, steps_after_diag)


def _diff_attention(dq, dk, dv, dz, norm_w, lq1, lk1, lq2, lk2, batch, seq):
    tq, tk = ATT_TQ, ATT_TK
    nq = seq // tq
    n_chains = 2 * (tq // ATT_ROWS)
    qspec = pl.BlockSpec((tq, DIFF_DV), lambda b, h, i: (b * nq + i, h))
    kvspec = pl.BlockSpec((seq, DIFF_DV), lambda b, h, i: (b, h))
    full = lambda a: pl.BlockSpec(a.shape, lambda b, h, i: (0, 0))
    return pl.pallas_call(
        _attn_kernel,
        grid=(batch, DIFF_HEADS, nq),
        in_specs=[qspec, kvspec, kvspec, kvspec, qspec, full(norm_w), full(lq1), full(lk1),
                  full(lq2), full(lk2)],
        out_specs=qspec,
        out_shape=jax.ShapeDtypeStruct((batch * seq, DIFF_WIDTH), BF16),
        scratch_shapes=[pltpu.VMEM((2, seq, DIFF_DV), BF16),
                        pltpu.VMEM((seq // tk, DIFF_DV, tk), BF16),
                        pltpu.VMEM((n_chains, 1, ATT_ROWS), F32),
                        pltpu.VMEM((n_chains, DIFF_DV, ATT_ROWS), F32),
                        pltpu.VMEM((n_chains, 8, ATT_ROWS), F32),
                        pltpu.VMEM((2, n_chains, tk, ATT_ROWS), BF16),
                        pltpu.VMEM((2, n_chains, 1, ATT_ROWS), F32),
                        pltpu.VMEM((8, LANES), F32),
                        pltpu.SMEM((1,), jnp.int32)],
        compiler_params=pltpu.CompilerParams(
            dimension_semantics=("arbitrary", "arbitrary", "arbitrary"),
            vmem_limit_bytes=VMEM_LIMIT),
        name="diff_attention",
    )(dq, dq, dk, dv, dz, norm_w, lq1, lk1, lq2, lk2)


def _out_kernel(x_ref, oa_ref, ob_ref, mab_ref, wa_ref, wb_ref, wo_ref, g_ref, b_ref, y_ref,
                merged_scr):
    for c0 in range(0, D_MODEL, OUT_CHUNK):
        cs = slice(c0, c0 + OUT_CHUNK)
        ya = _dot(oa_ref[...], wa_ref[:, cs])
        yb = _dot(ob_ref[...], wb_ref[:, cs])
        ga = _sigmoid(mab_ref[:, c0:c0 + OUT_CHUNK].astype(F32))
        gb = _sigmoid(mab_ref[:, D_MODEL + c0:D_MODEL + c0 + OUT_CHUNK].astype(F32))
        merged_scr[:, cs] = (ga * ya + gb * yb).astype(BF16)
    r0 = 0
    for nrows in OUT_ROW_GROUPS:
        rs = slice(r0, r0 + nrows)
        r0 += nrows
        y = _dot(merged_scr[rs, :], wo_ref[...])
        z = DEEPNORM_ALPHA * x_ref[rs, :] + y
        mu = jnp.mean(z, axis=-1, keepdims=True)
        zc = z - mu
        var = jnp.mean(zc * zc, axis=-1, keepdims=True)
        y_ref[rs, :] = zc * lax.rsqrt(var + LN_EPS) * g_ref[...] + b_ref[...]


def _output(x2, o_a, o_b, mab, wa, wb, wo, ln_g, ln_b):
    m = x2.shape[0]
    tm = OUT_TM
    row = lambda width: pl.BlockSpec((tm, width), lambda i: (i, 0))
    full = lambda a: pl.BlockSpec(a.shape, lambda i: (0, 0))
    return pl.pallas_call(
        _out_kernel,
        grid=(m // tm,),
        in_specs=[row(D_MODEL), row(GDN_WIDTH), row(DIFF_WIDTH), row(2 * D_MODEL),
                  full(wa), full(wb), full(wo), full(ln_g), full(ln_b)],
        out_specs=row(D_MODEL),
        out_shape=jax.ShapeDtypeStruct((m, D_MODEL), F32),
        scratch_shapes=[pltpu.VMEM((tm, D_MODEL), BF16)],
        compiler_params=pltpu.CompilerParams(
            dimension_semantics=("arbitrary",), vmem_limit_bytes=VMEM_LIMIT),
        name="merge_out_ln",
    )(x2, o_a, o_b, mab, wa, wb, wo, ln_g, ln_b)


def _lane_row(v):
    return jnp.zeros((1, LANES), F32).at[0, :v.shape[0]].set(v.astype(F32))


def kernel(x, w_in, conv_w, a_log, dt_bias, gdn_norm_w, w_up_a, lambda_q1, lambda_k1,
           lambda_q2, lambda_k2, diff_norm_w, w_up_b, w_out, ln_g, ln_b):
    batch, seq, d = x.shape
    x2 = x.reshape(batch * seq, d)
    layer = 0
    w = w_in[layer].astype(BF16)
    w_tail = w[:, PROJ_HEAD + 2 * GDN_HEADS:]

    gq, gk, gv, hab, gz, dq, dk, dv, dz, mab = _project(
        x2, w, w_tail, conv_w[layer].astype(F32), seq)

    o_a = _gdn(gq, gk, gv, hab, gz, _lane_row(a_log[layer]), _lane_row(dt_bias[layer]),
               gdn_norm_w[layer].reshape(1, GDN_DV).astype(F32), batch, seq)

    o_b = _diff_attention(
        dq, dk, dv, dz, diff_norm_w[layer].reshape(1, DIFF_DV).astype(F32),
        lambda_q1[layer].reshape(1, DIFF_DH).astype(F32),
        lambda_k1[layer].reshape(1, DIFF_DH).astype(F32),
        lambda_q2[layer].reshape(1, DIFF_DH).astype(F32),
        lambda_k2[layer].reshape(1, DIFF_DH).astype(F32), batch, seq)

    y = _output(x2, o_a, o_b, mab, w_up_a[layer].astype(BF16), w_up_b[layer].astype(BF16),
                w_out[layer].astype(BF16), ln_g[layer].reshape(1, d).astype(F32),
                ln_b[layer].reshape(1, d).astype(F32))
    return y.reshape(batch, seq, d)
```

```python
import functools
import math

import jax
import jax.numpy as jnp
from jax import lax
from jax.experimental import pallas as pl
from jax.experimental.pallas import tpu as pltpu

F32 = jnp.float32
BF16 = jnp.bfloat16

D_MODEL = 1024
GDN_HEADS = 4
GDN_DK = 128
GDN_DV = 128
GDN_QK = GDN_HEADS * GDN_DK
GDN_WIDTH = GDN_HEADS * GDN_DV
CONV_K = 4
DIFF_HEADS = 4
DIFF_DH = 64
DIFF_DV = 2 * DIFF_DH
DIFF_QK = DIFF_HEADS * 2 * DIFF_DH
DIFF_WIDTH = DIFF_HEADS * DIFF_DV
NORM_EPS = 1e-6
SUBLN_EPS = 1e-5
LN_EPS = 1e-5
DEPTH = 1
DEEPNORM_ALPHA = (2.0 * DEPTH) ** 0.25
LAM_INIT = 0.8 - 0.6 * math.exp(-0.3 * 0)
ALIBI_SLOPES = tuple(2.0 ** (-8.0 * (i + 1) / DIFF_HEADS) for i in range(DIFF_HEADS))
LOG2E = math.log2(math.e)

LANES = 128
CONV_HIST = 8
VMEM_LIMIT = 56 * 1024 * 1024

PROJ_TM = 256
PROJ_HEAD = 2 * GDN_QK + GDN_WIDTH
GDN_T = 256
GDN_STEP_CHUNKS = 2
GDN_INV_BASE = 16
ATT_TQ = 512
ATT_TK = 512
ATT_ROWS = 256
OUT_TM = 512
OUT_CHUNK = 256
OUT_ROW_GROUPS = (256, 256)
assert sum(OUT_ROW_GROUPS) == OUT_TM
MASK_NEG = -1e30
ATT_SKIP_EXP2 = 150.0
ATT_SKIP_SLACK = 1.0
ATT_FIXED_REF_MAX = 48.0


def _fixed_ref_steps(head):
    slope = ALIBI_SLOPES[head] * LOG2E
    return max(math.ceil(ATT_SKIP_EXP2 / (slope * ATT_TK) + (ATT_TK - 1) / ATT_TK) - 1, 0)


def _sigmoid(x):
    return 1.0 / (1.0 + jnp.exp(-x))


def _silu(x):
    return x * _sigmoid(x)


def _dot(a, b):
    return jnp.dot(a, b, preferred_element_type=F32)


def _dot_nt(a, b):
    return lax.dot_general(a, b, (((1,), (1,)), ((), ())), preferred_element_type=F32)


def _proj_kernel(tiles_per_seq, x_ref, wh_ref, wt_ref, cw_ref, gq_ref, gk_ref, gv_ref,
                 hab_ref, gz_ref, dq_ref, dk_ref, dv_ref, dz_ref, mab_ref, hist_scr, xb_scr):
    tm = PROJ_TM

    @pl.when(pl.program_id(0) % tiles_per_seq == 0)
    def _():
        hist_scr[...] = jnp.zeros_like(hist_scr)

    xb_scr[...] = x_ref[...].astype(BF16)

    def mm(c0, width):
        if c0 < PROJ_HEAD:
            return _dot(xb_scr[...], wh_ref[:, c0:c0 + width])
        return _dot(xb_scr[...], wt_ref[:, c0 - PROJ_HEAD:c0 - PROJ_HEAD + width])

    piece = 2 * GDN_DK

    def conv_silu(c0):
        cols = slice(c0, c0 + piece)
        acc = mm(c0, piece)
        ext = jnp.concatenate([hist_scr[:, cols], acc], axis=0)
        hist_scr[:, cols] = acc[tm - CONV_HIST:]
        y = None
        for j in range(CONV_K):
            r0 = CONV_HIST - (CONV_K - 1) + j
            term = cw_ref[j:j + 1, cols] * ext[r0:r0 + tm]
            y = term if y is None else y + term
        return _silu(y)

    def l2norm_heads(y, scale):
        parts = []
        for h in range(piece // GDN_DK):
            v = y[:, h * GDN_DK:(h + 1) * GDN_DK]
            inv = lax.rsqrt(jnp.sum(v * v, axis=-1, keepdims=True) + NORM_EPS)
            parts.append(v * (inv * scale) if scale != 1.0 else v * inv)
        return jnp.concatenate(parts, axis=1)

    def plain(out_ref, c0, scale=None):
        for j in range(out_ref.shape[1] // 512):
            acc = mm(c0 + j * 512, 512)
            if scale is not None:
                acc = acc * scale
            out_ref[:, j * 512:(j + 1) * 512] = acc.astype(out_ref.dtype)

    for half in range(2):
        c0 = half * piece
        gq_ref[:, c0:c0 + piece] = l2norm_heads(conv_silu(c0), GDN_DK ** -0.5).astype(BF16)
        if half == 0:
            plain(gz_ref, 1536)
        else:
            plain(dq_ref, 2048, DIFF_DH ** -0.5 * LOG2E)
    for half in range(2):
        c0 = half * piece
        gk_ref[:, c0:c0 + piece] = l2norm_heads(conv_silu(GDN_QK + c0), 1.0).astype(BF16)
        plain(dk_ref if half == 0 else dv_ref, 2560 + half * 512)
    for half in range(2):
        c0 = half * piece
        gv_ref[:, c0:c0 + piece] = conv_silu(2 * GDN_QK + c0).astype(BF16)
        if half == 0:
            plain(dz_ref, 3584)
    hab_ref[...] = _dot(xb_scr[...], wh_ref[:, PROJ_HEAD:PROJ_HEAD + LANES])
    plain(mab_ref, 4096)


def _project(x2, w_bf, w_tail, conv_w, seq):
    m = x2.shape[0]
    tm = PROJ_TM
    row = lambda width: pl.BlockSpec((tm, width), lambda i: (i, 0))
    full = lambda a: pl.BlockSpec(a.shape, lambda i: (0, 0))
    out_shape = (
        jax.ShapeDtypeStruct((m, GDN_QK), BF16),
        jax.ShapeDtypeStruct((m, GDN_QK), BF16),
        jax.ShapeDtypeStruct((m, GDN_WIDTH), BF16),
        jax.ShapeDtypeStruct((m, LANES), F32),
        jax.ShapeDtypeStruct((m, 512), BF16),
        jax.ShapeDtypeStruct((m, 512), BF16),
        jax.ShapeDtypeStruct((m, 512), BF16),
        jax.ShapeDtypeStruct((m, 512), BF16),
        jax.ShapeDtypeStruct((m, 512), BF16),
        jax.ShapeDtypeStruct((m, 2048), BF16),
    )
    return pl.pallas_call(
        functools.partial(_proj_kernel, seq // tm),
        grid=(m // tm,),
        in_specs=[row(D_MODEL),
                  pl.BlockSpec((D_MODEL, PROJ_HEAD + LANES), lambda i: (0, 0)),
                  full(w_tail), full(conv_w)],
        out_specs=tuple(row(s.shape[1]) for s in out_shape),
        out_shape=out_shape,
        scratch_shapes=[pltpu.VMEM((CONV_HIST, 3 * GDN_QK), F32),
                        pltpu.VMEM((tm, D_MODEL), BF16)],
        compiler_params=pltpu.CompilerParams(
            dimension_semantics=("arbitrary",), vmem_limit_bytes=VMEM_LIMIT),
        name="in_proj",
    )(x2, w_bf, w_tail, conv_w)


def _gdn_kernel(gq_ref, gk_ref, gv_ref, hab_ref, gz_ref, alog_ref, dtb_ref, nw_ref, o_ref,
                st_scr):
    @pl.when(pl.program_id(1) == 0)
    def _():
        st_scr[...] = jnp.zeros_like(st_scr)

    for ck in range(GDN_STEP_CHUNKS):
        r = pl.ds(ck * GDN_T, GDN_T)
        prep = _gdn_prep(gq_ref.at[r], gk_ref.at[r], gv_ref.at[r], hab_ref.at[r], alog_ref,
                         dtb_ref)
        _gdn_apply(prep, gz_ref.at[r], nw_ref, o_ref.at[r], st_scr)


def _gdn_prep(gq_ref, gk_ref, gv_ref, hab_ref, alog_ref, dtb_ref):
    t = GDN_T
    heads = range(GDN_HEADS)

    hab = hab_ref[...]
    xg = hab + dtb_ref[...]
    y_sp = jnp.exp(-jnp.abs(xg))
    u_sp = 1.0 + y_sp
    softplus = jnp.maximum(xg, 0.0) + (jnp.log(u_sp) - ((u_sp - 1.0) - y_sp) / u_sp)
    g_full = -jnp.exp(alog_ref[...]) * softplus
    beta_full = _sigmoid(hab)

    ri = lax.broadcasted_iota(jnp.int32, (t, t), 0)
    ci = lax.broadcasted_iota(jnp.int32, (t, t), 1)
    causal = ri >= ci
    strict = ri > ci
    eye = jnp.where(ri == ci, 1.0, 0.0).astype(F32)

    tri = jnp.where(causal, 1.0, 0.0).astype(BF16)
    g_hi = g_full.astype(BF16)
    g_r1 = g_full - g_hi.astype(F32)
    g_mid = g_r1.astype(BF16)
    g_lo = (g_r1 - g_mid.astype(F32)).astype(BF16)
    gc = _dot(tri, g_hi) + _dot(tri, g_mid) + _dot(tri, g_lo)
    gc_t = gc.T
    eg = jnp.exp(gc)
    g_last = gc[t - 1:t, :]
    k_dec = jnp.exp(g_last - gc)
    eg_last = jnp.exp(g_last)

    qn_b = [gq_ref[:, h * GDN_DK:(h + 1) * GDN_DK] for h in heads]
    kn_b = [gk_ref[:, h * GDN_DK:(h + 1) * GDN_DK] for h in heads]
    qn = [qn_b[h].astype(F32) for h in heads]
    kn = [kn_b[h].astype(F32) for h in heads]
    vv = [gv_ref[:, h * GDN_DV:(h + 1) * GDN_DV].astype(F32) for h in heads]

    bcol = [beta_full[:, GDN_HEADS + h:GDN_HEADS + h + 1] for h in heads]
    egc = [eg[:, h:h + 1] for h in heads]
    kb = [kn[h] * bcol[h] for h in heads]
    kq = [_dot_nt(jnp.concatenate([kb[h].astype(BF16), qn_b[h]], axis=0), kn_b[h])
          for h in heads]
    decay = []
    for h in heads:
        gd = gc[:, h:h + 1] - gc_t[h:h + 1, :]
        decay.append(jnp.where(causal, jnp.exp(jnp.where(causal, gd, 0.0)), 0.0))
    nmat = [jnp.where(strict, -(kq[h][:t] * decay[h]), 0.0) for h in heads]
    amat = [(kq[h][t:] * decay[h]).astype(BF16) for h in heads]
    rhs = [jnp.concatenate([vv[h] * bcol[h], kb[h] * egc[h]], axis=1).astype(BF16)
           for h in heads]
    qe = [qn[h] * egc[h] for h in heads]
    kd_t = [(kn[h] * k_dec[:, h:h + 1]).T.astype(BF16) for h in heads]

    def same_block(b):
        sh = int(math.log2(b))
        return (ri >> sh) == (ci >> sh)

    blk = GDN_INV_BASE
    in_blk = same_block(blk)
    n0 = [jnp.where(in_blk, nmat[h], 0.0) for h in heads]
    x = [eye + n0[h] for h in heads]
    pw = [n0[h].astype(BF16) for h in heads]
    for _ in range(int(math.log2(blk)) - 1):
        p32 = [_dot(pw[h], pw[h]) for h in heads]
        pw = [p32[h].astype(BF16) for h in heads]
        x = [x[h] + _dot(x[h].astype(BF16), pw[h]) for h in heads]
    while blk < t:
        in_big = same_block(2 * blk)
        n_off = [jnp.where(in_big, jnp.where(in_blk, 0.0, nmat[h]), 0.0).astype(BF16)
                 for h in heads]
        xb = [x[h].astype(BF16) for h in heads]
        xn = [_dot(xb[h], n_off[h]).astype(BF16) for h in heads]
        x = [x[h] + _dot(xn[h], xb[h]) for h in heads]
        in_blk = in_big
        blk *= 2
    sol = [_dot(x[h].astype(BF16), rhs[h]) for h in heads]
    return sol, qe, amat, kd_t, eg_last


def _gdn_apply(prep, gz_ref, nw_ref, o_ref, st_scr):
    t = GDN_T
    heads = range(GDN_HEADS)
    sol, qe, amat, kd_t, eg_last = prep
    s_prev = [st_scr[h] for h in heads]
    ws = [_dot(jnp.concatenate([sol[h][:, GDN_DV:], qe[h]], axis=0).astype(BF16),
               s_prev[h].astype(BF16)) for h in heads]
    v_new = [(sol[h][:, :GDN_DV] - ws[h][:t]).astype(BF16) for h in heads]
    o_l = [ws[h][t:] + _dot(amat[h], v_new[h]) for h in heads]
    for h in heads:
        st_scr[h] = s_prev[h] * eg_last[:, h:h + 1] + _dot(kd_t[h], v_new[h])
    nw = nw_ref[...]
    for h in heads:
        o = o_l[h]
        o = o * lax.rsqrt(jnp.mean(o * o, axis=-1, keepdims=True) + NORM_EPS) * nw
        gate = _silu(gz_ref[:, h * GDN_DV:(h + 1) * GDN_DV].astype(F32))
        o_ref[:, h * GDN_DV:(h + 1) * GDN_DV] = (o * gate).astype(o_ref.dtype)


def _gdn(gq, gk, gv, hab, gz, alog_row, dtb_row, norm_w, batch, seq):
    t = GDN_T * GDN_STEP_CHUNKS
    nt = seq // t
    row = lambda width: pl.BlockSpec((t, width), lambda b, s: (b * nt + s, 0))
    full = lambda a: pl.BlockSpec(a.shape, lambda b, s: (0, 0))
    return pl.pallas_call(
        _gdn_kernel,
        grid=(batch, nt),
        in_specs=[row(GDN_QK), row(GDN_QK), row(GDN_WIDTH), row(LANES), row(GDN_WIDTH),
                  full(alog_row), full(dtb_row), full(norm_w)],
        out_specs=row(GDN_WIDTH),
        out_shape=jax.ShapeDtypeStruct((batch * seq, GDN_WIDTH), BF16),
        scratch_shapes=[pltpu.VMEM((GDN_HEADS, GDN_DK, GDN_DV), F32)],
        compiler_params=pltpu.CompilerParams(
            dimension_semantics=("arbitrary", "arbitrary"), vmem_limit_bytes=VMEM_LIMIT),
        name="gated_deltanet",
    )(gq, gk, gv, hab, gz, alog_row, dtb_row, norm_w)


def _attn_kernel(q_ref, qall_ref, k_ref, v_ref, dz_ref, nw_ref, lq1_ref, lk1_ref, lq2_ref,
                 lk2_ref, o_ref, ke_scr, vt_scr, m_scr, acc_scr, l_scr, pend_scr, pendl_scr,
                 p_scr, alpha_scr, bound_scr, flag_smem):
    tq, tk, cols = ATT_TQ, ATT_TK, ATT_ROWS
    n_half = tq // cols
    seq = k_ref.shape[0]
    nq_steps = seq // tq
    h = pl.program_id(1)
    i = pl.program_id(2)
    slope = jnp.float32(ALIBI_SLOPES[DIFF_HEADS - 1] * LOG2E)
    n_cap = jnp.int32(_fixed_ref_steps(DIFF_HEADS - 1))
    for hh in range(DIFF_HEADS - 1):
        slope = jnp.where(h == hh, jnp.float32(ALIBI_SLOPES[hh] * LOG2E), slope)
        n_cap = jnp.where(h == hh, jnp.int32(_fixed_ref_steps(hh)), n_cap)

    lane = lax.broadcasted_iota(jnp.int32, (tk, LANES), 1)
    feat_lane = (lane - DIFF_DH, lane)

    def half_norm_max(vf):
        hl = lax.broadcasted_iota(jnp.int32, (LANES, LANES), 0)
        hc = lax.broadcasted_iota(jnp.int32, (LANES, LANES), 1)
        pick = jnp.where(hc == jnp.where(hl < DIFF_DH, 0, 1), 1.0, 0.0).astype(BF16)
        sums = jnp.max(_dot((vf * vf).astype(BF16), pick), axis=0, keepdims=True)
        return jnp.sqrt(jnp.maximum(sums[:, 0:1], sums[:, 1:2])) * (1.0 + 2.0 ** -6)

    @pl.when(i == 0)
    def _():
        bound_scr[...] = jnp.zeros_like(bound_scr)
        pend_scr[...] = jnp.ones_like(pend_scr)
        pendl_scr[...] = jnp.ones_like(pendl_scr)

        def build(blk, carry):
            r0 = pl.multiple_of(blk * tk, tk)
            kf = k_ref[pl.ds(r0, tk), :].astype(F32)
            bound_scr[0:1, :] = jnp.maximum(bound_scr[0:1, :], half_norm_max(kf))
            bound_scr[1:2, :] = jnp.maximum(
                bound_scr[1:2, :], half_norm_max(qall_ref[pl.ds(r0, tk), :].astype(F32)))
            pos = (lax.broadcasted_iota(jnp.int32, (tk, LANES), 0) + r0).astype(F32) * slope
            p_hi = pos.astype(BF16).astype(F32)
            r1 = pos - p_hi
            p_mid = r1.astype(BF16).astype(F32)
            p_lo = r1 - p_mid
            for mp in range(2):
                fl = feat_lane[mp]
                feat = jnp.where(fl == 0, p_hi, jnp.where(fl == 1, p_mid, jnp.where(
                    fl == 2, p_lo, jnp.where((fl >= 3) & (fl < 6), 1.0, 0.0))))
                own = (lane < DIFF_DH) if mp == 0 else (lane >= DIFF_DH)
                ke_scr[mp, pl.ds(r0, tk), :] = jnp.where(own, kf, feat).astype(BF16)
            vt = v_ref[pl.ds(r0, tk), :].astype(F32).T
            vt_scr[blk] = vt.astype(BF16)
            return carry

        lax.fori_loop(0, seq // tk, build, 0, unroll=2)
        qk_all = bound_scr[0:1, :] * bound_scr[1:2, :]
        bound_scr[2:3, :] = qk_all
        flag_smem[0] = jnp.where(jnp.max(qk_all) <= ATT_FIXED_REF_MAX, 1, 0).astype(jnp.int32)

    qf = q_ref[...].astype(F32)
    qk_bound = bound_scr[2:3, 0:1]
    chains = [(mp, r) for mp in range(2) for r in range(n_half)]
    n_chains = len(chains)

    def query_maps(ref_terms):
        out = []
        for mp in range(2):
            fl = feat_lane[mp]
            own = (lane < DIFF_DH) if mp == 0 else (lane >= DIFF_DH)
            feat = jnp.where((fl >= 0) & (fl < 3), 1.0, 0.0)
            if ref_terms is not None:
                r_hi, r_mid, r_lo = ref_terms
                feat = jnp.where(fl == 3, r_hi, jnp.where(fl == 4, r_mid,
                                                          jnp.where(fl == 5, r_lo, feat)))
            out.append(jnp.where(own, qf, feat).astype(BF16))
        return out

    def scores_t(qmaps, c, j, masked):
        mp, r = chains[c]
        nk = (r + 1) * cols if masked else tk
        qc = qmaps[mp][r * cols:(r + 1) * cols]
        kb = ke_scr[mp, pl.ds(pl.multiple_of(j * tk, tk), nk), :]
        st = _dot_nt(kb, qc)
        if masked:
            key_l = lax.broadcasted_iota(jnp.int32, (nk, cols), 0)
            qry_l = lax.broadcasted_iota(jnp.int32, (nk, cols), 1)
            st = jnp.where(key_l <= qry_l + r * cols, st, MASK_NEG)
        return st

    def key_sums(p32):
        nk = p32.shape[0]
        return jnp.sum(p32.reshape(nk // 8, 8, cols), axis=0)

    def store_p(slot, c, p32):
        p = p32.astype(BF16)
        nk = p.shape[0]
        p_scr[slot, c, 0:nk, :] = p
        if nk < tk:
            p_scr[slot, c, nk:, :] = jnp.zeros((tk - nk, cols), BF16)

    def park(acc_fin):
        for c in range(n_chains):
            pend_scr[c] = acc_fin[c]
            pendl_scr[c] = l_scr[c]

    def finish_pending():
        def map_out(mp):
            parts = []
            for r in range(n_half):
                c = mp * n_half + r
                denom = jnp.sum(pendl_scr[c], axis=0, keepdims=True)
                parts.append(pend_scr[c] * (1.0 / denom))
            return jnp.concatenate(parts, axis=1)

        lam = (jnp.exp(jnp.sum(lq1_ref[...] * lk1_ref[...], axis=-1, keepdims=True))
               - jnp.exp(jnp.sum(lq2_ref[...] * lk2_ref[...], axis=-1, keepdims=True))
               + LAM_INIT)
        o_t = map_out(0) - lam * map_out(1)
        o_t = o_t * lax.rsqrt(jnp.mean(o_t * o_t, axis=0, keepdims=True) + SUBLN_EPS)
        o = o_t.T * (nw_ref[...] * (1.0 - LAM_INIT)) * _silu(dz_ref[...].astype(F32))
        o_ref[...] = o.astype(o_ref.dtype)

    def pipeline(scores_step, pv_value, steps_after_diag):
        def scores_block(j, slot, masked=False):
            for c in range(n_chains):
                scores_step(c, j, slot, masked)

        def pv_block(j, slot):
            for c in range(n_chains):
                acc_scr[c] = pv_value(c, j, slot)

        acc_scr[...] = jnp.zeros_like(acc_scr)
        l_scr[...] = jnp.zeros_like(l_scr)
        scores_block(i, 0, masked=True)
        finish_pending()
        n = steps_after_diag()
        last = i - n

        def pair(u, carry):
            j = i - 2 * u - 1
            scores_block(j, 1)
            pv_block(j + 1, 0)
            scores_block(j - 1, 0)
            pv_block(j, 1)
            return carry

        lax.fori_loop(0, n // 2, pair, 0)

        @pl.when(n % 2 == 1)
        def _():
            scores_block(last, 1)
            pv_block(last + 1, 0)
            park([pv_value(c, last, 1) for c in range(n_chains)])

        @pl.when(n % 2 == 0)
        def _():
            park([pv_value(c, last, 0) for c in range(n_chains)])

    use_fixed_ref = flag_smem[0] == 1
    live = i < nq_steps

    @pl.when(jnp.logical_not(live))
    def _():
        finish_pending()

    @pl.when(use_fixed_ref & live)
    def _():
        pos_q = (lax.broadcasted_iota(jnp.int32, (tq, LANES), 0) + i * tq).astype(F32) * slope
        neg_ref = -(qk_bound + ATT_SKIP_SLACK + pos_q)
        r_hi = neg_ref.astype(BF16).astype(F32)
        r1 = neg_ref - r_hi
        r_mid = r1.astype(BF16).astype(F32)
        qmaps = query_maps((r_hi, r_mid, r1 - r_mid))

        def scores_step(c, j, slot, masked):
            p32 = jnp.exp2(scores_t(qmaps, c, j, masked))
            l_scr[c] = l_scr[c] + key_sums(p32)
            store_p(slot, c, p32)

        def pv_value(c, j, slot):
            return acc_scr[c] + _dot(vt_scr[j], p_scr[slot, c])

        pipeline(scores_step, pv_value, lambda: jnp.minimum(i, n_cap))

    @pl.when(jnp.logical_not(use_fixed_ref) & live)
    def _():
        qmaps = query_maps(None)
        m_scr[...] = jnp.full(m_scr.shape, MASK_NEG, F32)

        def scores_step(c, j, slot, masked):
            st = scores_t(qmaps, c, j, masked)
            m_prev = m_scr[c]
            m_new = jnp.maximum(m_prev, jnp.max(st, axis=0, keepdims=True))
            p32 = jnp.exp2(st - m_new)
            alpha = jnp.exp2(m_prev - m_new)
            l_scr[c] = l_scr[c] * alpha + key_sums(p32)
            store_p(slot, c, p32)
            alpha_scr[slot, c] = alpha
            m_scr[c] = m_new

        def pv_value(c, j, slot):
            pv = _dot(vt_scr[j], p_scr[slot, c])
            return acc_scr[c] * alpha_scr[slot, c] + pv

        def steps_after_diag():
            m_min = jnp.min(jnp.min(m_scr[...], axis=0), axis=-1, keepdims=True)
            need = m_min - (ATT_SKIP_EXP2 + ATT_SKIP_SLACK) - qk_bound
            j_first = jnp.ceil(need / (slope * tk) - (tk - 1) / tk)
            n_arr = jnp.clip(i.astype(F32) - jnp.maximum(j_first, 0.0), 0.0, i.astype(F32))
            return jnp.max(n_arr).astype(jnp.int32)

        pipeline(scores_step, pv_value, steps_after_diag)


def _diff_attention(dq, dk, dv, dz, norm_w, lq1, lk1, lq2, lk2, batch, seq):
    tq, tk = ATT_TQ, ATT_TK
    nq = seq // tq
    n_chains = 2 * (tq // ATT_ROWS)
    qspec = pl.BlockSpec((tq, DIFF_DV), lambda b, h, i: (b * nq + jnp.minimum(i, nq - 1), h))
    lagspec = pl.BlockSpec((tq, DIFF_DV), lambda b, h, i: (b * nq + jnp.maximum(i - 1, 0), h))
    kvspec = pl.BlockSpec((seq, DIFF_DV), lambda b, h, i: (b, h))
    full = lambda a: pl.BlockSpec(a.shape, lambda b, h, i: (0, 0))
    return pl.pallas_call(
        _attn_kernel,
        grid=(batch, DIFF_HEADS, nq + 1),
        in_specs=[qspec, kvspec, kvspec, kvspec, lagspec, full(norm_w), full(lq1), full(lk1),
                  full(lq2), full(lk2)],
        out_specs=lagspec,
        out_shape=jax.ShapeDtypeStruct((batch * seq, DIFF_WIDTH), BF16),
        scratch_shapes=[pltpu.VMEM((2, seq, DIFF_DV), BF16),
                        pltpu.VMEM((seq // tk, DIFF_DV, tk), BF16),
                        pltpu.VMEM((n_chains, 1, ATT_ROWS), F32),
                        pltpu.VMEM((n_chains, DIFF_DV, ATT_ROWS), F32),
                        pltpu.VMEM((n_chains, 8, ATT_ROWS), F32),
                        pltpu.VMEM((n_chains, DIFF_DV, ATT_ROWS), F32),
                        pltpu.VMEM((n_chains, 8, ATT_ROWS), F32),
                        pltpu.VMEM((2, n_chains, tk, ATT_ROWS), BF16),
                        pltpu.VMEM((2, n_chains, 1, ATT_ROWS), F32),
                        pltpu.VMEM((8, LANES), F32),
                        pltpu.SMEM((1,), jnp.int32)],
        compiler_params=pltpu.CompilerParams(
            dimension_semantics=("arbitrary", "arbitrary", "arbitrary"),
            vmem_limit_bytes=VMEM_LIMIT),
        name="diff_attention",
    )(dq, dq, dk, dv, dz, norm_w, lq1, lk1, lq2, lk2)


def _out_kernel(x_ref, oa_ref, ob_ref, mab_ref, wa_ref, wb_ref, wo_ref, g_ref, b_ref, y_ref,
                merged_scr):
    for c0 in range(0, D_MODEL, OUT_CHUNK):
        cs = slice(c0, c0 + OUT_CHUNK)
        ya = _dot(oa_ref[...], wa_ref[:, cs])
        yb = _dot(ob_ref[...], wb_ref[:, cs])
        ga = _sigmoid(mab_ref[:, c0:c0 + OUT_CHUNK].astype(F32))
        gb = _sigmoid(mab_ref[:, D_MODEL + c0:D_MODEL + c0 + OUT_CHUNK].astype(F32))
        merged_scr[:, cs] = (ga * ya + gb * yb).astype(BF16)
    r0 = 0
    for nrows in OUT_ROW_GROUPS:
        rs = slice(r0, r0 + nrows)
        r0 += nrows
        y = _dot(merged_scr[rs, :], wo_ref[...])
        z = DEEPNORM_ALPHA * x_ref[rs, :] + y
        mu = jnp.mean(z, axis=-1, keepdims=True)
        zc = z - mu
        var = jnp.mean(zc * zc, axis=-1, keepdims=True)
        y_ref[rs, :] = zc * lax.rsqrt(var + LN_EPS) * g_ref[...] + b_ref[...]


def _output(x2, o_a, o_b, mab, wa, wb, wo, ln_g, ln_b):
    m = x2.shape[0]
    tm = OUT_TM
    row = lambda width: pl.BlockSpec((tm, width), lambda i: (i, 0))
    full = lambda a: pl.BlockSpec(a.shape, lambda i: (0, 0))
    return pl.pallas_call(
        _out_kernel,
        grid=(m // tm,),
        in_specs=[row(D_MODEL), row(GDN_WIDTH), row(DIFF_WIDTH), row(2 * D_MODEL),
                  full(wa), full(wb), full(wo), full(ln_g), full(ln_b)],
        out_specs=row(D_MODEL),
        out_shape=jax.ShapeDtypeStruct((m, D_MODEL), F32),
        scratch_shapes=[pltpu.VMEM((tm, D_MODEL), BF16)],
        compiler_params=pltpu.CompilerParams(
            dimension_semantics=("arbitrary",), vmem_limit_bytes=VMEM_LIMIT),
        name="merge_out_ln",
    )(x2, o_a, o_b, mab, wa, wb, wo, ln_g, ln_b)


def _lane_row(v):
    return jnp.zeros((1, LANES), F32).at[0, :v.shape[0]].set(v.astype(F32))


def kernel(x, w_in, conv_w, a_log, dt_bias, gdn_norm_w, w_up_a, lambda_q1, lambda_k1,
           lambda_q2, lambda_k2, diff_norm_w, w_up_b, w_out, ln_g, ln_b):
    batch, seq, d = x.shape
    x2 = x.reshape(batch * seq, d)
    layer = 0
    w = w_in[layer].astype(BF16)
    w_tail = w[:, PROJ_HEAD + 2 * GDN_HEADS:]

    gq, gk, gv, hab, gz, dq, dk, dv, dz, mab = _project(
        x2, w, w_tail, conv_w[layer].astype(F32), seq)

    o_a = _gdn(gq, gk, gv, hab, gz, _lane_row(a_log[layer]), _lane_row(dt_bias[layer]),
               gdn_norm_w[layer].reshape(1, GDN_DV).astype(F32), batch, seq)

    o_b = _diff_attention(
        dq, dk, dv, dz, diff_norm_w[layer].reshape(1, DIFF_DV).astype(F32),
        lambda_q1[layer].reshape(1, DIFF_DH).astype(F32),
        lambda_k1[layer].reshape(1, DIFF_DH).astype(F32),
        lambda_q2[layer].reshape(1, DIFF_DH).astype(F32),
        lambda_k2[layer].reshape(1, DIFF_DH).astype(F32), batch, seq)

    y = _output(x2, o_a, o_b, mab, w_up_a[layer].astype(BF16), w_up_b[layer].astype(BF16),
                w_out[layer].astype(BF16), ln_g[layer].reshape(1, d).astype(F32),
                ln_b[layer].reshape(1, d).astype(F32))
    return y.reshape(batch, seq, d)
```

```python
import functools
import math

import jax
import jax.numpy as jnp
from jax import lax
from jax.experimental import pallas as pl
from jax.experimental.pallas import tpu as pltpu

F32 = jnp.float32
BF16 = jnp.bfloat16

D_MODEL = 1024
GDN_HEADS = 4
GDN_DK = 128
GDN_DV = 128
GDN_QK = GDN_HEADS * GDN_DK
GDN_WIDTH = GDN_HEADS * GDN_DV
CONV_K = 4
DIFF_HEADS = 4
DIFF_DH = 64
DIFF_DV = 2 * DIFF_DH
DIFF_QK = DIFF_HEADS * 2 * DIFF_DH
DIFF_WIDTH = DIFF_HEADS * DIFF_DV
NORM_EPS = 1e-6
SUBLN_EPS = 1e-5
LN_EPS = 1e-5
DEPTH = 1
DEEPNORM_ALPHA = (2.0 * DEPTH) ** 0.25
LAM_INIT = 0.8 - 0.6 * math.exp(-0.3 * 0)
ALIBI_SLOPES = tuple(2.0 ** (-8.0 * (i + 1) / DIFF_HEADS) for i in range(DIFF_HEADS))
LOG2E = math.log2(math.e)

LANES = 128
CONV_HIST = 8
VMEM_LIMIT = 56 * 1024 * 1024

PROJ_TM = 256
PROJ_HEAD = 2 * GDN_QK + GDN_WIDTH
GDN_T = 256
GDN_STEP_CHUNKS = 4
GDN_INV_BASE = 16
ATT_TQ = 512
ATT_TK = 512
ATT_ROWS = 256
OUT_TM = 512
OUT_CHUNK = 256
OUT_ROW_GROUPS = (256, 256)
assert sum(OUT_ROW_GROUPS) == OUT_TM
MASK_NEG = -1e30
ATT_SKIP_EXP2 = 150.0
ATT_SKIP_SLACK = 1.0
ATT_FIXED_REF_MAX = 48.0


def _fixed_ref_steps(head):
    slope = ALIBI_SLOPES[head] * LOG2E
    return max(math.ceil(ATT_SKIP_EXP2 / (slope * ATT_TK) + (ATT_TK - 1) / ATT_TK) - 1, 0)


def _sigmoid(x):
    return 1.0 / (1.0 + jnp.exp(-x))


def _silu(x):
    return x * _sigmoid(x)


def _dot(a, b):
    return jnp.dot(a, b, preferred_element_type=F32)


def _dot_nt(a, b):
    return lax.dot_general(a, b, (((1,), (1,)), ((), ())), preferred_element_type=F32)


def _proj_kernel(tiles_per_seq, x_ref, wh_ref, wt_ref, cw_ref, gq_ref, gk_ref, gv_ref,
                 hab_ref, gz_ref, dq_ref, dk_ref, dv_ref, dz_ref, mab_ref, hist_scr, xb_scr):
    tm = PROJ_TM

    @pl.when(pl.program_id(0) % tiles_per_seq == 0)
    def _():
        hist_scr[...] = jnp.zeros_like(hist_scr)

    xb_scr[...] = x_ref[...].astype(BF16)

    def mm(c0, width):
        if c0 < PROJ_HEAD:
            return _dot(xb_scr[...], wh_ref[:, c0:c0 + width])
        return _dot(xb_scr[...], wt_ref[:, c0 - PROJ_HEAD:c0 - PROJ_HEAD + width])

    piece = 2 * GDN_DK

    def conv_silu(c0):
        cols = slice(c0, c0 + piece)
        acc = mm(c0, piece)
        ext = jnp.concatenate([hist_scr[:, cols], acc], axis=0)
        hist_scr[:, cols] = acc[tm - CONV_HIST:]
        y = None
        for j in range(CONV_K):
            r0 = CONV_HIST - (CONV_K - 1) + j
            term = cw_ref[j:j + 1, cols] * ext[r0:r0 + tm]
            y = term if y is None else y + term
        return _silu(y)

    def l2norm_heads(y, scale):
        parts = []
        for h in range(piece // GDN_DK):
            v = y[:, h * GDN_DK:(h + 1) * GDN_DK]
            inv = lax.rsqrt(jnp.sum(v * v, axis=-1, keepdims=True) + NORM_EPS)
            parts.append(v * (inv * scale) if scale != 1.0 else v * inv)
        return jnp.concatenate(parts, axis=1)

    def plain(out_ref, c0, scale=None):
        for j in range(out_ref.shape[1] // 512):
            acc = mm(c0 + j * 512, 512)
            if scale is not None:
                acc = acc * scale
            out_ref[:, j * 512:(j + 1) * 512] = acc.astype(out_ref.dtype)

    for half in range(2):
        c0 = half * piece
        gq_ref[:, c0:c0 + piece] = l2norm_heads(conv_silu(c0), GDN_DK ** -0.5).astype(BF16)
        if half == 0:
            plain(gz_ref, 1536)
        else:
            plain(dq_ref, 2048, DIFF_DH ** -0.5 * LOG2E)
    for half in range(2):
        c0 = half * piece
        gk_ref[:, c0:c0 + piece] = l2norm_heads(conv_silu(GDN_QK + c0), 1.0).astype(BF16)
        plain(dk_ref if half == 0 else dv_ref, 2560 + half * 512)
    for half in range(2):
        c0 = half * piece
        gv_ref[:, c0:c0 + piece] = conv_silu(2 * GDN_QK + c0).astype(BF16)
        if half == 0:
            plain(dz_ref, 3584)
    hab_ref[...] = _dot(xb_scr[...], wh_ref[:, PROJ_HEAD:PROJ_HEAD + LANES])
    plain(mab_ref, 4096)


def _project(x2, w_bf, w_tail, conv_w, seq):
    m = x2.shape[0]
    tm = PROJ_TM
    row = lambda width: pl.BlockSpec((tm, width), lambda i: (i, 0))
    full = lambda a: pl.BlockSpec(a.shape, lambda i: (0, 0))
    out_shape = (
        jax.ShapeDtypeStruct((m, GDN_QK), BF16),
        jax.ShapeDtypeStruct((m, GDN_QK), BF16),
        jax.ShapeDtypeStruct((m, GDN_WIDTH), BF16),
        jax.ShapeDtypeStruct((m, LANES), F32),
        jax.ShapeDtypeStruct((m, 512), BF16),
        jax.ShapeDtypeStruct((m, 512), BF16),
        jax.ShapeDtypeStruct((m, 512), BF16),
        jax.ShapeDtypeStruct((m, 512), BF16),
        jax.ShapeDtypeStruct((m, 512), BF16),
        jax.ShapeDtypeStruct((m, 2048), BF16),
    )
    return pl.pallas_call(
        functools.partial(_proj_kernel, seq // tm),
        grid=(m // tm,),
        in_specs=[row(D_MODEL),
                  pl.BlockSpec((D_MODEL, PROJ_HEAD + LANES), lambda i: (0, 0)),
                  full(w_tail), full(conv_w)],
        out_specs=tuple(row(s.shape[1]) for s in out_shape),
        out_shape=out_shape,
        scratch_shapes=[pltpu.VMEM((CONV_HIST, 3 * GDN_QK), F32),
                        pltpu.VMEM((tm, D_MODEL), BF16)],
        compiler_params=pltpu.CompilerParams(
            dimension_semantics=("arbitrary",), vmem_limit_bytes=VMEM_LIMIT),
        name="in_proj",
    )(x2, w_bf, w_tail, conv_w)


def _gdn_kernel(gq_ref, gk_ref, gv_ref, hab_ref, gz_ref, alog_ref, dtb_ref, nw_ref, o_ref,
                st_scr):
    @pl.when(pl.program_id(1) == 0)
    def _():
        st_scr[...] = jnp.zeros_like(st_scr)

    for ck in range(GDN_STEP_CHUNKS):
        r = pl.ds(ck * GDN_T, GDN_T)
        prep = _gdn_prep(gq_ref.at[r], gk_ref.at[r], gv_ref.at[r], hab_ref.at[r], alog_ref,
                         dtb_ref)
        _gdn_apply(prep, gz_ref.at[r], nw_ref, o_ref.at[r], st_scr)


def _gdn_prep(gq_ref, gk_ref, gv_ref, hab_ref, alog_ref, dtb_ref):
    t = GDN_T
    heads = range(GDN_HEADS)

    hab = hab_ref[...]
    xg = hab + dtb_ref[...]
    y_sp = jnp.exp(-jnp.abs(xg))
    u_sp = 1.0 + y_sp
    softplus = jnp.maximum(xg, 0.0) + (jnp.log(u_sp) - ((u_sp - 1.0) - y_sp) / u_sp)
    g_full = -jnp.exp(alog_ref[...]) * softplus
    beta_full = _sigmoid(hab)

    ri = lax.broadcasted_iota(jnp.int32, (t, t), 0)
    ci = lax.broadcasted_iota(jnp.int32, (t, t), 1)
    causal = ri >= ci
    strict = ri > ci
    eye = jnp.where(ri == ci, 1.0, 0.0).astype(F32)

    tri = jnp.where(causal, 1.0, 0.0).astype(BF16)
    g_hi = g_full.astype(BF16)
    g_r1 = g_full - g_hi.astype(F32)
    g_mid = g_r1.astype(BF16)
    g_lo = (g_r1 - g_mid.astype(F32)).astype(BF16)
    gc = _dot(tri, g_hi) + _dot(tri, g_mid) + _dot(tri, g_lo)
    gc_t = gc.T
    eg = jnp.exp(gc)
    g_last = gc[t - 1:t, :]
    k_dec = jnp.exp(g_last - gc)
    eg_last = jnp.exp(g_last)

    qn_b = [gq_ref[:, h * GDN_DK:(h + 1) * GDN_DK] for h in heads]
    kn_b = [gk_ref[:, h * GDN_DK:(h + 1) * GDN_DK] for h in heads]
    qn = [qn_b[h].astype(F32) for h in heads]
    kn = [kn_b[h].astype(F32) for h in heads]
    vv = [gv_ref[:, h * GDN_DV:(h + 1) * GDN_DV].astype(F32) for h in heads]

    bcol = [beta_full[:, GDN_HEADS + h:GDN_HEADS + h + 1] for h in heads]
    egc = [eg[:, h:h + 1] for h in heads]
    kb = [kn[h] * bcol[h] for h in heads]
    kq = [_dot_nt(jnp.concatenate([kb[h].astype(BF16), qn_b[h]], axis=0), kn_b[h])
          for h in heads]
    decay = []
    for h in heads:
        gd = gc[:, h:h + 1] - gc_t[h:h + 1, :]
        decay.append(jnp.where(causal, jnp.exp(jnp.where(causal, gd, 0.0)), 0.0))
    nmat = [jnp.where(strict, -(kq[h][:t] * decay[h]), 0.0) for h in heads]
    amat = [(kq[h][t:] * decay[h]).astype(BF16) for h in heads]
    rhs = [jnp.concatenate([vv[h] * bcol[h], kb[h] * egc[h]], axis=1).astype(BF16)
           for h in heads]
    qe = [qn[h] * egc[h] for h in heads]
    kd_t = [(kn[h] * k_dec[:, h:h + 1]).T.astype(BF16) for h in heads]

    def same_block(b):
        sh = int(math.log2(b))
        return (ri >> sh) == (ci >> sh)

    blk = GDN_INV_BASE
    in_blk = same_block(blk)
    n0 = [jnp.where(in_blk, nmat[h], 0.0) for h in heads]
    x = [eye + n0[h] for h in heads]
    pw = [n0[h].astype(BF16) for h in heads]
    for _ in range(int(math.log2(blk)) - 1):
        p32 = [_dot(pw[h], pw[h]) for h in heads]
        pw = [p32[h].astype(BF16) for h in heads]
        x = [x[h] + _dot(x[h].astype(BF16), pw[h]) for h in heads]
    while blk < t:
        in_big = same_block(2 * blk)
        n_off = [jnp.where(in_big, jnp.where(in_blk, 0.0, nmat[h]), 0.0).astype(BF16)
                 for h in heads]
        xb = [x[h].astype(BF16) for h in heads]
        xn = [_dot(xb[h], n_off[h]).astype(BF16) for h in heads]
        x = [x[h] + _dot(xn[h], xb[h]) for h in heads]
        in_blk = in_big
        blk *= 2
    sol = [_dot(x[h].astype(BF16), rhs[h]) for h in heads]
    return sol, qe, amat, kd_t, eg_last


def _gdn_apply(prep, gz_ref, nw_ref, o_ref, st_scr):
    t = GDN_T
    heads = range(GDN_HEADS)
    sol, qe, amat, kd_t, eg_last = prep
    s_prev = [st_scr[h] for h in heads]
    ws = [_dot(jnp.concatenate([sol[h][:, GDN_DV:], qe[h]], axis=0).astype(BF16),
               s_prev[h].astype(BF16)) for h in heads]
    v_new = [(sol[h][:, :GDN_DV] - ws[h][:t]).astype(BF16) for h in heads]
    o_l = [ws[h][t:] + _dot(amat[h], v_new[h]) for h in heads]
    for h in heads:
        st_scr[h] = s_prev[h] * eg_last[:, h:h + 1] + _dot(kd_t[h], v_new[h])
    nw = nw_ref[...]
    for h in heads:
        o = o_l[h]
        o = o * lax.rsqrt(jnp.mean(o * o, axis=-1, keepdims=True) + NORM_EPS) * nw
        gate = _silu(gz_ref[:, h * GDN_DV:(h + 1) * GDN_DV].astype(F32))
        o_ref[:, h * GDN_DV:(h + 1) * GDN_DV] = (o * gate).astype(o_ref.dtype)


def _gdn(gq, gk, gv, hab, gz, alog_row, dtb_row, norm_w, batch, seq):
    t = GDN_T * GDN_STEP_CHUNKS
    nt = seq // t
    row = lambda width: pl.BlockSpec((t, width), lambda b, s: (b * nt + s, 0))
    full = lambda a: pl.BlockSpec(a.shape, lambda b, s: (0, 0))
    return pl.pallas_call(
        _gdn_kernel,
        grid=(batch, nt),
        in_specs=[row(GDN_QK), row(GDN_QK), row(GDN_WIDTH), row(LANES), row(GDN_WIDTH),
                  full(alog_row), full(dtb_row), full(norm_w)],
        out_specs=row(GDN_WIDTH),
        out_shape=jax.ShapeDtypeStruct((batch * seq, GDN_WIDTH), BF16),
        scratch_shapes=[pltpu.VMEM((GDN_HEADS, GDN_DK, GDN_DV), F32)],
        compiler_params=pltpu.CompilerParams(
            dimension_semantics=("arbitrary", "arbitrary"), vmem_limit_bytes=VMEM_LIMIT),
        name="gated_deltanet",
    )(gq, gk, gv, hab, gz, alog_row, dtb_row, norm_w)


def _attn_kernel(q_ref, qall_ref, k_ref, v_ref, dz_ref, nw_ref, lq1_ref, lk1_ref, lq2_ref,
                 lk2_ref, o_ref, ke_scr, vt_scr, m_scr, acc_scr, l_scr, p_scr, alpha_scr,
                 bound_scr, flag_smem):
    tq, tk, cols = ATT_TQ, ATT_TK, ATT_ROWS
    n_half = tq // cols
    seq = k_ref.shape[0]
    h = pl.program_id(1)
    i = pl.program_id(2)
    slope = jnp.float32(ALIBI_SLOPES[DIFF_HEADS - 1] * LOG2E)
    n_cap = jnp.int32(_fixed_ref_steps(DIFF_HEADS - 1))
    for hh in range(DIFF_HEADS - 1):
        slope = jnp.where(h == hh, jnp.float32(ALIBI_SLOPES[hh] * LOG2E), slope)
        n_cap = jnp.where(h == hh, jnp.int32(_fixed_ref_steps(hh)), n_cap)

    lane = lax.broadcasted_iota(jnp.int32, (tk, LANES), 1)
    feat_lane = (lane - DIFF_DH, lane)

    def half_norm_max(vf):
        hl = lax.broadcasted_iota(jnp.int32, (LANES, LANES), 0)
        hc = lax.broadcasted_iota(jnp.int32, (LANES, LANES), 1)
        pick = jnp.where(hc == jnp.where(hl < DIFF_DH, 0, 1), 1.0, 0.0).astype(BF16)
        sums = jnp.max(_dot((vf * vf).astype(BF16), pick), axis=0, keepdims=True)
        return jnp.sqrt(jnp.maximum(sums[:, 0:1], sums[:, 1:2])) * (1.0 + 2.0 ** -6)

    @pl.when(i == 0)
    def _():
        bound_scr[...] = jnp.zeros_like(bound_scr)

        def build(blk, carry):
            r0 = pl.multiple_of(blk * tk, tk)
            kf = k_ref[pl.ds(r0, tk), :].astype(F32)
            bound_scr[0:1, :] = jnp.maximum(bound_scr[0:1, :], half_norm_max(kf))
            bound_scr[1:2, :] = jnp.maximum(
                bound_scr[1:2, :], half_norm_max(qall_ref[pl.ds(r0, tk), :].astype(F32)))
            pos = (lax.broadcasted_iota(jnp.int32, (tk, LANES), 0) + r0).astype(F32) * slope
            p_hi = pos.astype(BF16).astype(F32)
            r1 = pos - p_hi
            p_mid = r1.astype(BF16).astype(F32)
            p_lo = r1 - p_mid
            for mp in range(2):
                fl = feat_lane[mp]
                feat = jnp.where(fl == 0, p_hi, jnp.where(fl == 1, p_mid, jnp.where(
                    fl == 2, p_lo, jnp.where((fl >= 3) & (fl < 6), 1.0, 0.0))))
                own = (lane < DIFF_DH) if mp == 0 else (lane >= DIFF_DH)
                ke_scr[mp, pl.ds(r0, tk), :] = jnp.where(own, kf, feat).astype(BF16)
            vt = v_ref[pl.ds(r0, tk), :].astype(F32).T
            vt_scr[blk] = vt.astype(BF16)
            return carry

        lax.fori_loop(0, seq // tk, build, 0, unroll=2)
        qk_all = bound_scr[0:1, :] * bound_scr[1:2, :]
        bound_scr[2:3, :] = qk_all
        flag_smem[0] = jnp.where(jnp.max(qk_all) <= ATT_FIXED_REF_MAX, 1, 0).astype(jnp.int32)

    qf = q_ref[...].astype(F32)
    qk_bound = bound_scr[2:3, 0:1]
    chains = [(mp, r) for mp in range(2) for r in range(n_half)]
    n_chains = len(chains)

    def query_maps(ref_terms):
        out = []
        for mp in range(2):
            fl = feat_lane[mp]
            own = (lane < DIFF_DH) if mp == 0 else (lane >= DIFF_DH)
            feat = jnp.where((fl >= 0) & (fl < 3), 1.0, 0.0)
            if ref_terms is not None:
                r_hi, r_mid, r_lo = ref_terms
                feat = jnp.where(fl == 3, r_hi, jnp.where(fl == 4, r_mid,
                                                          jnp.where(fl == 5, r_lo, feat)))
            out.append(jnp.where(own, qf, feat).astype(BF16))
        return out

    def scores_t(qmaps, c, j, masked):
        mp, r = chains[c]
        nk = (r + 1) * cols if masked else tk
        qc = qmaps[mp][r * cols:(r + 1) * cols]
        kb = ke_scr[mp, pl.ds(pl.multiple_of(j * tk, tk), nk), :]
        st = _dot_nt(kb, qc)
        if masked:
            key_l = lax.broadcasted_iota(jnp.int32, (nk, cols), 0)
            qry_l = lax.broadcasted_iota(jnp.int32, (nk, cols), 1)
            st = jnp.where(key_l <= qry_l + r * cols, st, MASK_NEG)
        return st

    def key_sums(p32):
        nk = p32.shape[0]
        return jnp.sum(p32.reshape(nk // 8, 8, cols), axis=0)

    def store_p(slot, c, p32):
        p = p32.astype(BF16)
        nk = p.shape[0]
        p_scr[slot, c, 0:nk, :] = p
        if nk < tk:
            p_scr[slot, c, nk:, :] = jnp.zeros((tk - nk, cols), BF16)

    def finish(acc_fin):
        def map_out(mp):
            parts = []
            for r in range(n_half):
                c = mp * n_half + r
                denom = jnp.sum(l_scr[c], axis=0, keepdims=True)
                parts.append(acc_fin[c] * (1.0 / denom))
            return jnp.concatenate(parts, axis=1)

        lam = (jnp.exp(jnp.sum(lq1_ref[...] * lk1_ref[...], axis=-1, keepdims=True))
               - jnp.exp(jnp.sum(lq2_ref[...] * lk2_ref[...], axis=-1, keepdims=True))
               + LAM_INIT)
        o_t = map_out(0) - lam * map_out(1)
        o_t = o_t * lax.rsqrt(jnp.mean(o_t * o_t, axis=0, keepdims=True) + SUBLN_EPS)
        o = o_t.T * (nw_ref[...] * (1.0 - LAM_INIT)) * _silu(dz_ref[...].astype(F32))
        o_ref[...] = o.astype(o_ref.dtype)

    def pipeline(scores_step, pv_value, steps_after_diag):
        def scores_block(j, slot, masked=False):
            for c in range(n_chains):
                scores_step(c, j, slot, masked)

        def pv_block(j, slot):
            for c in range(n_chains):
                acc_scr[c] = pv_value(c, j, slot)

        acc_scr[...] = jnp.zeros_like(acc_scr)
        l_scr[...] = jnp.zeros_like(l_scr)
        scores_block(i, 0, masked=True)
        n = steps_after_diag()
        last = i - n

        def pair(u, carry):
            j = i - 2 * u - 1
            scores_block(j, 1)
            pv_block(j + 1, 0)
            scores_block(j - 1, 0)
            pv_block(j, 1)
            return carry

        lax.fori_loop(0, n // 2, pair, 0)

        @pl.when(n % 2 == 1)
        def _():
            scores_block(last, 1)
            pv_block(last + 1, 0)
            finish([pv_value(c, last, 1) for c in range(n_chains)])

        @pl.when(n % 2 == 0)
        def _():
            finish([pv_value(c, last, 0) for c in range(n_chains)])

    use_fixed_ref = flag_smem[0] == 1

    @pl.when(use_fixed_ref)
    def _():
        pos_q = (lax.broadcasted_iota(jnp.int32, (tq, LANES), 0) + i * tq).astype(F32) * slope
        neg_ref = -(qk_bound + ATT_SKIP_SLACK + pos_q)
        r_hi = neg_ref.astype(BF16).astype(F32)
        r1 = neg_ref - r_hi
        r_mid = r1.astype(BF16).astype(F32)
        qmaps = query_maps((r_hi, r_mid, r1 - r_mid))

        def scores_step(c, j, slot, masked):
            p32 = jnp.exp2(scores_t(qmaps, c, j, masked))
            l_scr[c] = l_scr[c] + key_sums(p32)
            store_p(slot, c, p32)

        def pv_value(c, j, slot):
            return acc_scr[c] + _dot(vt_scr[j], p_scr[slot, c])

        pipeline(scores_step, pv_value, lambda: jnp.minimum(i, n_cap))

    @pl.when(jnp.logical_not(use_fixed_ref))
    def _():
        qmaps = query_maps(None)
        m_scr[...] = jnp.full(m_scr.shape, MASK_NEG, F32)

        def scores_step(c, j, slot, masked):
            st = scores_t(qmaps, c, j, masked)
            m_prev = m_scr[c]
            m_new = jnp.maximum(m_prev, jnp.max(st, axis=0, keepdims=True))
            p32 = jnp.exp2(st - m_new)
            alpha = jnp.exp2(m_prev - m_new)
            l_scr[c] = l_scr[c] * alpha + key_sums(p32)
            store_p(slot, c, p32)
            alpha_scr[slot, c] = alpha
            m_scr[c] = m_new

        def pv_value(c, j, slot):
            pv = _dot(vt_scr[j], p_scr[slot, c])
            return acc_scr[c] * alpha_scr[slot, c] + pv

        def steps_after_diag():
            m_min = jnp.min(jnp.min(m_scr[...], axis=0), axis=-1, keepdims=True)
            need = m_min - (ATT_SKIP_EXP2 + ATT_SKIP_SLACK) - qk_bound
            j_first = jnp.ceil(need / (slope * tk) - (tk - 1) / tk)
            n_arr = jnp.clip(i.astype(F32) - jnp.maximum(j_first, 0.0), 0.0, i.astype(F32))
            return jnp.max(n_arr).astype(jnp.int32)

        pipeline(scores_step, pv_value, steps_after_diag)


def _diff_attention(dq, dk, dv, dz, norm_w, lq1, lk1, lq2, lk2, batch, seq):
    tq, tk = ATT_TQ, ATT_TK
    nq = seq // tq
    n_chains = 2 * (tq // ATT_ROWS)
    qspec = pl.BlockSpec((tq, DIFF_DV), lambda b, h, i: (b * nq + i, h))
    kvspec = pl.BlockSpec((seq, DIFF_DV), lambda b, h, i: (b, h))
    full = lambda a: pl.BlockSpec(a.shape, lambda b, h, i: (0, 0))
    return pl.pallas_call(
        _attn_kernel,
        grid=(batch, DIFF_HEADS, nq),
        in_specs=[qspec, kvspec, kvspec, kvspec, qspec, full(norm_w), full(lq1), full(lk1),
                  full(lq2), full(lk2)],
        out_specs=qspec,
        out_shape=jax.ShapeDtypeStruct((batch * seq, DIFF_WIDTH), BF16),
        scratch_shapes=[pltpu.VMEM((2, seq, DIFF_DV), BF16),
                        pltpu.VMEM((seq // tk, DIFF_DV, tk), BF16),
                        pltpu.VMEM((n_chains, 1, ATT_ROWS), F32),
                        pltpu.VMEM((n_chains, DIFF_DV, ATT_ROWS), F32),
                        pltpu.VMEM((n_chains, 8, ATT_ROWS), F32),
                        pltpu.VMEM((2, n_chains, tk, ATT_ROWS), BF16),
                        pltpu.VMEM((2, n_chains, 1, ATT_ROWS), F32),
                        pltpu.VMEM((8, LANES), F32),
                        pltpu.SMEM((1,), jnp.int32)],
        compiler_params=pltpu.CompilerParams(
            dimension_semantics=("arbitrary", "arbitrary", "arbitrary"),
            vmem_limit_bytes=VMEM_LIMIT),
        name="diff_attention",
    )(dq, dq, dk, dv, dz, norm_w, lq1, lk1, lq2, lk2)


def _out_kernel(x_ref, oa_ref, ob_ref, mab_ref, wa_ref, wb_ref, wo_ref, g_ref, b_ref, y_ref,
                merged_scr):
    for c0 in range(0, D_MODEL, OUT_CHUNK):
        cs = slice(c0, c0 + OUT_CHUNK)
        ya = _dot(oa_ref[...], wa_ref[:, cs])
        yb = _dot(ob_ref[...], wb_ref[:, cs])
        ga = _sigmoid(mab_ref[:, c0:c0 + OUT_CHUNK].astype(F32))
        gb = _sigmoid(mab_ref[:, D_MODEL + c0:D_MODEL + c0 + OUT_CHUNK].astype(F32))
        merged_scr[:, cs] = (ga * ya + gb * yb).astype(BF16)
    r0 = 0
    for nrows in OUT_ROW_GROUPS:
        rs = slice(r0, r0 + nrows)
        r0 += nrows
        y = _dot(merged_scr[rs, :], wo_ref[...])
        z = DEEPNORM_ALPHA * x_ref[rs, :] + y
        mu = jnp.mean(z, axis=-1, keepdims=True)
        zc = z - mu
        var = jnp.mean(zc * zc, axis=-1, keepdims=True)
        y_ref[rs, :] = zc * lax.rsqrt(var + LN_EPS) * g_ref[...] + b_ref[...]


def _output(x2, o_a, o_b, mab, wa, wb, wo, ln_g, ln_b):
    m = x2.shape[0]
    tm = OUT_TM
    row = lambda width: pl.BlockSpec((tm, width), lambda i: (i, 0))
    full = lambda a: pl.BlockSpec(a.shape, lambda i: (0, 0))
    return pl.pallas_call(
        _out_kernel,
        grid=(m // tm,),
        in_specs=[row(D_MODEL), row(GDN_WIDTH), row(DIFF_WIDTH), row(2 * D_MODEL),
                  full(wa), full(wb), full(wo), full(ln_g), full(ln_b)],
        out_specs=row(D_MODEL),
        out_shape=jax.ShapeDtypeStruct((m, D_MODEL), F32),
        scratch_shapes=[pltpu.VMEM((tm, D_MODEL), BF16)],
        compiler_params=pltpu.CompilerParams(
            dimension_semantics=("arbitrary",), vmem_limit_bytes=VMEM_LIMIT),
        name="merge_out_ln",
    )(x2, o_a, o_b, mab, wa, wb, wo, ln_g, ln_b)


def _lane_row(v):
    return jnp.zeros((1, LANES), F32).at[0, :v.shape[0]].set(v.astype(F32))


def kernel(x, w_in, conv_w, a_log, dt_bias, gdn_norm_w, w_up_a, lambda_q1, lambda_k1,
           lambda_q2, lambda_k2, diff_norm_w, w_up_b, w_out, ln_g, ln_b):
    batch, seq, d = x.shape
    x2 = x.reshape(batch * seq, d)
    layer = 0
    w = w_in[layer].astype(BF16)
    w_tail = w[:, PROJ_HEAD + 2 * GDN_HEADS:]

    gq, gk, gv, hab, gz, dq, dk, dv, dz, mab = _project(
        x2, w, w_tail, conv_w[layer].astype(F32), seq)

    o_a = _gdn(gq, gk, gv, hab, gz, _lane_row(a_log[layer]), _lane_row(dt_bias[layer]),
               gdn_norm_w[layer].reshape(1, GDN_DV).astype(F32), batch, seq)

    o_b = _diff_attention(
        dq, dk, dv, dz, diff_norm_w[layer].reshape(1, DIFF_DV).astype(F32),
        lambda_q1[layer].reshape(1, DIFF_DH).astype(F32),
        lambda_k1[layer].reshape(1, DIFF_DH).astype(F32),
        lambda_q2[layer].reshape(1, DIFF_DH).astype(F32),
        lambda_k2[layer].reshape(1, DIFF_DH).astype(F32), batch, seq)

    y = _output(x2, o_a, o_b, mab, w_up_a[layer].astype(BF16), w_up_b[layer].astype(BF16),
                w_out[layer].astype(BF16), ln_g[layer].reshape(1, d).astype(F32),
                ln_b[layer].reshape(1, d).astype(F32))
    return y.reshape(batch, seq, d)
```

```python
import functools
import math

import jax
import jax.numpy as jnp
from jax import lax
from jax.experimental import pallas as pl
from jax.experimental.pallas import tpu as pltpu

F32 = jnp.float32
BF16 = jnp.bfloat16

D_MODEL = 1024
GDN_HEADS = 4
GDN_DK = 128
GDN_DV = 128
GDN_QK = GDN_HEADS * GDN_DK
GDN_WIDTH = GDN_HEADS * GDN_DV
CONV_K = 4
DIFF_HEADS = 4
DIFF_DH = 64
DIFF_DV = 2 * DIFF_DH
DIFF_QK = DIFF_HEADS * 2 * DIFF_DH
DIFF_WIDTH = DIFF_HEADS * DIFF_DV
NORM_EPS = 1e-6
SUBLN_EPS = 1e-5
LN_EPS = 1e-5
DEPTH = 1
DEEPNORM_ALPHA = (2.0 * DEPTH) ** 0.25
LAM_INIT = 0.8 - 0.6 * math.exp(-0.3 * 0)
ALIBI_SLOPES = tuple(2.0 ** (-8.0 * (i + 1) / DIFF_HEADS) for i in range(DIFF_HEADS))
LOG2E = math.log2(math.e)

LANES = 128
CONV_HIST = 8
VMEM_LIMIT = 56 * 1024 * 1024

PROJ_TM = 256
PROJ_STEP_TILES = 2
PROJ_HEAD = 2 * GDN_QK + GDN_WIDTH
GDN_T = 256
GDN_STEP_CHUNKS = 4
GDN_INV_BASE = 16
ATT_TQ = 512
ATT_TK = 512
ATT_ROWS = 256
OUT_TM = 512
OUT_STEP_TILES = 2
OUT_CHUNK = 256
OUT_ROW_GROUPS = (256, 256)
assert sum(OUT_ROW_GROUPS) == OUT_TM
MASK_NEG = -1e30
ATT_SKIP_EXP2 = 150.0
ATT_SKIP_SLACK = 1.0
ATT_FIXED_REF_MAX = 48.0


def _fixed_ref_steps(head):
    slope = ALIBI_SLOPES[head] * LOG2E
    return max(math.ceil(ATT_SKIP_EXP2 / (slope * ATT_TK) + (ATT_TK - 1) / ATT_TK) - 1, 0)


def _sigmoid(x):
    return 1.0 / (1.0 + jnp.exp(-x))


def _silu(x):
    return x * _sigmoid(x)


def _dot(a, b):
    return jnp.dot(a, b, preferred_element_type=F32)


def _dot_nt(a, b):
    return lax.dot_general(a, b, (((1,), (1,)), ((), ())), preferred_element_type=F32)


def _proj_kernel(tiles_per_seq, x_ref, wh_ref, wt_ref, cw_ref, *rest):
    out_refs, (hist_scr, xb_scr) = rest[:-2], rest[-2:]
    for tile in range(PROJ_STEP_TILES):
        r = pl.ds(tile * PROJ_TM, PROJ_TM)
        _proj_tile(pl.program_id(0) * PROJ_STEP_TILES + tile, tiles_per_seq, x_ref.at[r],
                   wh_ref, wt_ref, cw_ref, *[o.at[r] for o in out_refs], hist_scr, xb_scr)


def _proj_tile(tile_idx, tiles_per_seq, x_ref, wh_ref, wt_ref, cw_ref, gq_ref, gk_ref, gv_ref,
               hab_ref, gz_ref, dq_ref, dk_ref, dv_ref, dz_ref, mab_ref, hist_scr, xb_scr):
    tm = PROJ_TM

    @pl.when(tile_idx % tiles_per_seq == 0)
    def _():
        hist_scr[...] = jnp.zeros_like(hist_scr)

    xb_scr[...] = x_ref[...].astype(BF16)

    def mm(c0, width):
        if c0 < PROJ_HEAD:
            return _dot(xb_scr[...], wh_ref[:, c0:c0 + width])
        return _dot(xb_scr[...], wt_ref[:, c0 - PROJ_HEAD:c0 - PROJ_HEAD + width])

    piece = 2 * GDN_DK

    def conv_silu(c0):
        cols = slice(c0, c0 + piece)
        acc = mm(c0, piece)
        ext = jnp.concatenate([hist_scr[:, cols], acc], axis=0)
        hist_scr[:, cols] = acc[tm - CONV_HIST:]
        y = None
        for j in range(CONV_K):
            r0 = CONV_HIST - (CONV_K - 1) + j
            term = cw_ref[j:j + 1, cols] * ext[r0:r0 + tm]
            y = term if y is None else y + term
        return _silu(y)

    def l2norm_heads(y, scale):
        parts = []
        for h in range(piece // GDN_DK):
            v = y[:, h * GDN_DK:(h + 1) * GDN_DK]
            inv = lax.rsqrt(jnp.sum(v * v, axis=-1, keepdims=True) + NORM_EPS)
            parts.append(v * (inv * scale) if scale != 1.0 else v * inv)
        return jnp.concatenate(parts, axis=1)

    def plain(out_ref, c0, scale=None):
        for j in range(out_ref.shape[1] // 512):
            acc = mm(c0 + j * 512, 512)
            if scale is not None:
                acc = acc * scale
            out_ref[:, j * 512:(j + 1) * 512] = acc.astype(out_ref.dtype)

    for half in range(2):
        c0 = half * piece
        gq_ref[:, c0:c0 + piece] = l2norm_heads(conv_silu(c0), GDN_DK ** -0.5).astype(BF16)
        if half == 0:
            plain(gz_ref, 1536)
        else:
            plain(dq_ref, 2048, DIFF_DH ** -0.5 * LOG2E)
    for half in range(2):
        c0 = half * piece
        gk_ref[:, c0:c0 + piece] = l2norm_heads(conv_silu(GDN_QK + c0), 1.0).astype(BF16)
        plain(dk_ref if half == 0 else dv_ref, 2560 + half * 512)
    for half in range(2):
        c0 = half * piece
        gv_ref[:, c0:c0 + piece] = conv_silu(2 * GDN_QK + c0).astype(BF16)
        if half == 0:
            plain(dz_ref, 3584)
    hab_ref[...] = _dot(xb_scr[...], wh_ref[:, PROJ_HEAD:PROJ_HEAD + LANES])
    plain(mab_ref, 4096)


def _project(x2, w_bf, w_tail, conv_w, seq):
    m = x2.shape[0]
    tm = PROJ_TM * PROJ_STEP_TILES
    row = lambda width: pl.BlockSpec((tm, width), lambda i: (i, 0))
    full = lambda a: pl.BlockSpec(a.shape, lambda i: (0, 0))
    out_shape = (
        jax.ShapeDtypeStruct((m, GDN_QK), BF16),
        jax.ShapeDtypeStruct((m, GDN_QK), BF16),
        jax.ShapeDtypeStruct((m, GDN_WIDTH), BF16),
        jax.ShapeDtypeStruct((m, LANES), F32),
        jax.ShapeDtypeStruct((m, 512), BF16),
        jax.ShapeDtypeStruct((m, 512), BF16),
        jax.ShapeDtypeStruct((m, 512), BF16),
        jax.ShapeDtypeStruct((m, 512), BF16),
        jax.ShapeDtypeStruct((m, 512), BF16),
        jax.ShapeDtypeStruct((m, 2048), BF16),
    )
    return pl.pallas_call(
        functools.partial(_proj_kernel, seq // PROJ_TM),
        grid=(m // tm,),
        in_specs=[row(D_MODEL),
                  pl.BlockSpec((D_MODEL, PROJ_HEAD + LANES), lambda i: (0, 0)),
                  full(w_tail), full(conv_w)],
        out_specs=tuple(row(s.shape[1]) for s in out_shape),
        out_shape=out_shape,
        scratch_shapes=[pltpu.VMEM((CONV_HIST, 3 * GDN_QK), F32),
                        pltpu.VMEM((PROJ_TM, D_MODEL), BF16)],
        compiler_params=pltpu.CompilerParams(
            dimension_semantics=("arbitrary",), vmem_limit_bytes=VMEM_LIMIT),
        name="in_proj",
    )(x2, w_bf, w_tail, conv_w)


def _gdn_kernel(gq_ref, gk_ref, gv_ref, hab_ref, gz_ref, alog_ref, dtb_ref, nw_ref, o_ref,
                st_scr):
    @pl.when(pl.program_id(1) == 0)
    def _():
        st_scr[...] = jnp.zeros_like(st_scr)

    for ck in range(GDN_STEP_CHUNKS):
        r = pl.ds(ck * GDN_T, GDN_T)
        prep = _gdn_prep(gq_ref.at[r], gk_ref.at[r], gv_ref.at[r], hab_ref.at[r], alog_ref,
                         dtb_ref)
        _gdn_apply(prep, gz_ref.at[r], nw_ref, o_ref.at[r], st_scr)


def _gdn_prep(gq_ref, gk_ref, gv_ref, hab_ref, alog_ref, dtb_ref):
    t = GDN_T
    heads = range(GDN_HEADS)

    hab = hab_ref[...]
    xg = hab + dtb_ref[...]
    y_sp = jnp.exp(-jnp.abs(xg))
    u_sp = 1.0 + y_sp
    softplus = jnp.maximum(xg, 0.0) + (jnp.log(u_sp) - ((u_sp - 1.0) - y_sp) / u_sp)
    g_full = -jnp.exp(alog_ref[...]) * softplus
    beta_full = _sigmoid(hab)

    ri = lax.broadcasted_iota(jnp.int32, (t, t), 0)
    ci = lax.broadcasted_iota(jnp.int32, (t, t), 1)
    causal = ri >= ci
    strict = ri > ci
    eye = jnp.where(ri == ci, 1.0, 0.0).astype(F32)

    tri = jnp.where(causal, 1.0, 0.0).astype(BF16)
    g_hi = g_full.astype(BF16)
    g_r1 = g_full - g_hi.astype(F32)
    g_mid = g_r1.astype(BF16)
    g_lo = (g_r1 - g_mid.astype(F32)).astype(BF16)
    gc = _dot(tri, g_hi) + _dot(tri, g_mid) + _dot(tri, g_lo)
    gc_t = gc.T
    eg = jnp.exp(gc)
    g_last = gc[t - 1:t, :]
    k_dec = jnp.exp(g_last - gc)
    eg_last = jnp.exp(g_last)

    qn_b = [gq_ref[:, h * GDN_DK:(h + 1) * GDN_DK] for h in heads]
    kn_b = [gk_ref[:, h * GDN_DK:(h + 1) * GDN_DK] for h in heads]
    qn = [qn_b[h].astype(F32) for h in heads]
    kn = [kn_b[h].astype(F32) for h in heads]
    vv = [gv_ref[:, h * GDN_DV:(h + 1) * GDN_DV].astype(F32) for h in heads]

    bcol = [beta_full[:, GDN_HEADS + h:GDN_HEADS + h + 1] for h in heads]
    egc = [eg[:, h:h + 1] for h in heads]
    kb = [kn[h] * bcol[h] for h in heads]
    kq = [_dot_nt(jnp.concatenate([kb[h].astype(BF16), qn_b[h]], axis=0), kn_b[h])
          for h in heads]
    decay = []
    for h in heads:
        gd = gc[:, h:h + 1] - gc_t[h:h + 1, :]
        decay.append(jnp.where(causal, jnp.exp(jnp.where(causal, gd, 0.0)), 0.0))
    nmat = [jnp.where(strict, -(kq[h][:t] * decay[h]), 0.0) for h in heads]
    amat = [(kq[h][t:] * decay[h]).astype(BF16) for h in heads]
    rhs = [jnp.concatenate([vv[h] * bcol[h], kb[h] * egc[h]], axis=1).astype(BF16)
           for h in heads]
    qe = [qn[h] * egc[h] for h in heads]
    kd_t = [(kn[h] * k_dec[:, h:h + 1]).T.astype(BF16) for h in heads]

    def same_block(b):
        sh = int(math.log2(b))
        return (ri >> sh) == (ci >> sh)

    blk = GDN_INV_BASE
    in_blk = same_block(blk)
    n0 = [jnp.where(in_blk, nmat[h], 0.0) for h in heads]
    x = [eye + n0[h] for h in heads]
    pw = [n0[h].astype(BF16) for h in heads]
    for _ in range(int(math.log2(blk)) - 1):
        p32 = [_dot(pw[h], pw[h]) for h in heads]
        pw = [p32[h].astype(BF16) for h in heads]
        x = [x[h] + _dot(x[h].astype(BF16), pw[h]) for h in heads]
    while blk < t:
        in_big = same_block(2 * blk)
        n_off = [jnp.where(in_big, jnp.where(in_blk, 0.0, nmat[h]), 0.0).astype(BF16)
                 for h in heads]
        xb = [x[h].astype(BF16) for h in heads]
        xn = [_dot(xb[h], n_off[h]).astype(BF16) for h in heads]
        x = [x[h] + _dot(xn[h], xb[h]) for h in heads]
        in_blk = in_big
        blk *= 2
    sol = [_dot(x[h].astype(BF16), rhs[h]) for h in heads]
    return sol, qe, amat, kd_t, eg_last


def _gdn_apply(prep, gz_ref, nw_ref, o_ref, st_scr):
    t = GDN_T
    heads = range(GDN_HEADS)
    sol, qe, amat, kd_t, eg_last = prep
    s_prev = [st_scr[h] for h in heads]
    ws = [_dot(jnp.concatenate([sol[h][:, GDN_DV:], qe[h]], axis=0).astype(BF16),
               s_prev[h].astype(BF16)) for h in heads]
    v_new = [(sol[h][:, :GDN_DV] - ws[h][:t]).astype(BF16) for h in heads]
    o_l = [ws[h][t:] + _dot(amat[h], v_new[h]) for h in heads]
    for h in heads:
        st_scr[h] = s_prev[h] * eg_last[:, h:h + 1] + _dot(kd_t[h], v_new[h])
    nw = nw_ref[...]
    for h in heads:
        o = o_l[h]
        o = o * lax.rsqrt(jnp.mean(o * o, axis=-1, keepdims=True) + NORM_EPS) * nw
        gate = _silu(gz_ref[:, h * GDN_DV:(h + 1) * GDN_DV].astype(F32))
        o_ref[:, h * GDN_DV:(h + 1) * GDN_DV] = (o * gate).astype(o_ref.dtype)


def _gdn(gq, gk, gv, hab, gz, alog_row, dtb_row, norm_w, batch, seq):
    t = GDN_T * GDN_STEP_CHUNKS
    nt = seq // t
    row = lambda width: pl.BlockSpec((t, width), lambda b, s: (b * nt + s, 0))
    full = lambda a: pl.BlockSpec(a.shape, lambda b, s: (0, 0))
    return pl.pallas_call(
        _gdn_kernel,
        grid=(batch, nt),
        in_specs=[row(GDN_QK), row(GDN_QK), row(GDN_WIDTH), row(LANES), row(GDN_WIDTH),
                  full(alog_row), full(dtb_row), full(norm_w)],
        out_specs=row(GDN_WIDTH),
        out_shape=jax.ShapeDtypeStruct((batch * seq, GDN_WIDTH), BF16),
        scratch_shapes=[pltpu.VMEM((GDN_HEADS, GDN_DK, GDN_DV), F32)],
        compiler_params=pltpu.CompilerParams(
            dimension_semantics=("arbitrary", "arbitrary"), vmem_limit_bytes=VMEM_LIMIT),
        name="gated_deltanet",
    )(gq, gk, gv, hab, gz, alog_row, dtb_row, norm_w)


def _attn_kernel(q_ref, qall_ref, k_ref, v_ref, dz_ref, nw_ref, lq1_ref, lk1_ref, lq2_ref,
                 lk2_ref, o_ref, ke_scr, vt_scr, m_scr, acc_scr, l_scr, p_scr, alpha_scr,
                 bound_scr, flag_smem):
    tq, tk, cols = ATT_TQ, ATT_TK, ATT_ROWS
    n_half = tq // cols
    seq = k_ref.shape[0]
    h = pl.program_id(1)
    i = pl.program_id(2)
    slope = jnp.float32(ALIBI_SLOPES[DIFF_HEADS - 1] * LOG2E)
    n_cap = jnp.int32(_fixed_ref_steps(DIFF_HEADS - 1))
    for hh in range(DIFF_HEADS - 1):
        slope = jnp.where(h == hh, jnp.float32(ALIBI_SLOPES[hh] * LOG2E), slope)
        n_cap = jnp.where(h == hh, jnp.int32(_fixed_ref_steps(hh)), n_cap)

    lane = lax.broadcasted_iota(jnp.int32, (tk, LANES), 1)
    feat_lane = (lane - DIFF_DH, lane)

    def half_norm_max(vf):
        hl = lax.broadcasted_iota(jnp.int32, (LANES, LANES), 0)
        hc = lax.broadcasted_iota(jnp.int32, (LANES, LANES), 1)
        pick = jnp.where(hc == jnp.where(hl < DIFF_DH, 0, 1), 1.0, 0.0).astype(BF16)
        sums = jnp.max(_dot((vf * vf).astype(BF16), pick), axis=0, keepdims=True)
        return jnp.sqrt(jnp.maximum(sums[:, 0:1], sums[:, 1:2])) * (1.0 + 2.0 ** -6)

    @pl.when(i == 0)
    def _():
        bound_scr[...] = jnp.zeros_like(bound_scr)

        def build(blk, carry):
            r0 = pl.multiple_of(blk * tk, tk)
            kf = k_ref[pl.ds(r0, tk), :].astype(F32)
            bound_scr[0:1, :] = jnp.maximum(bound_scr[0:1, :], half_norm_max(kf))
            bound_scr[1:2, :] = jnp.maximum(
                bound_scr[1:2, :], half_norm_max(qall_ref[pl.ds(r0, tk), :].astype(F32)))
            pos = (lax.broadcasted_iota(jnp.int32, (tk, LANES), 0) + r0).astype(F32) * slope
            p_hi = pos.astype(BF16).astype(F32)
            r1 = pos - p_hi
            p_mid = r1.astype(BF16).astype(F32)
            p_lo = r1 - p_mid
            for mp in range(2):
                fl = feat_lane[mp]
                feat = jnp.where(fl == 0, p_hi, jnp.where(fl == 1, p_mid, jnp.where(
                    fl == 2, p_lo, jnp.where((fl >= 3) & (fl < 6), 1.0, 0.0))))
                own = (lane < DIFF_DH) if mp == 0 else (lane >= DIFF_DH)
                ke_scr[mp, pl.ds(r0, tk), :] = jnp.where(own, kf, feat).astype(BF16)
            vt = v_ref[pl.ds(r0, tk), :].astype(F32).T
            vt_scr[blk] = vt.astype(BF16)
            return carry

        lax.fori_loop(0, seq // tk, build, 0, unroll=2)
        qk_all = bound_scr[0:1, :] * bound_scr[1:2, :]
        bound_scr[2:3, :] = qk_all
        flag_smem[0] = jnp.where(jnp.max(qk_all) <= ATT_FIXED_REF_MAX, 1, 0).astype(jnp.int32)

    qf = q_ref[...].astype(F32)
    qk_bound = bound_scr[2:3, 0:1]
    chains = [(mp, r) for mp in range(2) for r in range(n_half)]
    n_chains = len(chains)

    def query_maps(ref_terms):
        out = []
        for mp in range(2):
            fl = feat_lane[mp]
            own = (lane < DIFF_DH) if mp == 0 else (lane >= DIFF_DH)
            feat = jnp.where((fl >= 0) & (fl < 3), 1.0, 0.0)
            if ref_terms is not None:
                r_hi, r_mid, r_lo = ref_terms
                feat = jnp.where(fl == 3, r_hi, jnp.where(fl == 4, r_mid,
                                                          jnp.where(fl == 5, r_lo, feat)))
            out.append(jnp.where(own, qf, feat).astype(BF16))
        return out

    def scores_t(qmaps, c, j, masked):
        mp, r = chains[c]
        nk = (r + 1) * cols if masked else tk
        qc = qmaps[mp][r * cols:(r + 1) * cols]
        kb = ke_scr[mp, pl.ds(pl.multiple_of(j * tk, tk), nk), :]
        st = _dot_nt(kb, qc)
        if masked:
            key_l = lax.broadcasted_iota(jnp.int32, (nk, cols), 0)
            qry_l = lax.broadcasted_iota(jnp.int32, (nk, cols), 1)
            st = jnp.where(key_l <= qry_l + r * cols, st, MASK_NEG)
        return st

    def key_sums(p32):
        nk = p32.shape[0]
        return jnp.sum(p32.reshape(nk // 8, 8, cols), axis=0)

    def store_p(slot, c, p32):
        p = p32.astype(BF16)
        nk = p.shape[0]
        p_scr[slot, c, 0:nk, :] = p
        if nk < tk:
            p_scr[slot, c, nk:, :] = jnp.zeros((tk - nk, cols), BF16)

    def finish(acc_fin):
        def map_out(mp):
            parts = []
            for r in range(n_half):
                c = mp * n_half + r
                denom = jnp.sum(l_scr[c], axis=0, keepdims=True)
                parts.append(acc_fin[c] * (1.0 / denom))
            return jnp.concatenate(parts, axis=1)

        lam = (jnp.exp(jnp.sum(lq1_ref[...] * lk1_ref[...], axis=-1, keepdims=True))
               - jnp.exp(jnp.sum(lq2_ref[...] * lk2_ref[...], axis=-1, keepdims=True))
               + LAM_INIT)
        o_t = map_out(0) - lam * map_out(1)
        o_t = o_t * lax.rsqrt(jnp.mean(o_t * o_t, axis=0, keepdims=True) + SUBLN_EPS)
        o = o_t.T * (nw_ref[...] * (1.0 - LAM_INIT)) * _silu(dz_ref[...].astype(F32))
        o_ref[...] = o.astype(o_ref.dtype)

    def pipeline(scores_step, pv_value, steps_after_diag):
        def scores_block(j, slot, masked=False):
            for c in range(n_chains):
                scores_step(c, j, slot, masked)

        def pv_block(j, slot):
            for c in range(n_chains):
                acc_scr[c] = pv_value(c, j, slot)

        acc_scr[...] = jnp.zeros_like(acc_scr)
        l_scr[...] = jnp.zeros_like(l_scr)
        scores_block(i, 0, masked=True)
        n = steps_after_diag()
        last = i - n

        def pair(u, carry):
            j = i - 2 * u - 1
            scores_block(j, 1)
            pv_block(j + 1, 0)
            scores_block(j - 1, 0)
            pv_block(j, 1)
            return carry

        lax.fori_loop(0, n // 2, pair, 0)

        @pl.when(n % 2 == 1)
        def _():
            scores_block(last, 1)
            pv_block(last + 1, 0)
            finish([pv_value(c, last, 1) for c in range(n_chains)])

        @pl.when(n % 2 == 0)
        def _():
            finish([pv_value(c, last, 0) for c in range(n_chains)])

    use_fixed_ref = flag_smem[0] == 1

    @pl.when(use_fixed_ref)
    def _():
        pos_q = (lax.broadcasted_iota(jnp.int32, (tq, LANES), 0) + i * tq).astype(F32) * slope
        neg_ref = -(qk_bound + ATT_SKIP_SLACK + pos_q)
        r_hi = neg_ref.astype(BF16).astype(F32)
        r1 = neg_ref - r_hi
        r_mid = r1.astype(BF16).astype(F32)
        qmaps = query_maps((r_hi, r_mid, r1 - r_mid))

        def scores_step(c, j, slot, masked):
            p32 = jnp.exp2(scores_t(qmaps, c, j, masked))
            l_scr[c] = l_scr[c] + key_sums(p32)
            store_p(slot, c, p32)

        def pv_value(c, j, slot):
            return acc_scr[c] + _dot(vt_scr[j], p_scr[slot, c])

        pipeline(scores_step, pv_value, lambda: jnp.minimum(i, n_cap))

    @pl.when(jnp.logical_not(use_fixed_ref))
    def _():
        qmaps = query_maps(None)
        m_scr[...] = jnp.full(m_scr.shape, MASK_NEG, F32)

        def scores_step(c, j, slot, masked):
            st = scores_t(qmaps, c, j, masked)
            m_prev = m_scr[c]
            m_new = jnp.maximum(m_prev, jnp.max(st, axis=0, keepdims=True))
            p32 = jnp.exp2(st - m_new)
            alpha = jnp.exp2(m_prev - m_new)
            l_scr[c] = l_scr[c] * alpha + key_sums(p32)
            store_p(slot, c, p32)
            alpha_scr[slot, c] = alpha
            m_scr[c] = m_new

        def pv_value(c, j, slot):
            pv = _dot(vt_scr[j], p_scr[slot, c])
            return acc_scr[c] * alpha_scr[slot, c] + pv

        def steps_after_diag():
            m_min = jnp.min(jnp.min(m_scr[...], axis=0), axis=-1, keepdims=True)
            need = m_min - (ATT_SKIP_EXP2 + ATT_SKIP_SLACK) - qk_bound
            j_first = jnp.ceil(need / (slope * tk) - (tk - 1) / tk)
            n_arr = jnp.clip(i.astype(F32) - jnp.maximum(j_first, 0.0), 0.0, i.astype(F32))
            return jnp.max(n_arr).astype(jnp.int32)

        pipeline(scores_step, pv_value, steps_after_diag)


def _diff_attention(dq, dk, dv, dz, norm_w, lq1, lk1, lq2, lk2, batch, seq):
    tq, tk = ATT_TQ, ATT_TK
    nq = seq // tq
    n_chains = 2 * (tq // ATT_ROWS)
    qspec = pl.BlockSpec((tq, DIFF_DV), lambda b, h, i: (b * nq + i, h))
    kvspec = pl.BlockSpec((seq, DIFF_DV), lambda b, h, i: (b, h))
    full = lambda a: pl.BlockSpec(a.shape, lambda b, h, i: (0, 0))
    return pl.pallas_call(
        _attn_kernel,
        grid=(batch, DIFF_HEADS, nq),
        in_specs=[qspec, kvspec, kvspec, kvspec, qspec, full(norm_w), full(lq1), full(lk1),
                  full(lq2), full(lk2)],
        out_specs=qspec,
        out_shape=jax.ShapeDtypeStruct((batch * seq, DIFF_WIDTH), BF16),
        scratch_shapes=[pltpu.VMEM((2, seq, DIFF_DV), BF16),
                        pltpu.VMEM((seq // tk, DIFF_DV, tk), BF16),
                        pltpu.VMEM((n_chains, 1, ATT_ROWS), F32),
                        pltpu.VMEM((n_chains, DIFF_DV, ATT_ROWS), F32),
                        pltpu.VMEM((n_chains, 8, ATT_ROWS), F32),
                        pltpu.VMEM((2, n_chains, tk, ATT_ROWS), BF16),
                        pltpu.VMEM((2, n_chains, 1, ATT_ROWS), F32),
                        pltpu.VMEM((8, LANES), F32),
                        pltpu.SMEM((1,), jnp.int32)],
        compiler_params=pltpu.CompilerParams(
            dimension_semantics=("arbitrary", "arbitrary", "arbitrary"),
            vmem_limit_bytes=VMEM_LIMIT),
        name="diff_attention",
    )(dq, dq, dk, dv, dz, norm_w, lq1, lk1, lq2, lk2)


def _out_kernel(x_ref, oa_ref, ob_ref, mab_ref, wa_ref, wb_ref, wo_ref, g_ref, b_ref, y_ref,
                merged_scr):
    for tile in range(OUT_STEP_TILES):
        r = pl.ds(tile * OUT_TM, OUT_TM)
        _out_tile(x_ref.at[r], oa_ref.at[r], ob_ref.at[r], mab_ref.at[r], wa_ref, wb_ref,
                  wo_ref, g_ref, b_ref, y_ref.at[r], merged_scr)


def _out_tile(x_ref, oa_ref, ob_ref, mab_ref, wa_ref, wb_ref, wo_ref, g_ref, b_ref, y_ref,
              merged_scr):
    for c0 in range(0, D_MODEL, OUT_CHUNK):
        cs = slice(c0, c0 + OUT_CHUNK)
        ya = _dot(oa_ref[...], wa_ref[:, cs])
        yb = _dot(ob_ref[...], wb_ref[:, cs])
        ga = _sigmoid(mab_ref[:, c0:c0 + OUT_CHUNK].astype(F32))
        gb = _sigmoid(mab_ref[:, D_MODEL + c0:D_MODEL + c0 + OUT_CHUNK].astype(F32))
        merged_scr[:, cs] = (ga * ya + gb * yb).astype(BF16)
    r0 = 0
    for nrows in OUT_ROW_GROUPS:
        rs = slice(r0, r0 + nrows)
        r0 += nrows
        y = _dot(merged_scr[rs, :], wo_ref[...])
        z = DEEPNORM_ALPHA * x_ref[rs, :] + y
        mu = jnp.mean(z, axis=-1, keepdims=True)
        zc = z - mu
        var = jnp.mean(zc * zc, axis=-1, keepdims=True)
        y_ref[rs, :] = zc * lax.rsqrt(var + LN_EPS) * g_ref[...] + b_ref[...]


def _output(x2, o_a, o_b, mab, wa, wb, wo, ln_g, ln_b):
    m = x2.shape[0]
    tm = OUT_TM * OUT_STEP_TILES
    row = lambda width: pl.BlockSpec((tm, width), lambda i: (i, 0))
    full = lambda a: pl.BlockSpec(a.shape, lambda i: (0, 0))
    return pl.pallas_call(
        _out_kernel,
        grid=(m // tm,),
        in_specs=[row(D_MODEL), row(GDN_WIDTH), row(DIFF_WIDTH), row(2 * D_MODEL),
                  full(wa), full(wb), full(wo), full(ln_g), full(ln_b)],
        out_specs=row(D_MODEL),
        out_shape=jax.ShapeDtypeStruct((m, D_MODEL), F32),
        scratch_shapes=[pltpu.VMEM((OUT_TM, D_MODEL), BF16)],
        compiler_params=pltpu.CompilerParams(
            dimension_semantics=("arbitrary",), vmem_limit_bytes=VMEM_LIMIT),
        name="merge_out_ln",
    )(x2, o_a, o_b, mab, wa, wb, wo, ln_g, ln_b)


def _lane_row(v):
    return jnp.zeros((1, LANES), F32).at[0, :v.shape[0]].set(v.astype(F32))


def kernel(x, w_in, conv_w, a_log, dt_bias, gdn_norm_w, w_up_a, lambda_q1, lambda_k1,
           lambda_q2, lambda_k2, diff_norm_w, w_up_b, w_out, ln_g, ln_b):
    batch, seq, d = x.shape
    x2 = x.reshape(batch * seq, d)
    layer = 0
    w = w_in[layer].astype(BF16)
    w_tail = w[:, PROJ_HEAD + 2 * GDN_HEADS:]

    gq, gk, gv, hab, gz, dq, dk, dv, dz, mab = _project(
        x2, w, w_tail, conv_w[layer].astype(F32), seq)

    o_a = _gdn(gq, gk, gv, hab, gz, _lane_row(a_log[layer]), _lane_row(dt_bias[layer]),
               gdn_norm_w[layer].reshape(1, GDN_DV).astype(F32), batch, seq)

    o_b = _diff_attention(
        dq, dk, dv, dz, diff_norm_w[layer].reshape(1, DIFF_DV).astype(F32),
        lambda_q1[layer].reshape(1, DIFF_DH).astype(F32),
        lambda_k1[layer].reshape(1, DIFF_DH).astype(F32),
        lambda_q2[layer].reshape(1, DIFF_DH).astype(F32),
        lambda_k2[layer].reshape(1, DIFF_DH).astype(F32), batch, seq)

    y = _output(x2, o_a, o_b, mab, w_up_a[layer].astype(BF16), w_up_b[layer].astype(BF16),
                w_out[layer].astype(BF16), ln_g[layer].reshape(1, d).astype(F32),
                ln_b[layer].reshape(1, d).astype(F32))
    return y.reshape(batch, seq, d)
```

```python
import functools
import math

import jax
import jax.numpy as jnp
from jax import lax
from jax.experimental import pallas as pl
from jax.experimental.pallas import tpu as pltpu

F32 = jnp.float32
BF16 = jnp.bfloat16

D_MODEL = 1024
GDN_HEADS = 4
GDN_DK = 128
GDN_DV = 128
GDN_QK = GDN_HEADS * GDN_DK
GDN_WIDTH = GDN_HEADS * GDN_DV
CONV_K = 4
DIFF_HEADS = 4
DIFF_DH = 64
DIFF_DV = 2 * DIFF_DH
DIFF_QK = DIFF_HEADS * 2 * DIFF_DH
DIFF_WIDTH = DIFF_HEADS * DIFF_DV
NORM_EPS = 1e-6
SUBLN_EPS = 1e-5
LN_EPS = 1e-5
DEPTH = 1
DEEPNORM_ALPHA = (2.0 * DEPTH) ** 0.25
LAM_INIT = 0.8 - 0.6 * math.exp(-0.3 * 0)
ALIBI_SLOPES = tuple(2.0 ** (-8.0 * (i + 1) / DIFF_HEADS) for i in range(DIFF_HEADS))
LOG2E = math.log2(math.e)

LANES = 128
CONV_HIST = 8
VMEM_LIMIT = 56 * 1024 * 1024

PROJ_TM = 256
PROJ_STEP_TILES = 1
PROJ_HEAD = 2 * GDN_QK + GDN_WIDTH
GDN_T = 256
GDN_STEP_CHUNKS = 4
GDN_INV_BASE = 16
ATT_TQ = 512
ATT_TK = 512
ATT_ROWS = 256
OUT_TM = 512
OUT_STEP_TILES = 2
OUT_CHUNK = 256
OUT_ROW_GROUPS = (256, 256)
assert sum(OUT_ROW_GROUPS) == OUT_TM
MASK_NEG = -1e30
ATT_SKIP_EXP2 = 150.0
ATT_SKIP_SLACK = 1.0
ATT_FIXED_REF_MAX = 48.0


def _fixed_ref_steps(head):
    slope = ALIBI_SLOPES[head] * LOG2E
    return max(math.ceil(ATT_SKIP_EXP2 / (slope * ATT_TK) + (ATT_TK - 1) / ATT_TK) - 1, 0)


def _sigmoid(x):
    return 1.0 / (1.0 + jnp.exp(-x))


def _silu(x):
    return x * _sigmoid(x)


def _dot(a, b):
    return jnp.dot(a, b, preferred_element_type=F32)


def _dot_nt(a, b):
    return lax.dot_general(a, b, (((1,), (1,)), ((), ())), preferred_element_type=F32)


def _proj_kernel(tiles_per_seq, x_ref, wh_ref, wt_ref, cw_ref, *rest):
    out_refs, (hist_scr, xb_scr) = rest[:-2], rest[-2:]
    for tile in range(PROJ_STEP_TILES):
        r = pl.ds(tile * PROJ_TM, PROJ_TM)
        _proj_tile(pl.program_id(0) * PROJ_STEP_TILES + tile, tiles_per_seq, x_ref.at[r],
                   wh_ref, wt_ref, cw_ref, *[o.at[r] for o in out_refs], hist_scr, xb_scr)


def _proj_tile(tile_idx, tiles_per_seq, x_ref, wh_ref, wt_ref, cw_ref, gq_ref, gk_ref, gv_ref,
               hab_ref, gz_ref, dq_ref, dk_ref, dv_ref, dz_ref, mab_ref, hist_scr, xb_scr):
    tm = PROJ_TM

    @pl.when(tile_idx % tiles_per_seq == 0)
    def _():
        hist_scr[...] = jnp.zeros_like(hist_scr)

    xb_scr[...] = x_ref[...].astype(BF16)

    def mm(c0, width):
        if c0 < PROJ_HEAD:
            return _dot(xb_scr[...], wh_ref[:, c0:c0 + width])
        return _dot(xb_scr[...], wt_ref[:, c0 - PROJ_HEAD:c0 - PROJ_HEAD + width])

    piece = 2 * GDN_DK

    def conv_silu(c0):
        cols = slice(c0, c0 + piece)
        acc = mm(c0, piece)
        ext = jnp.concatenate([hist_scr[:, cols], acc], axis=0)
        hist_scr[:, cols] = acc[tm - CONV_HIST:]
        y = None
        for j in range(CONV_K):
            r0 = CONV_HIST - (CONV_K - 1) + j
            term = cw_ref[j:j + 1, cols] * ext[r0:r0 + tm]
            y = term if y is None else y + term
        return _silu(y)

    def l2norm_heads(y, scale):
        parts = []
        for h in range(piece // GDN_DK):
            v = y[:, h * GDN_DK:(h + 1) * GDN_DK]
            inv = lax.rsqrt(jnp.sum(v * v, axis=-1, keepdims=True) + NORM_EPS)
            parts.append(v * (inv * scale) if scale != 1.0 else v * inv)
        return jnp.concatenate(parts, axis=1)

    def plain(out_ref, c0, scale=None):
        for j in range(out_ref.shape[1] // 512):
            acc = mm(c0 + j * 512, 512)
            if scale is not None:
                acc = acc * scale
            out_ref[:, j * 512:(j + 1) * 512] = acc.astype(out_ref.dtype)

    for half in range(2):
        c0 = half * piece
        gq_ref[:, c0:c0 + piece] = l2norm_heads(conv_silu(c0), GDN_DK ** -0.5).astype(BF16)
        if half == 0:
            plain(gz_ref, 1536)
        else:
            plain(dq_ref, 2048, DIFF_DH ** -0.5 * LOG2E)
    for half in range(2):
        c0 = half * piece
        gk_ref[:, c0:c0 + piece] = l2norm_heads(conv_silu(GDN_QK + c0), 1.0).astype(BF16)
        plain(dk_ref if half == 0 else dv_ref, 2560 + half * 512)
    for half in range(2):
        c0 = half * piece
        gv_ref[:, c0:c0 + piece] = conv_silu(2 * GDN_QK + c0).astype(BF16)
        if half == 0:
            plain(dz_ref, 3584)
    hab_ref[...] = _dot(xb_scr[...], wh_ref[:, PROJ_HEAD:PROJ_HEAD + LANES])
    plain(mab_ref, 4096)


def _project(x2, w_bf, w_tail, conv_w, seq):
    m = x2.shape[0]
    tm = PROJ_TM * PROJ_STEP_TILES
    row = lambda width: pl.BlockSpec((tm, width), lambda i: (i, 0))
    full = lambda a: pl.BlockSpec(a.shape, lambda i: (0, 0))
    out_shape = (
        jax.ShapeDtypeStruct((m, GDN_QK), BF16),
        jax.ShapeDtypeStruct((m, GDN_QK), BF16),
        jax.ShapeDtypeStruct((m, GDN_WIDTH), BF16),
        jax.ShapeDtypeStruct((m, LANES), F32),
        jax.ShapeDtypeStruct((m, 512), BF16),
        jax.ShapeDtypeStruct((m, 512), BF16),
        jax.ShapeDtypeStruct((m, 512), BF16),
        jax.ShapeDtypeStruct((m, 512), BF16),
        jax.ShapeDtypeStruct((m, 512), BF16),
        jax.ShapeDtypeStruct((m, 2048), BF16),
    )
    return pl.pallas_call(
        functools.partial(_proj_kernel, seq // PROJ_TM),
        grid=(m // tm,),
        in_specs=[row(D_MODEL),
                  pl.BlockSpec((D_MODEL, PROJ_HEAD + LANES), lambda i: (0, 0)),
                  full(w_tail), full(conv_w)],
        out_specs=tuple(row(s.shape[1]) for s in out_shape),
        out_shape=out_shape,
        scratch_shapes=[pltpu.VMEM((CONV_HIST, 3 * GDN_QK), F32),
                        pltpu.VMEM((PROJ_TM, D_MODEL), BF16)],
        compiler_params=pltpu.CompilerParams(
            dimension_semantics=("arbitrary",), vmem_limit_bytes=VMEM_LIMIT),
        name="in_proj",
    )(x2, w_bf, w_tail, conv_w)


def _gdn_kernel(gq_ref, gk_ref, gv_ref, hab_ref, gz_ref, alog_ref, dtb_ref, nw_ref, o_ref,
                st_scr):
    @pl.when(pl.program_id(1) == 0)
    def _():
        st_scr[...] = jnp.zeros_like(st_scr)

    for ck in range(GDN_STEP_CHUNKS):
        r = pl.ds(ck * GDN_T, GDN_T)
        prep = _gdn_prep(gq_ref.at[r], gk_ref.at[r], gv_ref.at[r], hab_ref.at[r], alog_ref,
                         dtb_ref)
        _gdn_apply(prep, gz_ref.at[r], nw_ref, o_ref.at[r], st_scr)


def _gdn_prep(gq_ref, gk_ref, gv_ref, hab_ref, alog_ref, dtb_ref):
    t = GDN_T
    heads = range(GDN_HEADS)

    hab = hab_ref[...]
    xg = hab + dtb_ref[...]
    y_sp = jnp.exp(-jnp.abs(xg))
    u_sp = 1.0 + y_sp
    softplus = jnp.maximum(xg, 0.0) + (jnp.log(u_sp) - ((u_sp - 1.0) - y_sp) / u_sp)
    g_full = -jnp.exp(alog_ref[...]) * softplus
    beta_full = _sigmoid(hab)

    ri = lax.broadcasted_iota(jnp.int32, (t, t), 0)
    ci = lax.broadcasted_iota(jnp.int32, (t, t), 1)
    causal = ri >= ci
    strict = ri > ci
    eye = jnp.where(ri == ci, 1.0, 0.0).astype(F32)

    tri = jnp.where(causal, 1.0, 0.0).astype(BF16)
    g_hi = g_full.astype(BF16)
    g_r1 = g_full - g_hi.astype(F32)
    g_mid = g_r1.astype(BF16)
    g_lo = (g_r1 - g_mid.astype(F32)).astype(BF16)
    gc = _dot(tri, g_hi) + _dot(tri, g_mid) + _dot(tri, g_lo)
    gc_t = gc.T
    eg = jnp.exp(gc)
    g_last = gc[t - 1:t, :]
    k_dec = jnp.exp(g_last - gc)
    eg_last = jnp.exp(g_last)

    qn_b = [gq_ref[:, h * GDN_DK:(h + 1) * GDN_DK] for h in heads]
    kn_b = [gk_ref[:, h * GDN_DK:(h + 1) * GDN_DK] for h in heads]
    qn = [qn_b[h].astype(F32) for h in heads]
    kn = [kn_b[h].astype(F32) for h in heads]
    vv = [gv_ref[:, h * GDN_DV:(h + 1) * GDN_DV].astype(F32) for h in heads]

    bcol = [beta_full[:, GDN_HEADS + h:GDN_HEADS + h + 1] for h in heads]
    egc = [eg[:, h:h + 1] for h in heads]
    kb = [kn[h] * bcol[h] for h in heads]
    kq = [_dot_nt(jnp.concatenate([kb[h].astype(BF16), qn_b[h]], axis=0), kn_b[h])
          for h in heads]
    decay = []
    for h in heads:
        gd = gc[:, h:h + 1] - gc_t[h:h + 1, :]
        decay.append(jnp.where(causal, jnp.exp(jnp.where(causal, gd, 0.0)), 0.0))
    nmat = [jnp.where(strict, -(kq[h][:t] * decay[h]), 0.0) for h in heads]
    amat = [(kq[h][t:] * decay[h]).astype(BF16) for h in heads]
    rhs = [jnp.concatenate([vv[h] * bcol[h], kb[h] * egc[h]], axis=1).astype(BF16)
           for h in heads]
    qe = [qn[h] * egc[h] for h in heads]
    kd_t = [(kn[h] * k_dec[:, h:h + 1]).T.astype(BF16) for h in heads]

    def same_block(b):
        sh = int(math.log2(b))
        return (ri >> sh) == (ci >> sh)

    blk = GDN_INV_BASE
    in_blk = same_block(blk)
    n0 = [jnp.where(in_blk, nmat[h], 0.0) for h in heads]
    x = [eye + n0[h] for h in heads]
    pw = [n0[h].astype(BF16) for h in heads]
    for _ in range(int(math.log2(blk)) - 1):
        p32 = [_dot(pw[h], pw[h]) for h in heads]
        pw = [p32[h].astype(BF16) for h in heads]
        x = [x[h] + _dot(x[h].astype(BF16), pw[h]) for h in heads]
    while blk < t:
        in_big = same_block(2 * blk)
        n_off = [jnp.where(in_big, jnp.where(in_blk, 0.0, nmat[h]), 0.0).astype(BF16)
                 for h in heads]
        xb = [x[h].astype(BF16) for h in heads]
        xn = [_dot(xb[h], n_off[h]).astype(BF16) for h in heads]
        x = [x[h] + _dot(xn[h], xb[h]) for h in heads]
        in_blk = in_big
        blk *= 2
    sol = [_dot(x[h].astype(BF16), rhs[h]) for h in heads]
    return sol, qe, amat, kd_t, eg_last


def _gdn_apply(prep, gz_ref, nw_ref, o_ref, st_scr):
    t = GDN_T
    heads = range(GDN_HEADS)
    sol, qe, amat, kd_t, eg_last = prep
    s_prev = [st_scr[h] for h in heads]
    ws = [_dot(jnp.concatenate([sol[h][:, GDN_DV:], qe[h]], axis=0).astype(BF16),
               s_prev[h].astype(BF16)) for h in heads]
    v_new = [(sol[h][:, :GDN_DV] - ws[h][:t]).astype(BF16) for h in heads]
    o_l = [ws[h][t:] + _dot(amat[h], v_new[h]) for h in heads]
    for h in heads:
        st_scr[h] = s_prev[h] * eg_last[:, h:h + 1] + _dot(kd_t[h], v_new[h])
    nw = nw_ref[...]
    for h in heads:
        o = o_l[h]
        o = o * lax.rsqrt(jnp.mean(o * o, axis=-1, keepdims=True) + NORM_EPS) * nw
        gate = _silu(gz_ref[:, h * GDN_DV:(h + 1) * GDN_DV].astype(F32))
        o_ref[:, h * GDN_DV:(h + 1) * GDN_DV] = (o * gate).astype(o_ref.dtype)


def _gdn(gq, gk, gv, hab, gz, alog_row, dtb_row, norm_w, batch, seq):
    t = GDN_T * GDN_STEP_CHUNKS
    nt = seq // t
    row = lambda width: pl.BlockSpec((t, width), lambda b, s: (b * nt + s, 0))
    full = lambda a: pl.BlockSpec(a.shape, lambda b, s: (0, 0))
    return pl.pallas_call(
        _gdn_kernel,
        grid=(batch, nt),
        in_specs=[row(GDN_QK), row(GDN_QK), row(GDN_WIDTH), row(LANES), row(GDN_WIDTH),
                  full(alog_row), full(dtb_row), full(norm_w)],
        out_specs=row(GDN_WIDTH),
        out_shape=jax.ShapeDtypeStruct((batch * seq, GDN_WIDTH), BF16),
        scratch_shapes=[pltpu.VMEM((GDN_HEADS, GDN_DK, GDN_DV), F32)],
        compiler_params=pltpu.CompilerParams(
            dimension_semantics=("arbitrary", "arbitrary"), vmem_limit_bytes=VMEM_LIMIT),
        name="gated_deltanet",
    )(gq, gk, gv, hab, gz, alog_row, dtb_row, norm_w)


def _attn_kernel(q_ref, qall_ref, k_ref, v_ref, dz_ref, nw_ref, lq1_ref, lk1_ref, lq2_ref,
                 lk2_ref, o_ref, ke_scr, vt_scr, m_scr, acc_scr, l_scr, p_scr, alpha_scr,
                 bound_scr, flag_smem):
    tq, tk, cols = ATT_TQ, ATT_TK, ATT_ROWS
    n_half = tq // cols
    seq = k_ref.shape[0]
    h = pl.program_id(1)
    i = pl.program_id(2)
    slope = jnp.float32(ALIBI_SLOPES[DIFF_HEADS - 1] * LOG2E)
    n_cap = jnp.int32(_fixed_ref_steps(DIFF_HEADS - 1))
    for hh in range(DIFF_HEADS - 1):
        slope = jnp.where(h == hh, jnp.float32(ALIBI_SLOPES[hh] * LOG2E), slope)
        n_cap = jnp.where(h == hh, jnp.int32(_fixed_ref_steps(hh)), n_cap)

    lane = lax.broadcasted_iota(jnp.int32, (tk, LANES), 1)
    feat_lane = (lane - DIFF_DH, lane)

    def half_norm_max(vf):
        hl = lax.broadcasted_iota(jnp.int32, (LANES, LANES), 0)
        hc = lax.broadcasted_iota(jnp.int32, (LANES, LANES), 1)
        pick = jnp.where(hc == jnp.where(hl < DIFF_DH, 0, 1), 1.0, 0.0).astype(BF16)
        sums = jnp.max(_dot((vf * vf).astype(BF16), pick), axis=0, keepdims=True)
        return jnp.sqrt(jnp.maximum(sums[:, 0:1], sums[:, 1:2])) * (1.0 + 2.0 ** -6)

    @pl.when(i == 0)
    def _():
        bound_scr[...] = jnp.zeros_like(bound_scr)

        def build(blk, carry):
            r0 = pl.multiple_of(blk * tk, tk)
            kf = k_ref[pl.ds(r0, tk), :].astype(F32)
            bound_scr[0:1, :] = jnp.maximum(bound_scr[0:1, :], half_norm_max(kf))
            bound_scr[1:2, :] = jnp.maximum(
                bound_scr[1:2, :], half_norm_max(qall_ref[pl.ds(r0, tk), :].astype(F32)))
            pos = (lax.broadcasted_iota(jnp.int32, (tk, LANES), 0) + r0).astype(F32) * slope
            p_hi = pos.astype(BF16).astype(F32)
            r1 = pos - p_hi
            p_mid = r1.astype(BF16).astype(F32)
            p_lo = r1 - p_mid
            for mp in range(2):
                fl = feat_lane[mp]
                feat = jnp.where(fl == 0, p_hi, jnp.where(fl == 1, p_mid, jnp.where(
                    fl == 2, p_lo, jnp.where((fl >= 3) & (fl < 6), 1.0, 0.0))))
                own = (lane < DIFF_DH) if mp == 0 else (lane >= DIFF_DH)
                ke_scr[mp, pl.ds(r0, tk), :] = jnp.where(own, kf, feat).astype(BF16)
            vt = v_ref[pl.ds(r0, tk), :].astype(F32).T
            vt_scr[blk] = vt.astype(BF16)
            return carry

        lax.fori_loop(0, seq // tk, build, 0, unroll=2)
        qk_all = bound_scr[0:1, :] * bound_scr[1:2, :]
        bound_scr[2:3, :] = qk_all
        flag_smem[0] = jnp.where(jnp.max(qk_all) <= ATT_FIXED_REF_MAX, 1, 0).astype(jnp.int32)

    qf = q_ref[...].astype(F32)
    qk_bound = bound_scr[2:3, 0:1]
    chains = [(mp, r) for mp in range(2) for r in range(n_half)]
    n_chains = len(chains)

    def query_maps(ref_terms):
        out = []
        for mp in range(2):
            fl = feat_lane[mp]
            own = (lane < DIFF_DH) if mp == 0 else (lane >= DIFF_DH)
            feat = jnp.where((fl >= 0) & (fl < 3), 1.0, 0.0)
            if ref_terms is not None:
                r_hi, r_mid, r_lo = ref_terms
                feat = jnp.where(fl == 3, r_hi, jnp.where(fl == 4, r_mid,
                                                          jnp.where(fl == 5, r_lo, feat)))
            out.append(jnp.where(own, qf, feat).astype(BF16))
        return out

    def scores_t(qmaps, c, j, masked):
        mp, r = chains[c]
        nk = (r + 1) * cols if masked else tk
        qc = qmaps[mp][r * cols:(r + 1) * cols]
        kb = ke_scr[mp, pl.ds(pl.multiple_of(j * tk, tk), nk), :]
        st = _dot_nt(kb, qc)
        if masked:
            key_l = lax.broadcasted_iota(jnp.int32, (nk, cols), 0)
            qry_l = lax.broadcasted_iota(jnp.int32, (nk, cols), 1)
            st = jnp.where(key_l <= qry_l + r * cols, st, MASK_NEG)
        return st

    def key_sums(p32):
        nk = p32.shape[0]
        return jnp.sum(p32.reshape(nk // 8, 8, cols), axis=0)

    def store_p(slot, c, p32):
        p = p32.astype(BF16)
        nk = p.shape[0]
        p_scr[slot, c, 0:nk, :] = p
        if nk < tk:
            p_scr[slot, c, nk:, :] = jnp.zeros((tk - nk, cols), BF16)

    def finish(acc_fin):
        def map_out(mp):
            parts = []
            for r in range(n_half):
                c = mp * n_half + r
                denom = jnp.sum(l_scr[c], axis=0, keepdims=True)
                parts.append(acc_fin[c] * (1.0 / denom))
            return jnp.concatenate(parts, axis=1)

        lam = (jnp.exp(jnp.sum(lq1_ref[...] * lk1_ref[...], axis=-1, keepdims=True))
               - jnp.exp(jnp.sum(lq2_ref[...] * lk2_ref[...], axis=-1, keepdims=True))
               + LAM_INIT)
        o_t = map_out(0) - lam * map_out(1)
        o_t = o_t * lax.rsqrt(jnp.mean(o_t * o_t, axis=0, keepdims=True) + SUBLN_EPS)
        o = o_t.T * (nw_ref[...] * (1.0 - LAM_INIT)) * _silu(dz_ref[...].astype(F32))
        o_ref[...] = o.astype(o_ref.dtype)

    def pipeline(scores_step, pv_value, steps_after_diag):
        def scores_block(j, slot, masked=False):
            for c in range(n_chains):
                scores_step(c, j, slot, masked)

        def pv_block(j, slot):
            for c in range(n_chains):
                acc_scr[c] = pv_value(c, j, slot)

        acc_scr[...] = jnp.zeros_like(acc_scr)
        l_scr[...] = jnp.zeros_like(l_scr)
        scores_block(i, 0, masked=True)
        n = steps_after_diag()
        last = i - n

        def pair(u, carry):
            j = i - 2 * u - 1
            scores_block(j, 1)
            pv_block(j + 1, 0)
            scores_block(j - 1, 0)
            pv_block(j, 1)
            return carry

        lax.fori_loop(0, n // 2, pair, 0)

        @pl.when(n % 2 == 1)
        def _():
            scores_block(last, 1)
            pv_block(last + 1, 0)
            finish([pv_value(c, last, 1) for c in range(n_chains)])

        @pl.when(n % 2 == 0)
        def _():
            finish([pv_value(c, last, 0) for c in range(n_chains)])

    use_fixed_ref = flag_smem[0] == 1

    @pl.when(use_fixed_ref)
    def _():
        pos_q = (lax.broadcasted_iota(jnp.int32, (tq, LANES), 0) + i * tq).astype(F32) * slope
        neg_ref = -(qk_bound + ATT_SKIP_SLACK + pos_q)
        r_hi = neg_ref.astype(BF16).astype(F32)
        r1 = neg_ref - r_hi
        r_mid = r1.astype(BF16).astype(F32)
        qmaps = query_maps((r_hi, r_mid, r1 - r_mid))

        def scores_step(c, j, slot, masked):
            p32 = jnp.exp2(scores_t(qmaps, c, j, masked))
            l_scr[c] = l_scr[c] + key_sums(p32)
            store_p(slot, c, p32)

        def pv_value(c, j, slot):
            return acc_scr[c] + _dot(vt_scr[j], p_scr[slot, c])

        pipeline(scores_step, pv_value, lambda: jnp.minimum(i, n_cap))

    @pl.when(jnp.logical_not(use_fixed_ref))
    def _():
        qmaps = query_maps(None)
        m_scr[...] = jnp.full(m_scr.shape, MASK_NEG, F32)

        def scores_step(c, j, slot, masked):
            st = scores_t(qmaps, c, j, masked)
            m_prev = m_scr[c]
            m_new = jnp.maximum(m_prev, jnp.max(st, axis=0, keepdims=True))
            p32 = jnp.exp2(st - m_new)
            alpha = jnp.exp2(m_prev - m_new)
            l_scr[c] = l_scr[c] * alpha + key_sums(p32)
            store_p(slot, c, p32)
            alpha_scr[slot, c] = alpha
            m_scr[c] = m_new

        def pv_value(c, j, slot):
            pv = _dot(vt_scr[j], p_scr[slot, c])
            return acc_scr[c] * alpha_scr[slot, c] + pv

        def steps_after_diag():
            m_min = jnp.min(jnp.min(m_scr[...], axis=0), axis=-1, keepdims=True)
            need = m_min - (ATT_SKIP_EXP2 + ATT_SKIP_SLACK) - qk_bound
            j_first = jnp.ceil(need / (slope * tk) - (tk - 1) / tk)
            n_arr = jnp.clip(i.astype(F32) - jnp.maximum(j_first, 0.0), 0.0, i.astype(F32))
            return jnp.max(n_arr).astype(jnp.int32)

        pipeline(scores_step, pv_value, steps_after_diag)


def _diff_attention(dq, dk, dv, dz, norm_w, lq1, lk1, lq2, lk2, batch, seq):
    tq, tk = ATT_TQ, ATT_TK
    nq = seq // tq
    n_chains = 2 * (tq // ATT_ROWS)
    qspec = pl.BlockSpec((tq, DIFF_DV), lambda b, h, i: (b * nq + i, h))
    kvspec = pl.BlockSpec((seq, DIFF_DV), lambda b, h, i: (b, h))
    full = lambda a: pl.BlockSpec(a.shape, lambda b, h, i: (0, 0))
    return pl.pallas_call(
        _attn_kernel,
        grid=(batch, DIFF_HEADS, nq),
        in_specs=[qspec, kvspec, kvspec, kvspec, qspec, full(norm_w), full(lq1), full(lk1),
                  full(lq2), full(lk2)],
        out_specs=qspec,
        out_shape=jax.ShapeDtypeStruct((batch * seq, DIFF_WIDTH), BF16),
        scratch_shapes=[pltpu.VMEM((2, seq, DIFF_DV), BF16),
                        pltpu.VMEM((seq // tk, DIFF_DV, tk), BF16),
                        pltpu.VMEM((n_chains, 1, ATT_ROWS), F32),
                        pltpu.VMEM((n_chains, DIFF_DV, ATT_ROWS), F32),
                        pltpu.VMEM((n_chains, 8, ATT_ROWS), F32),
                        pltpu.VMEM((2, n_chains, tk, ATT_ROWS), BF16),
                        pltpu.VMEM((2, n_chains, 1, ATT_ROWS), F32),
                        pltpu.VMEM((8, LANES), F32),
                        pltpu.SMEM((1,), jnp.int32)],
        compiler_params=pltpu.CompilerParams(
            dimension_semantics=("arbitrary", "arbitrary", "arbitrary"),
            vmem_limit_bytes=VMEM_LIMIT),
        name="diff_attention",
    )(dq, dq, dk, dv, dz, norm_w, lq1, lk1, lq2, lk2)


def _out_kernel(x_ref, oa_ref, ob_ref, mab_ref, wa_ref, wb_ref, wo_ref, g_ref, b_ref, y_ref,
                merged_scr):
    for tile in range(OUT_STEP_TILES):
        r = pl.ds(tile * OUT_TM, OUT_TM)
        _out_tile(x_ref.at[r], oa_ref.at[r], ob_ref.at[r], mab_ref.at[r], wa_ref, wb_ref,
                  wo_ref, g_ref, b_ref, y_ref.at[r], merged_scr)


def _out_tile(x_ref, oa_ref, ob_ref, mab_ref, wa_ref, wb_ref, wo_ref, g_ref, b_ref, y_ref,
              merged_scr):
    for c0 in range(0, D_MODEL, OUT_CHUNK):
        cs = slice(c0, c0 + OUT_CHUNK)
        ya = _dot(oa_ref[...], wa_ref[:, cs])
        yb = _dot(ob_ref[...], wb_ref[:, cs])
        ga = _sigmoid(mab_ref[:, c0:c0 + OUT_CHUNK].astype(F32))
        gb = _sigmoid(mab_ref[:, D_MODEL + c0:D_MODEL + c0 + OUT_CHUNK].astype(F32))
        merged_scr[:, cs] = (ga * ya + gb * yb).astype(BF16)
    r0 = 0
    for nrows in OUT_ROW_GROUPS:
        rs = slice(r0, r0 + nrows)
        r0 += nrows
        y = _dot(merged_scr[rs, :], wo_ref[...])
        z = DEEPNORM_ALPHA * x_ref[rs, :] + y
        mu = jnp.mean(z, axis=-1, keepdims=True)
        zc = z - mu
        var = jnp.mean(zc * zc, axis=-1, keepdims=True)
        y_ref[rs, :] = zc * lax.rsqrt(var + LN_EPS) * g_ref[...] + b_ref[...]


def _output(x2, o_a, o_b, mab, wa, wb, wo, ln_g, ln_b):
    m = x2.shape[0]
    tm = OUT_TM * OUT_STEP_TILES
    row = lambda width: pl.BlockSpec((tm, width), lambda i: (i, 0))
    full = lambda a: pl.BlockSpec(a.shape, lambda i: (0, 0))
    return pl.pallas_call(
        _out_kernel,
        grid=(m // tm,),
        in_specs=[row(D_MODEL), row(GDN_WIDTH), row(DIFF_WIDTH), row(2 * D_MODEL),
                  full(wa), full(wb), full(wo), full(ln_g), full(ln_b)],
        out_specs=row(D_MODEL),
        out_shape=jax.ShapeDtypeStruct((m, D_MODEL), F32),
        scratch_shapes=[pltpu.VMEM((OUT_TM, D_MODEL), BF16)],
        compiler_params=pltpu.CompilerParams(
            dimension_semantics=("arbitrary",), vmem_limit_bytes=VMEM_LIMIT),
        name="merge_out_ln",
    )(x2, o_a, o_b, mab, wa, wb, wo, ln_g, ln_b)


def _lane_row(v):
    return jnp.zeros((1, LANES), F32).at[0, :v.shape[0]].set(v.astype(F32))


def kernel(x, w_in, conv_w, a_log, dt_bias, gdn_norm_w, w_up_a, lambda_q1, lambda_k1,
           lambda_q2, lambda_k2, diff_norm_w, w_up_b, w_out, ln_g, ln_b):
    batch, seq, d = x.shape
    x2 = x.reshape(batch * seq, d)
    layer = 0
    w = w_in[layer].astype(BF16)
    w_tail = w[:, PROJ_HEAD + 2 * GDN_HEADS:]

    gq, gk, gv, hab, gz, dq, dk, dv, dz, mab = _project(
        x2, w, w_tail, conv_w[layer].astype(F32), seq)

    o_a = _gdn(gq, gk, gv, hab, gz, _lane_row(a_log[layer]), _lane_row(dt_bias[layer]),
               gdn_norm_w[layer].reshape(1, GDN_DV).astype(F32), batch, seq)

    o_b = _diff_attention(
        dq, dk, dv, dz, diff_norm_w[layer].reshape(1, DIFF_DV).astype(F32),
        lambda_q1[layer].reshape(1, DIFF_DH).astype(F32),
        lambda_k1[layer].reshape(1, DIFF_DH).astype(F32),
        lambda_q2[layer].reshape(1, DIFF_DH).astype(F32),
        lambda_k2[layer].reshape(1, DIFF_DH).astype(F32), batch, seq)

    y = _output(x2, o_a, o_b, mab, w_up_a[layer].astype(BF16), w_up_b[layer].astype(BF16),
                w_out[layer].astype(BF16), ln_g[layer].reshape(1, d).astype(F32),
                ln_b[layer].reshape(1, d).astype(F32))
    return y.reshape(batch, seq, d)
```

```python
import functools
import math

import jax
import jax.numpy as jnp
from jax import lax
from jax.experimental import pallas as pl
from jax.experimental.pallas import tpu as pltpu

F32 = jnp.float32
BF16 = jnp.bfloat16

D_MODEL = 1024
GDN_HEADS = 4
GDN_DK = 128
GDN_DV = 128
GDN_QK = GDN_HEADS * GDN_DK
GDN_WIDTH = GDN_HEADS * GDN_DV
CONV_K = 4
DIFF_HEADS = 4
DIFF_DH = 64
DIFF_DV = 2 * DIFF_DH
DIFF_QK = DIFF_HEADS * 2 * DIFF_DH
DIFF_WIDTH = DIFF_HEADS * DIFF_DV
NORM_EPS = 1e-6
SUBLN_EPS = 1e-5
LN_EPS = 1e-5
DEPTH = 1
DEEPNORM_ALPHA = (2.0 * DEPTH) ** 0.25
LAM_INIT = 0.8 - 0.6 * math.exp(-0.3 * 0)
ALIBI_SLOPES = tuple(2.0 ** (-8.0 * (i + 1) / DIFF_HEADS) for i in range(DIFF_HEADS))
LOG2E = math.log2(math.e)

LANES = 128
CONV_HIST = 8
VMEM_LIMIT = 56 * 1024 * 1024

PROJ_TM = 256
PROJ_STEP_TILES = 1
PROJ_HEAD = 2 * GDN_QK + GDN_WIDTH
GDN_T = 256
GDN_STEP_CHUNKS = 4
GDN_INV_BASE = 16
ATT_TQ = 512
ATT_TK = 512
ATT_ROWS = 256
OUT_TM = 512
OUT_STEP_TILES = 2
OUT_CHUNK = 256
OUT_ROW_GROUPS = (256, 256)
assert sum(OUT_ROW_GROUPS) == OUT_TM
MASK_NEG = -1e30
ATT_SKIP_EXP2 = 150.0
ATT_SKIP_SLACK = 1.0
ATT_FIXED_REF_MAX = 48.0


def _fixed_ref_steps(head):
    slope = ALIBI_SLOPES[head] * LOG2E
    return max(math.ceil(ATT_SKIP_EXP2 / (slope * ATT_TK) + (ATT_TK - 1) / ATT_TK) - 1, 0)


def _sigmoid(x):
    return 1.0 / (1.0 + jnp.exp(-x))


def _silu(x):
    return x * _sigmoid(x)


def _dot(a, b):
    return jnp.dot(a, b, preferred_element_type=F32)


def _dot_nt(a, b):
    return lax.dot_general(a, b, (((1,), (1,)), ((), ())), preferred_element_type=F32)


def _proj_kernel(tiles_per_seq, x_ref, wh_ref, wt_ref, cw_ref, *rest):
    out_refs, (hist_scr, xb_scr) = rest[:-2], rest[-2:]
    for tile in range(PROJ_STEP_TILES):
        r = pl.ds(tile * PROJ_TM, PROJ_TM)
        _proj_tile(pl.program_id(0) * PROJ_STEP_TILES + tile, tiles_per_seq, x_ref.at[r],
                   wh_ref, wt_ref, cw_ref, *[o.at[r] for o in out_refs], hist_scr, xb_scr)


def _proj_tile(tile_idx, tiles_per_seq, x_ref, wh_ref, wt_ref, cw_ref, gq_ref, gk_ref, gv_ref,
               hab_ref, gz_ref, dq_ref, dk_ref, dv_ref, dz_ref, mab_ref, hist_scr, xb_scr):
    tm = PROJ_TM

    @pl.when(tile_idx % tiles_per_seq == 0)
    def _():
        hist_scr[...] = jnp.zeros_like(hist_scr)

    xb_scr[...] = x_ref[...].astype(BF16)

    def mm(c0, width):
        if c0 < PROJ_HEAD:
            return _dot(xb_scr[...], wh_ref[:, c0:c0 + width])
        return _dot(xb_scr[...], wt_ref[:, c0 - PROJ_HEAD:c0 - PROJ_HEAD + width])

    piece = 2 * GDN_DK

    def conv_silu(c0):
        cols = slice(c0, c0 + piece)
        acc = mm(c0, piece)
        ext = jnp.concatenate([hist_scr[:, cols], acc], axis=0)
        hist_scr[:, cols] = acc[tm - CONV_HIST:]
        y = None
        for j in range(CONV_K):
            r0 = CONV_HIST - (CONV_K - 1) + j
            term = cw_ref[j:j + 1, cols] * ext[r0:r0 + tm]
            y = term if y is None else y + term
        return _silu(y)

    def l2norm_heads(y, scale):
        parts = []
        for h in range(piece // GDN_DK):
            v = y[:, h * GDN_DK:(h + 1) * GDN_DK]
            inv = lax.rsqrt(jnp.sum(v * v, axis=-1, keepdims=True) + NORM_EPS)
            parts.append(v * (inv * scale) if scale != 1.0 else v * inv)
        return jnp.concatenate(parts, axis=1)

    def plain(out_ref, c0, scale=None):
        for j in range(out_ref.shape[1] // 512):
            acc = mm(c0 + j * 512, 512)
            if scale is not None:
                acc = acc * scale
            out_ref[:, j * 512:(j + 1) * 512] = acc.astype(out_ref.dtype)

    for half in range(2):
        c0 = half * piece
        gq_ref[:, c0:c0 + piece] = l2norm_heads(conv_silu(c0), GDN_DK ** -0.5).astype(BF16)
        if half == 0:
            plain(gz_ref, 1536)
        else:
            plain(dq_ref, 2048, DIFF_DH ** -0.5 * LOG2E)
    for half in range(2):
        c0 = half * piece
        gk_ref[:, c0:c0 + piece] = l2norm_heads(conv_silu(GDN_QK + c0), 1.0).astype(BF16)
        plain(dk_ref if half == 0 else dv_ref, 2560 + half * 512)
    for half in range(2):
        c0 = half * piece
        gv_ref[:, c0:c0 + piece] = conv_silu(2 * GDN_QK + c0).astype(BF16)
        if half == 0:
            plain(dz_ref, 3584)
    hab_ref[...] = _dot(xb_scr[...], wh_ref[:, PROJ_HEAD:PROJ_HEAD + LANES])
    plain(mab_ref, 4096)


def _project(x2, w_bf, w_tail, conv_w, seq):
    m = x2.shape[0]
    tm = PROJ_TM * PROJ_STEP_TILES
    row = lambda width: pl.BlockSpec((tm, width), lambda i: (i, 0))
    full = lambda a: pl.BlockSpec(a.shape, lambda i: (0, 0))
    out_shape = (
        jax.ShapeDtypeStruct((m, GDN_QK), BF16),
        jax.ShapeDtypeStruct((m, GDN_QK), BF16),
        jax.ShapeDtypeStruct((m, GDN_WIDTH), BF16),
        jax.ShapeDtypeStruct((m, LANES), F32),
        jax.ShapeDtypeStruct((m, 512), BF16),
        jax.ShapeDtypeStruct((m, 512), BF16),
        jax.ShapeDtypeStruct((m, 512), BF16),
        jax.ShapeDtypeStruct((m, 512), BF16),
        jax.ShapeDtypeStruct((m, 512), BF16),
        jax.ShapeDtypeStruct((m, 2048), BF16),
    )
    return pl.pallas_call(
        functools.partial(_proj_kernel, seq // PROJ_TM),
        grid=(m // tm,),
        in_specs=[row(D_MODEL),
                  pl.BlockSpec((D_MODEL, PROJ_HEAD + LANES), lambda i: (0, 0)),
                  full(w_tail), full(conv_w)],
        out_specs=tuple(row(s.shape[1]) for s in out_shape),
        out_shape=out_shape,
        scratch_shapes=[pltpu.VMEM((CONV_HIST, 3 * GDN_QK), F32),
                        pltpu.VMEM((PROJ_TM, D_MODEL), BF16)],
        compiler_params=pltpu.CompilerParams(
            dimension_semantics=("arbitrary",), vmem_limit_bytes=VMEM_LIMIT),
        name="in_proj",
    )(x2, w_bf, w_tail, conv_w)


def _gdn_kernel(gq_ref, gk_ref, gv_ref, hab_ref, gz_ref, alog_ref, dtb_ref, nw_ref, o_ref,
                st_scr):
    @pl.when(pl.program_id(1) == 0)
    def _():
        st_scr[...] = jnp.zeros_like(st_scr)

    for ck in range(GDN_STEP_CHUNKS):
        r = pl.ds(ck * GDN_T, GDN_T)
        prep = _gdn_prep(gq_ref.at[r], gk_ref.at[r], gv_ref.at[r], hab_ref.at[r], alog_ref,
                         dtb_ref)
        _gdn_apply(prep, gz_ref.at[r], nw_ref, o_ref.at[r], st_scr)


def _gdn_prep(gq_ref, gk_ref, gv_ref, hab_ref, alog_ref, dtb_ref):
    t = GDN_T
    heads = range(GDN_HEADS)

    hab = hab_ref[...]
    xg = hab + dtb_ref[...]
    y_sp = jnp.exp(-jnp.abs(xg))
    u_sp = 1.0 + y_sp
    softplus = jnp.maximum(xg, 0.0) + (jnp.log(u_sp) - ((u_sp - 1.0) - y_sp) / u_sp)
    g_full = -jnp.exp(alog_ref[...]) * softplus
    beta_full = _sigmoid(hab)

    ri = lax.broadcasted_iota(jnp.int32, (t, t), 0)
    ci = lax.broadcasted_iota(jnp.int32, (t, t), 1)
    causal = ri >= ci
    strict = ri > ci
    eye = jnp.where(ri == ci, 1.0, 0.0).astype(F32)

    tri = jnp.where(causal, 1.0, 0.0).astype(BF16)
    g_hi = g_full.astype(BF16)
    g_r1 = g_full - g_hi.astype(F32)
    g_mid = g_r1.astype(BF16)
    g_lo = (g_r1 - g_mid.astype(F32)).astype(BF16)
    gc = _dot(tri, g_hi) + _dot(tri, g_mid) + _dot(tri, g_lo)
    gc_t = gc.T
    eg = jnp.exp(gc)
    g_last = gc[t - 1:t, :]
    k_dec = jnp.exp(g_last - gc)
    eg_last = jnp.exp(g_last)

    qn_b = [gq_ref[:, h * GDN_DK:(h + 1) * GDN_DK] for h in heads]
    kn_b = [gk_ref[:, h * GDN_DK:(h + 1) * GDN_DK] for h in heads]
    qn = [qn_b[h].astype(F32) for h in heads]
    kn = [kn_b[h].astype(F32) for h in heads]
    vv = [gv_ref[:, h * GDN_DV:(h + 1) * GDN_DV].astype(F32) for h in heads]

    bcol = [beta_full[:, GDN_HEADS + h:GDN_HEADS + h + 1] for h in heads]
    egc = [eg[:, h:h + 1] for h in heads]
    kb = [kn[h] * bcol[h] for h in heads]
    kq = [_dot_nt(jnp.concatenate([kb[h].astype(BF16), qn_b[h]], axis=0), kn_b[h])
          for h in heads]
    decay = []
    for h in heads:
        gd = gc[:, h:h + 1] - gc_t[h:h + 1, :]
        decay.append(jnp.where(causal, jnp.exp(jnp.where(causal, gd, 0.0)), 0.0))
    nmat = [jnp.where(strict, -(kq[h][:t] * decay[h]), 0.0) for h in heads]
    amat = [(kq[h][t:] * decay[h]).astype(BF16) for h in heads]
    rhs = [jnp.concatenate([vv[h] * bcol[h], kb[h] * egc[h]], axis=1).astype(BF16)
           for h in heads]
    qe = [qn[h] * egc[h] for h in heads]
    kd_t = [(kn[h] * k_dec[:, h:h + 1]).T.astype(BF16) for h in heads]

    def same_block(b):
        sh = int(math.log2(b))
        return (ri >> sh) == (ci >> sh)

    blk = GDN_INV_BASE
    in_blk = same_block(blk)
    n0 = [jnp.where(in_blk, nmat[h], 0.0) for h in heads]
    x = [eye + n0[h] for h in heads]
    pw = [n0[h].astype(BF16) for h in heads]
    for _ in range(int(math.log2(blk)) - 1):
        p32 = [_dot(pw[h], pw[h]) for h in heads]
        pw = [p32[h].astype(BF16) for h in heads]
        x = [x[h] + _dot(x[h].astype(BF16), pw[h]) for h in heads]
    while blk < t:
        in_big = same_block(2 * blk)
        n_off = [jnp.where(in_big, jnp.where(in_blk, 0.0, nmat[h]), 0.0).astype(BF16)
                 for h in heads]
        xb = [x[h].astype(BF16) for h in heads]
        xn = [_dot(xb[h], n_off[h]).astype(BF16) for h in heads]
        x = [x[h] + _dot(xn[h], xb[h]) for h in heads]
        in_blk = in_big
        blk *= 2
    sol = [_dot(x[h].astype(BF16), rhs[h]) for h in heads]
    return sol, qe, amat, kd_t, eg_last


def _gdn_apply(prep, gz_ref, nw_ref, o_ref, st_scr):
    t = GDN_T
    heads = range(GDN_HEADS)
    sol, qe, amat, kd_t, eg_last = prep
    s_prev = [st_scr[h] for h in heads]
    ws = [_dot(jnp.concatenate([sol[h][:, GDN_DV:], qe[h]], axis=0).astype(BF16),
               s_prev[h].astype(BF16)) for h in heads]
    v_new = [(sol[h][:, :GDN_DV] - ws[h][:t]).astype(BF16) for h in heads]
    o_l = [ws[h][t:] + _dot(amat[h], v_new[h]) for h in heads]
    for h in heads:
        st_scr[h] = s_prev[h] * eg_last[:, h:h + 1] + _dot(kd_t[h], v_new[h])
    nw = nw_ref[...]
    for h in heads:
        o = o_l[h]
        o = o * lax.rsqrt(jnp.mean(o * o, axis=-1, keepdims=True) + NORM_EPS) * nw
        gate = _silu(gz_ref[:, h * GDN_DV:(h + 1) * GDN_DV].astype(F32))
        o_ref[:, h * GDN_DV:(h + 1) * GDN_DV] = (o * gate).astype(o_ref.dtype)


def _gdn(gq, gk, gv, hab, gz, alog_row, dtb_row, norm_w, batch, seq):
    t = GDN_T * GDN_STEP_CHUNKS
    nt = seq // t
    row = lambda width: pl.BlockSpec((t, width), lambda b, s: (b * nt + s, 0))
    full = lambda a: pl.BlockSpec(a.shape, lambda b, s: (0, 0))
    return pl.pallas_call(
        _gdn_kernel,
        grid=(batch, nt),
        in_specs=[row(GDN_QK), row(GDN_QK), row(GDN_WIDTH), row(LANES), row(GDN_WIDTH),
                  full(alog_row), full(dtb_row), full(norm_w)],
        out_specs=row(GDN_WIDTH),
        out_shape=jax.ShapeDtypeStruct((batch * seq, GDN_WIDTH), BF16),
        scratch_shapes=[pltpu.VMEM((GDN_HEADS, GDN_DK, GDN_DV), F32)],
        compiler_params=pltpu.CompilerParams(
            dimension_semantics=("arbitrary", "arbitrary"), vmem_limit_bytes=VMEM_LIMIT),
        name="gated_deltanet",
    )(gq, gk, gv, hab, gz, alog_row, dtb_row, norm_w)


def _attn_kernel(q_ref, qall_ref, k_ref, v_ref, dz_ref, nw_ref, lq1_ref, lk1_ref, lq2_ref,
                 lk2_ref, o_ref, ke_scr, vt_scr, m_scr, acc_scr, l_scr, p_scr, alpha_scr,
                 bound_scr, flag_smem):
    tq, tk, cols = ATT_TQ, ATT_TK, ATT_ROWS
    n_half = tq // cols
    seq = k_ref.shape[0]
    h = pl.program_id(1)
    i = pl.program_id(2)
    slope = jnp.float32(ALIBI_SLOPES[DIFF_HEADS - 1] * LOG2E)
    n_cap = jnp.int32(_fixed_ref_steps(DIFF_HEADS - 1))
    for hh in range(DIFF_HEADS - 1):
        slope = jnp.where(h == hh, jnp.float32(ALIBI_SLOPES[hh] * LOG2E), slope)
        n_cap = jnp.where(h == hh, jnp.int32(_fixed_ref_steps(hh)), n_cap)

    lane = lax.broadcasted_iota(jnp.int32, (tk, LANES), 1)
    feat_lane = (lane - DIFF_DH, lane)

    def half_norm_max(vf):
        hl = lax.broadcasted_iota(jnp.int32, (LANES, LANES), 0)
        hc = lax.broadcasted_iota(jnp.int32, (LANES, LANES), 1)
        pick = jnp.where(hc == jnp.where(hl < DIFF_DH, 0, 1), 1.0, 0.0).astype(BF16)
        sums = jnp.max(_dot((vf * vf).astype(BF16), pick), axis=0, keepdims=True)
        return jnp.sqrt(jnp.maximum(sums[:, 0:1], sums[:, 1:2])) * (1.0 + 2.0 ** -6)

    @pl.when(i == 0)
    def _():
        bound_scr[...] = jnp.zeros_like(bound_scr)

        def build(blk, carry):
            r0 = pl.multiple_of(blk * tk, tk)
            kf = k_ref[pl.ds(r0, tk), :].astype(F32)
            bound_scr[0:1, :] = jnp.maximum(bound_scr[0:1, :], half_norm_max(kf))
            bound_scr[1:2, :] = jnp.maximum(
                bound_scr[1:2, :], half_norm_max(qall_ref[pl.ds(r0, tk), :].astype(F32)))
            pos = (lax.broadcasted_iota(jnp.int32, (tk, LANES), 0) + r0).astype(F32) * slope
            p_hi = pos.astype(BF16).astype(F32)
            r1 = pos - p_hi
            p_mid = r1.astype(BF16).astype(F32)
            p_lo = r1 - p_mid
            for mp in range(2):
                fl = feat_lane[mp]
                feat = jnp.where(fl == 0, p_hi, jnp.where(fl == 1, p_mid, jnp.where(
                    fl == 2, p_lo, jnp.where((fl >= 3) & (fl < 6), 1.0, 0.0))))
                own = (lane < DIFF_DH) if mp == 0 else (lane >= DIFF_DH)
                ke_scr[mp, pl.ds(r0, tk), :] = jnp.where(own, kf, feat).astype(BF16)
            vt = v_ref[pl.ds(r0, tk), :].astype(F32).T
            vt_scr[blk] = vt.astype(BF16)
            return carry

        lax.fori_loop(0, seq // tk, build, 0, unroll=4)
        qk_all = bound_scr[0:1, :] * bound_scr[1:2, :]
        bound_scr[2:3, :] = qk_all
        flag_smem[0] = jnp.where(jnp.max(qk_all) <= ATT_FIXED_REF_MAX, 1, 0).astype(jnp.int32)

    qf = q_ref[...].astype(F32)
    qk_bound = bound_scr[2:3, 0:1]
    chains = [(mp, r) for mp in range(2) for r in range(n_half)]
    n_chains = len(chains)

    def query_maps(ref_terms):
        out = []
        for mp in range(2):
            fl = feat_lane[mp]
            own = (lane < DIFF_DH) if mp == 0 else (lane >= DIFF_DH)
            feat = jnp.where((fl >= 0) & (fl < 3), 1.0, 0.0)
            if ref_terms is not None:
                r_hi, r_mid, r_lo = ref_terms
                feat = jnp.where(fl == 3, r_hi, jnp.where(fl == 4, r_mid,
                                                          jnp.where(fl == 5, r_lo, feat)))
            out.append(jnp.where(own, qf, feat).astype(BF16))
        return out

    def scores_t(qmaps, c, j, masked):
        mp, r = chains[c]
        nk = (r + 1) * cols if masked else tk
        qc = qmaps[mp][r * cols:(r + 1) * cols]
        kb = ke_scr[mp, pl.ds(pl.multiple_of(j * tk, tk), nk), :]
        st = _dot_nt(kb, qc)
        if masked:
            key_l = lax.broadcasted_iota(jnp.int32, (nk, cols), 0)
            qry_l = lax.broadcasted_iota(jnp.int32, (nk, cols), 1)
            st = jnp.where(key_l <= qry_l + r * cols, st, MASK_NEG)
        return st

    def key_sums(p32):
        nk = p32.shape[0]
        return jnp.sum(p32.reshape(nk // 8, 8, cols), axis=0)

    def store_p(slot, c, p32):
        p = p32.astype(BF16)
        nk = p.shape[0]
        p_scr[slot, c, 0:nk, :] = p
        if nk < tk:
            p_scr[slot, c, nk:, :] = jnp.zeros((tk - nk, cols), BF16)

    def finish(acc_fin):
        def map_out(mp):
            parts = []
            for r in range(n_half):
                c = mp * n_half + r
                denom = jnp.sum(l_scr[c], axis=0, keepdims=True)
                parts.append(acc_fin[c] * (1.0 / denom))
            return jnp.concatenate(parts, axis=1)

        lam = (jnp.exp(jnp.sum(lq1_ref[...] * lk1_ref[...], axis=-1, keepdims=True))
               - jnp.exp(jnp.sum(lq2_ref[...] * lk2_ref[...], axis=-1, keepdims=True))
               + LAM_INIT)
        o_t = map_out(0) - lam * map_out(1)
        o_t = o_t * lax.rsqrt(jnp.mean(o_t * o_t, axis=0, keepdims=True) + SUBLN_EPS)
        o = o_t.T * (nw_ref[...] * (1.0 - LAM_INIT)) * _silu(dz_ref[...].astype(F32))
        o_ref[...] = o.astype(o_ref.dtype)

    def pipeline(scores_step, pv_value, steps_after_diag):
        def scores_block(j, slot, masked=False):
            for c in range(n_chains):
                scores_step(c, j, slot, masked)

        def pv_block(j, slot):
            for c in range(n_chains):
                acc_scr[c] = pv_value(c, j, slot)

        acc_scr[...] = jnp.zeros_like(acc_scr)
        l_scr[...] = jnp.zeros_like(l_scr)
        scores_block(i, 0, masked=True)
        n = steps_after_diag()
        last = i - n

        def pair(u, carry):
            j = i - 2 * u - 1
            scores_block(j, 1)
            pv_block(j + 1, 0)
            scores_block(j - 1, 0)
            pv_block(j, 1)
            return carry

        lax.fori_loop(0, n // 2, pair, 0)

        @pl.when(n % 2 == 1)
        def _():
            scores_block(last, 1)
            pv_block(last + 1, 0)
            finish([pv_value(c, last, 1) for c in range(n_chains)])

        @pl.when(n % 2 == 0)
        def _():
            finish([pv_value(c, last, 0) for c in range(n_chains)])

    use_fixed_ref = flag_smem[0] == 1

    @pl.when(use_fixed_ref)
    def _():
        pos_q = (lax.broadcasted_iota(jnp.int32, (tq, LANES), 0) + i * tq).astype(F32) * slope
        neg_ref = -(qk_bound + ATT_SKIP_SLACK + pos_q)
        r_hi = neg_ref.astype(BF16).astype(F32)
        r1 = neg_ref - r_hi
        r_mid = r1.astype(BF16).astype(F32)
        qmaps = query_maps((r_hi, r_mid, r1 - r_mid))

        def scores_step(c, j, slot, masked):
            p32 = jnp.exp2(scores_t(qmaps, c, j, masked))
            l_scr[c] = l_scr[c] + key_sums(p32)
            store_p(slot, c, p32)

        def pv_value(c, j, slot):
            return acc_scr[c] + _dot(vt_scr[j], p_scr[slot, c])

        pipeline(scores_step, pv_value, lambda: jnp.minimum(i, n_cap))

    @pl.when(jnp.logical_not(use_fixed_ref))
    def _():
        qmaps = query_maps(None)
        m_scr[...] = jnp.full(m_scr.shape, MASK_NEG, F32)

        def scores_step(c, j, slot, masked):
            st = scores_t(qmaps, c, j, masked)
            m_prev = m_scr[c]
            m_new = jnp.maximum(m_prev, jnp.max(st, axis=0, keepdims=True))
            p32 = jnp.exp2(st - m_new)
            alpha = jnp.exp2(m_prev - m_new)
            l_scr[c] = l_scr[c] * alpha + key_sums(p32)
            store_p(slot, c, p32)
            alpha_scr[slot, c] = alpha
            m_scr[c] = m_new

        def pv_value(c, j, slot):
            pv = _dot(vt_scr[j], p_scr[slot, c])
            return acc_scr[c] * alpha_scr[slot, c] + pv

        def steps_after_diag():
            m_min = jnp.min(jnp.min(m_scr[...], axis=0), axis=-1, keepdims=True)
            need = m_min - (ATT_SKIP_EXP2 + ATT_SKIP_SLACK) - qk_bound
            j_first = jnp.ceil(need / (slope * tk) - (tk - 1) / tk)
            n_arr = jnp.clip(i.astype(F32) - jnp.maximum(j_first, 0.0), 0.0, i.astype(F32))
            return jnp.max(n_arr).astype(jnp.int32)

        pipeline(scores_step, pv_value, steps_after_diag)


def _diff_attention(dq, dk, dv, dz, norm_w, lq1, lk1, lq2, lk2, batch, seq):
    tq, tk = ATT_TQ, ATT_TK
    nq = seq // tq
    n_chains = 2 * (tq // ATT_ROWS)
    qspec = pl.BlockSpec((tq, DIFF_DV), lambda b, h, i: (b * nq + i, h))
    kvspec = pl.BlockSpec((seq, DIFF_DV), lambda b, h, i: (b, h))
    full = lambda a: pl.BlockSpec(a.shape, lambda b, h, i: (0, 0))
    return pl.pallas_call(
        _attn_kernel,
        grid=(batch, DIFF_HEADS, nq),
        in_specs=[qspec, kvspec, kvspec, kvspec, qspec, full(norm_w), full(lq1), full(lk1),
                  full(lq2), full(lk2)],
        out_specs=qspec,
        out_shape=jax.ShapeDtypeStruct((batch * seq, DIFF_WIDTH), BF16),
        scratch_shapes=[pltpu.VMEM((2, seq, DIFF_DV), BF16),
                        pltpu.VMEM((seq // tk, DIFF_DV, tk), BF16),
                        pltpu.VMEM((n_chains, 1, ATT_ROWS), F32),
                        pltpu.VMEM((n_chains, DIFF_DV, ATT_ROWS), F32),
                        pltpu.VMEM((n_chains, 8, ATT_ROWS), F32),
                        pltpu.VMEM((2, n_chains, tk, ATT_ROWS), BF16),
                        pltpu.VMEM((2, n_chains, 1, ATT_ROWS), F32),
                        pltpu.VMEM((8, LANES), F32),
                        pltpu.SMEM((1,), jnp.int32)],
        compiler_params=pltpu.CompilerParams(
            dimension_semantics=("arbitrary", "arbitrary", "arbitrary"),
            vmem_limit_bytes=VMEM_LIMIT),
        name="diff_attention",
    )(dq, dq, dk, dv, dz, norm_w, lq1, lk1, lq2, lk2)


def _out_kernel(x_ref, oa_ref, ob_ref, mab_ref, wa_ref, wb_ref, wo_ref, g_ref, b_ref, y_ref,
                merged_scr):
    for tile in range(OUT_STEP_TILES):
        r = pl.ds(tile * OUT_TM, OUT_TM)
        _out_tile(x_ref.at[r], oa_ref.at[r], ob_ref.at[r], mab_ref.at[r], wa_ref, wb_ref,
                  wo_ref, g_ref, b_ref, y_ref.at[r], merged_scr)


def _out_tile(x_ref, oa_ref, ob_ref, mab_ref, wa_ref, wb_ref, wo_ref, g_ref, b_ref, y_ref,
              merged_scr):
    for c0 in range(0, D_MODEL, OUT_CHUNK):
        cs = slice(c0, c0 + OUT_CHUNK)
        ya = _dot(oa_ref[...], wa_ref[:, cs])
        yb = _dot(ob_ref[...], wb_ref[:, cs])
        ga = _sigmoid(mab_ref[:, c0:c0 + OUT_CHUNK].astype(F32))
        gb = _sigmoid(mab_ref[:, D_MODEL + c0:D_MODEL + c0 + OUT_CHUNK].astype(F32))
        merged_scr[:, cs] = (ga * ya + gb * yb).astype(BF16)
    r0 = 0
    for nrows in OUT_ROW_GROUPS:
        rs = slice(r0, r0 + nrows)
        r0 += nrows
        y = _dot(merged_scr[rs, :], wo_ref[...])
        z = DEEPNORM_ALPHA * x_ref[rs, :] + y
        mu = jnp.mean(z, axis=-1, keepdims=True)
        zc = z - mu
        var = jnp.mean(zc * zc, axis=-1, keepdims=True)
        y_ref[rs, :] = zc * lax.rsqrt(var + LN_EPS) * g_ref[...] + b_ref[...]


def _output(x2, o_a, o_b, mab, wa, wb, wo, ln_g, ln_b):
    m = x2.shape[0]
    tm = OUT_TM * OUT_STEP_TILES
    row = lambda width: pl.BlockSpec((tm, width), lambda i: (i, 0))
    full = lambda a: pl.BlockSpec(a.shape, lambda i: (0, 0))
    return pl.pallas_call(
        _out_kernel,
        grid=(m // tm,),
        in_specs=[row(D_MODEL), row(GDN_WIDTH), row(DIFF_WIDTH), row(2 * D_MODEL),
                  full(wa), full(wb), full(wo), full(ln_g), full(ln_b)],
        out_specs=row(D_MODEL),
        out_shape=jax.ShapeDtypeStruct((m, D_MODEL), F32),
        scratch_shapes=[pltpu.VMEM((OUT_TM, D_MODEL), BF16)],
        compiler_params=pltpu.CompilerParams(
            dimension_semantics=("arbitrary",), vmem_limit_bytes=VMEM_LIMIT),
        name="merge_out_ln",
    )(x2, o_a, o_b, mab, wa, wb, wo, ln_g, ln_b)


def _lane_row(v):
    return jnp.zeros((1, LANES), F32).at[0, :v.shape[0]].set(v.astype(F32))


def kernel(x, w_in, conv_w, a_log, dt_bias, gdn_norm_w, w_up_a, lambda_q1, lambda_k1,
           lambda_q2, lambda_k2, diff_norm_w, w_up_b, w_out, ln_g, ln_b):
    batch, seq, d = x.shape
    x2 = x.reshape(batch * seq, d)
    layer = 0
    w = w_in[layer].astype(BF16)
    w_tail = w[:, PROJ_HEAD + 2 * GDN_HEADS:]

    gq, gk, gv, hab, gz, dq, dk, dv, dz, mab = _project(
        x2, w, w_tail, conv_w[layer].astype(F32), seq)

    o_a = _gdn(gq, gk, gv, hab, gz, _lane_row(a_log[layer]), _lane_row(dt_bias[layer]),
               gdn_norm_w[layer].reshape(1, GDN_DV).astype(F32), batch, seq)

    o_b = _diff_attention(
        dq, dk, dv, dz, diff_norm_w[layer].reshape(1, DIFF_DV).astype(F32),
        lambda_q1[layer].reshape(1, DIFF_DH).astype(F32),
        lambda_k1[layer].reshape(1, DIFF_DH).astype(F32),
        lambda_q2[layer].reshape(1, DIFF_DH).astype(F32),
        lambda_k2[layer].reshape(1, DIFF_DH).astype(F32), batch, seq)

    y = _output(x2, o_a, o_b, mab, w_up_a[layer].astype(BF16), w_up_b[layer].astype(BF16),
                w_out[layer].astype(BF16), ln_g[layer].reshape(1, d).astype(F32),
                ln_b[layer].reshape(1, d).astype(F32))
    return y.reshape(batch, seq, d)
```

```python
import functools
import math

import jax
import jax.numpy as jnp
from jax import lax
from jax.experimental import pallas as pl
from jax.experimental.pallas import tpu as pltpu

F32 = jnp.float32
BF16 = jnp.bfloat16

D_MODEL = 1024
GDN_HEADS = 4
GDN_DK = 128
GDN_DV = 128
GDN_QK = GDN_HEADS * GDN_DK
GDN_WIDTH = GDN_HEADS * GDN_DV
CONV_K = 4
DIFF_HEADS = 4
DIFF_DH = 64
DIFF_DV = 2 * DIFF_DH
DIFF_QK = DIFF_HEADS * 2 * DIFF_DH
DIFF_WIDTH = DIFF_HEADS * DIFF_DV
NORM_EPS = 1e-6
SUBLN_EPS = 1e-5
LN_EPS = 1e-5
DEPTH = 1
DEEPNORM_ALPHA = (2.0 * DEPTH) ** 0.25
LAM_INIT = 0.8 - 0.6 * math.exp(-0.3 * 0)
ALIBI_SLOPES = tuple(2.0 ** (-8.0 * (i + 1) / DIFF_HEADS) for i in range(DIFF_HEADS))
LOG2E = math.log2(math.e)

LANES = 128
CONV_HIST = 8
VMEM_LIMIT = 56 * 1024 * 1024

PROJ_TM = 256
PROJ_STEP_TILES = 1
PROJ_HEAD = 2 * GDN_QK + GDN_WIDTH
GDN_T = 256
GDN_STEP_CHUNKS = 4
GDN_INV_BASE = 16
ATT_TQ = 512
ATT_TK = 512
ATT_ROWS = 256
OUT_TM = 512
OUT_STEP_TILES = 2
OUT_CHUNK = 256
OUT_ROW_GROUPS = (256, 256)
assert sum(OUT_ROW_GROUPS) == OUT_TM
MASK_NEG = -1e30
ATT_SKIP_EXP2 = 150.0
ATT_SKIP_SLACK = 1.0
ATT_FIXED_REF_MAX = 48.0


def _fixed_ref_steps(head):
    slope = ALIBI_SLOPES[head] * LOG2E
    return max(math.ceil(ATT_SKIP_EXP2 / (slope * ATT_TK) + (ATT_TK - 1) / ATT_TK) - 1, 0)


def _sigmoid(x):
    return 1.0 / (1.0 + jnp.exp(-x))


def _silu(x):
    return x * _sigmoid(x)


def _dot(a, b):
    return jnp.dot(a, b, preferred_element_type=F32)


def _dot_nt(a, b):
    return lax.dot_general(a, b, (((1,), (1,)), ((), ())), preferred_element_type=F32)


def _proj_kernel(tiles_per_seq, x_ref, wh_ref, wt_ref, cw_ref, *rest):
    out_refs, (hist_scr, xb_scr) = rest[:-2], rest[-2:]
    for tile in range(PROJ_STEP_TILES):
        r = pl.ds(tile * PROJ_TM, PROJ_TM)
        _proj_tile(pl.program_id(0) * PROJ_STEP_TILES + tile, tiles_per_seq, x_ref.at[r],
                   wh_ref, wt_ref, cw_ref, *[o.at[r] for o in out_refs], hist_scr, xb_scr)


def _proj_tile(tile_idx, tiles_per_seq, x_ref, wh_ref, wt_ref, cw_ref, gq_ref, gk_ref, gv_ref,
               hab_ref, gz_ref, dq_ref, dk_ref, dv_ref, dz_ref, mab_ref, hist_scr, xb_scr):
    tm = PROJ_TM

    @pl.when(tile_idx % tiles_per_seq == 0)
    def _():
        hist_scr[...] = jnp.zeros_like(hist_scr)

    xb_scr[...] = x_ref[...].astype(BF16)

    def mm(c0, width):
        if c0 < PROJ_HEAD:
            return _dot(xb_scr[...], wh_ref[:, c0:c0 + width])
        return _dot(xb_scr[...], wt_ref[:, c0 - PROJ_HEAD:c0 - PROJ_HEAD + width])

    piece = 2 * GDN_DK

    def conv_silu(c0):
        cols = slice(c0, c0 + piece)
        acc = mm(c0, piece)
        ext = jnp.concatenate([hist_scr[:, cols], acc], axis=0)
        hist_scr[:, cols] = acc[tm - CONV_HIST:]
        y = None
        for j in range(CONV_K):
            r0 = CONV_HIST - (CONV_K - 1) + j
            term = cw_ref[j:j + 1, cols] * ext[r0:r0 + tm]
            y = term if y is None else y + term
        return _silu(y)

    def l2norm_heads(y, scale):
        parts = []
        for h in range(piece // GDN_DK):
            v = y[:, h * GDN_DK:(h + 1) * GDN_DK]
            inv = lax.rsqrt(jnp.sum(v * v, axis=-1, keepdims=True) + NORM_EPS)
            parts.append(v * (inv * scale) if scale != 1.0 else v * inv)
        return jnp.concatenate(parts, axis=1)

    def plain(out_ref, c0, scale=None):
        for j in range(out_ref.shape[1] // 512):
            acc = mm(c0 + j * 512, 512)
            if scale is not None:
                acc = acc * scale
            out_ref[:, j * 512:(j + 1) * 512] = acc.astype(out_ref.dtype)

    for half in range(2):
        c0 = half * piece
        gq_ref[:, c0:c0 + piece] = l2norm_heads(conv_silu(c0), GDN_DK ** -0.5).astype(BF16)
        if half == 0:
            plain(gz_ref, 1536)
        else:
            plain(dq_ref, 2048, DIFF_DH ** -0.5 * LOG2E)
    for half in range(2):
        c0 = half * piece
        gk_ref[:, c0:c0 + piece] = l2norm_heads(conv_silu(GDN_QK + c0), 1.0).astype(BF16)
        plain(dk_ref if half == 0 else dv_ref, 2560 + half * 512)
    for half in range(2):
        c0 = half * piece
        gv_ref[:, c0:c0 + piece] = conv_silu(2 * GDN_QK + c0).astype(BF16)
        if half == 0:
            plain(dz_ref, 3584)
    hab_ref[...] = _dot(xb_scr[...], wh_ref[:, PROJ_HEAD:PROJ_HEAD + LANES])
    plain(mab_ref, 4096)


def _project(x2, w_bf, w_tail, conv_w, seq):
    m = x2.shape[0]
    tm = PROJ_TM * PROJ_STEP_TILES
    row = lambda width: pl.BlockSpec((tm, width), lambda i: (i, 0))
    full = lambda a: pl.BlockSpec(a.shape, lambda i: (0, 0))
    out_shape = (
        jax.ShapeDtypeStruct((m, GDN_QK), BF16),
        jax.ShapeDtypeStruct((m, GDN_QK), BF16),
        jax.ShapeDtypeStruct((m, GDN_WIDTH), BF16),
        jax.ShapeDtypeStruct((m, LANES), F32),
        jax.ShapeDtypeStruct((m, 512), BF16),
        jax.ShapeDtypeStruct((m, 512), BF16),
        jax.ShapeDtypeStruct((m, 512), BF16),
        jax.ShapeDtypeStruct((m, 512), BF16),
        jax.ShapeDtypeStruct((m, 512), BF16),
        jax.ShapeDtypeStruct((m, 2048), BF16),
    )
    return pl.pallas_call(
        functools.partial(_proj_kernel, seq // PROJ_TM),
        grid=(m // tm,),
        in_specs=[row(D_MODEL),
                  pl.BlockSpec((D_MODEL, PROJ_HEAD + LANES), lambda i: (0, 0)),
                  full(w_tail), full(conv_w)],
        out_specs=tuple(row(s.shape[1]) for s in out_shape),
        out_shape=out_shape,
        scratch_shapes=[pltpu.VMEM((CONV_HIST, 3 * GDN_QK), F32),
                        pltpu.VMEM((PROJ_TM, D_MODEL), BF16)],
        compiler_params=pltpu.CompilerParams(
            dimension_semantics=("arbitrary",), vmem_limit_bytes=VMEM_LIMIT),
        name="in_proj",
    )(x2, w_bf, w_tail, conv_w)


def _gdn_kernel(gq_ref, gk_ref, gv_ref, hab_ref, gz_ref, alog_ref, dtb_ref, nw_ref, o_ref,
                st_scr):
    @pl.when(pl.program_id(1) == 0)
    def _():
        st_scr[...] = jnp.zeros_like(st_scr)

    for ck in range(GDN_STEP_CHUNKS):
        r = pl.ds(ck * GDN_T, GDN_T)
        prep = _gdn_prep(gq_ref.at[r], gk_ref.at[r], gv_ref.at[r], hab_ref.at[r], alog_ref,
                         dtb_ref)
        _gdn_apply(prep, gz_ref.at[r], nw_ref, o_ref.at[r], st_scr)


def _gdn_prep(gq_ref, gk_ref, gv_ref, hab_ref, alog_ref, dtb_ref):
    t = GDN_T
    heads = range(GDN_HEADS)

    hab = hab_ref[...]
    xg = hab + dtb_ref[...]
    y_sp = jnp.exp(-jnp.abs(xg))
    u_sp = 1.0 + y_sp
    softplus = jnp.maximum(xg, 0.0) + (jnp.log(u_sp) - ((u_sp - 1.0) - y_sp) / u_sp)
    g_full = -jnp.exp(alog_ref[...]) * softplus
    beta_full = _sigmoid(hab)

    ri = lax.broadcasted_iota(jnp.int32, (t, t), 0)
    ci = lax.broadcasted_iota(jnp.int32, (t, t), 1)
    causal = ri >= ci
    strict = ri > ci
    eye = jnp.where(ri == ci, 1.0, 0.0).astype(F32)

    tri = jnp.where(causal, 1.0, 0.0).astype(BF16)
    g_hi = g_full.astype(BF16)
    g_r1 = g_full - g_hi.astype(F32)
    g_mid = g_r1.astype(BF16)
    g_lo = (g_r1 - g_mid.astype(F32)).astype(BF16)
    gc = _dot(tri, g_hi) + _dot(tri, g_mid) + _dot(tri, g_lo)
    gc_t = gc.T
    eg = jnp.exp(gc)
    g_last = gc[t - 1:t, :]
    k_dec = jnp.exp(g_last - gc)
    eg_last = jnp.exp(g_last)

    qn_b = [gq_ref[:, h * GDN_DK:(h + 1) * GDN_DK] for h in heads]
    kn_b = [gk_ref[:, h * GDN_DK:(h + 1) * GDN_DK] for h in heads]
    qn = [qn_b[h].astype(F32) for h in heads]
    kn = [kn_b[h].astype(F32) for h in heads]
    vv = [gv_ref[:, h * GDN_DV:(h + 1) * GDN_DV].astype(F32) for h in heads]

    bcol = [beta_full[:, GDN_HEADS + h:GDN_HEADS + h + 1] for h in heads]
    egc = [eg[:, h:h + 1] for h in heads]
    kb = [kn[h] * bcol[h] for h in heads]
    kq = [_dot_nt(jnp.concatenate([kb[h].astype(BF16), qn_b[h]], axis=0), kn_b[h])
          for h in heads]
    decay = []
    for h in heads:
        gd = gc[:, h:h + 1] - gc_t[h:h + 1, :]
        decay.append(jnp.where(causal, jnp.exp(jnp.where(causal, gd, 0.0)), 0.0))
    nmat = [jnp.where(strict, -(kq[h][:t] * decay[h]), 0.0) for h in heads]
    amat = [(kq[h][t:] * decay[h]).astype(BF16) for h in heads]
    rhs = [jnp.concatenate([vv[h] * bcol[h], kb[h] * egc[h]], axis=1).astype(BF16)
           for h in heads]
    qe = [qn[h] * egc[h] for h in heads]
    kd_t = [(kn[h] * k_dec[:, h:h + 1]).T.astype(BF16) for h in heads]

    def same_block(b):
        sh = int(math.log2(b))
        return (ri >> sh) == (ci >> sh)

    blk = GDN_INV_BASE
    in_blk = same_block(blk)
    n0 = [jnp.where(in_blk, nmat[h], 0.0) for h in heads]
    x = [eye + n0[h] for h in heads]
    pw = [n0[h].astype(BF16) for h in heads]
    for _ in range(int(math.log2(blk)) - 1):
        p32 = [_dot(pw[h], pw[h]) for h in heads]
        pw = [p32[h].astype(BF16) for h in heads]
        x = [x[h] + _dot(x[h].astype(BF16), pw[h]) for h in heads]
    while blk < t:
        in_big = same_block(2 * blk)
        n_off = [jnp.where(in_big, jnp.where(in_blk, 0.0, nmat[h]), 0.0).astype(BF16)
                 for h in heads]
        xb = [x[h].astype(BF16) for h in heads]
        xn = [_dot(xb[h], n_off[h]).astype(BF16) for h in heads]
        x = [x[h] + _dot(xn[h], xb[h]) for h in heads]
        in_blk = in_big
        blk *= 2
    sol = [_dot(x[h].astype(BF16), rhs[h]) for h in heads]
    return sol, qe, amat, kd_t, eg_last


def _gdn_apply(prep, gz_ref, nw_ref, o_ref, st_scr):
    t = GDN_T
    heads = range(GDN_HEADS)
    sol, qe, amat, kd_t, eg_last = prep
    s_prev = [st_scr[h] for h in heads]
    ws = [_dot(jnp.concatenate([sol[h][:, GDN_DV:], qe[h]], axis=0).astype(BF16),
               s_prev[h].astype(BF16)) for h in heads]
    v_new = [(sol[h][:, :GDN_DV] - ws[h][:t]).astype(BF16) for h in heads]
    o_l = [ws[h][t:] + _dot(amat[h], v_new[h]) for h in heads]
    for h in heads:
        st_scr[h] = s_prev[h] * eg_last[:, h:h + 1] + _dot(kd_t[h], v_new[h])
    nw = nw_ref[...]
    for h in heads:
        o = o_l[h]
        o = o * lax.rsqrt(jnp.mean(o * o, axis=-1, keepdims=True) + NORM_EPS) * nw
        gate = _silu(gz_ref[:, h * GDN_DV:(h + 1) * GDN_DV].astype(F32))
        o_ref[:, h * GDN_DV:(h + 1) * GDN_DV] = (o * gate).astype(o_ref.dtype)


def _gdn(gq, gk, gv, hab, gz, alog_row, dtb_row, norm_w, batch, seq):
    t = GDN_T * GDN_STEP_CHUNKS
    nt = seq // t
    row = lambda width: pl.BlockSpec((t, width), lambda b, s: (b * nt + s, 0))
    full = lambda a: pl.BlockSpec(a.shape, lambda b, s: (0, 0))
    return pl.pallas_call(
        _gdn_kernel,
        grid=(batch, nt),
        in_specs=[row(GDN_QK), row(GDN_QK), row(GDN_WIDTH), row(LANES), row(GDN_WIDTH),
                  full(alog_row), full(dtb_row), full(norm_w)],
        out_specs=row(GDN_WIDTH),
        out_shape=jax.ShapeDtypeStruct((batch * seq, GDN_WIDTH), BF16),
        scratch_shapes=[pltpu.VMEM((GDN_HEADS, GDN_DK, GDN_DV), F32)],
        compiler_params=pltpu.CompilerParams(
            dimension_semantics=("arbitrary", "arbitrary"), vmem_limit_bytes=VMEM_LIMIT),
        name="gated_deltanet",
    )(gq, gk, gv, hab, gz, alog_row, dtb_row, norm_w)


def _attn_kernel(q_ref, qall_ref, k_ref, v_ref, dz_ref, nw_ref, lq1_ref, lk1_ref, lq2_ref,
                 lk2_ref, o_ref, ke_scr, vt_scr, m_scr, acc_scr, l_scr, p_scr, alpha_scr,
                 bound_scr, flag_smem):
    tq, tk, cols = ATT_TQ, ATT_TK, ATT_ROWS
    n_half = tq // cols
    seq = k_ref.shape[0]
    h = pl.program_id(1)
    i = pl.program_id(2)
    slope = jnp.float32(ALIBI_SLOPES[DIFF_HEADS - 1] * LOG2E)
    n_cap = jnp.int32(_fixed_ref_steps(DIFF_HEADS - 1))
    for hh in range(DIFF_HEADS - 1):
        slope = jnp.where(h == hh, jnp.float32(ALIBI_SLOPES[hh] * LOG2E), slope)
        n_cap = jnp.where(h == hh, jnp.int32(_fixed_ref_steps(hh)), n_cap)

    lane = lax.broadcasted_iota(jnp.int32, (tk, LANES), 1)
    feat_lane = lane % DIFF_DH

    def half_norm_max(vf):
        hl = lax.broadcasted_iota(jnp.int32, (LANES, LANES), 0)
        hc = lax.broadcasted_iota(jnp.int32, (LANES, LANES), 1)
        pick = jnp.where(hc == jnp.where(hl < DIFF_DH, 0, 1), 1.0, 0.0).astype(BF16)
        sums = jnp.max(_dot((vf * vf).astype(BF16), pick), axis=0, keepdims=True)
        return jnp.sqrt(jnp.maximum(sums[:, 0:1], sums[:, 1:2])) * (1.0 + 2.0 ** -6)

    @pl.when(i == 0)
    def _():
        bound_scr[...] = jnp.zeros_like(bound_scr)

        def build(blk, carry):
            r0 = pl.multiple_of(blk * tk, tk)
            kf = k_ref[pl.ds(r0, tk), :].astype(F32)
            bound_scr[0:1, :] = jnp.maximum(bound_scr[0:1, :], half_norm_max(kf))
            bound_scr[1:2, :] = jnp.maximum(
                bound_scr[1:2, :], half_norm_max(qall_ref[pl.ds(r0, tk), :].astype(F32)))
            pos = (lax.broadcasted_iota(jnp.int32, (tk, LANES), 0) + r0).astype(F32) * slope
            p_hi = pos.astype(BF16).astype(F32)
            r1 = pos - p_hi
            p_mid = r1.astype(BF16).astype(F32)
            p_lo = r1 - p_mid
            feat = jnp.where(feat_lane == 0, p_hi, jnp.where(feat_lane == 1, p_mid, jnp.where(
                feat_lane == 2, p_lo, jnp.where(feat_lane < 6, 1.0, 0.0))))
            for mp in range(2):
                own = (lane < DIFF_DH) if mp == 0 else (lane >= DIFF_DH)
                ke_scr[mp, pl.ds(r0, tk), :] = jnp.where(own, kf, feat).astype(BF16)
            vt = v_ref[pl.ds(r0, tk), :].astype(F32).T
            vt_scr[blk] = vt.astype(BF16)
            return carry

        lax.fori_loop(0, seq // tk, build, 0, unroll=4)
        qk_all = bound_scr[0:1, :] * bound_scr[1:2, :]
        bound_scr[2:3, :] = qk_all
        flag_smem[0] = jnp.where(jnp.max(qk_all) <= ATT_FIXED_REF_MAX, 1, 0).astype(jnp.int32)

    qf = q_ref[...].astype(F32)
    qk_bound = bound_scr[2:3, 0:1]
    chains = [(mp, r) for mp in range(2) for r in range(n_half)]
    n_chains = len(chains)

    def query_maps(ref_terms):
        feat = jnp.where(feat_lane < 3, 1.0, 0.0)
        if ref_terms is not None:
            r_hi, r_mid, r_lo = ref_terms
            feat = jnp.where(feat_lane == 3, r_hi, jnp.where(
                feat_lane == 4, r_mid, jnp.where(feat_lane == 5, r_lo, feat)))
        return [jnp.where((lane < DIFF_DH) if mp == 0 else (lane >= DIFF_DH), qf, feat)
                .astype(BF16) for mp in range(2)]

    def scores_t(qmaps, c, j, masked):
        mp, r = chains[c]
        nk = (r + 1) * cols if masked else tk
        qc = qmaps[mp][r * cols:(r + 1) * cols]
        kb = ke_scr[mp, pl.ds(pl.multiple_of(j * tk, tk), nk), :]
        st = _dot_nt(kb, qc)
        if masked:
            key_l = lax.broadcasted_iota(jnp.int32, (nk, cols), 0)
            qry_l = lax.broadcasted_iota(jnp.int32, (nk, cols), 1)
            st = jnp.where(key_l <= qry_l + r * cols, st, MASK_NEG)
        return st

    def key_sums(p32):
        nk = p32.shape[0]
        return jnp.sum(p32.reshape(nk // 8, 8, cols), axis=0)

    def store_p(slot, c, p32):
        p = p32.astype(BF16)
        nk = p.shape[0]
        p_scr[slot, c, 0:nk, :] = p
        if nk < tk:
            p_scr[slot, c, nk:, :] = jnp.zeros((tk - nk, cols), BF16)

    def finish(acc_fin):
        def map_out(mp):
            parts = []
            for r in range(n_half):
                c = mp * n_half + r
                denom = jnp.sum(l_scr[c], axis=0, keepdims=True)
                parts.append(acc_fin[c] * (1.0 / denom))
            return jnp.concatenate(parts, axis=1)

        lam = (jnp.exp(jnp.sum(lq1_ref[...] * lk1_ref[...], axis=-1, keepdims=True))
               - jnp.exp(jnp.sum(lq2_ref[...] * lk2_ref[...], axis=-1, keepdims=True))
               + LAM_INIT)
        o_t = map_out(0) - lam * map_out(1)
        o_t = o_t * lax.rsqrt(jnp.mean(o_t * o_t, axis=0, keepdims=True) + SUBLN_EPS)
        o = o_t.T * (nw_ref[...] * (1.0 - LAM_INIT)) * _silu(dz_ref[...].astype(F32))
        o_ref[...] = o.astype(o_ref.dtype)

    def pipeline(scores_step, pv_value, steps_after_diag):
        def scores_block(j, slot, masked=False):
            for c in range(n_chains):
                scores_step(c, j, slot, masked)

        def pv_block(j, slot):
            for c in range(n_chains):
                acc_scr[c] = pv_value(c, j, slot)

        acc_scr[...] = jnp.zeros_like(acc_scr)
        l_scr[...] = jnp.zeros_like(l_scr)
        scores_block(i, 0, masked=True)
        n = steps_after_diag()
        last = i - n

        def pair(u, carry):
            j = i - 2 * u - 1
            scores_block(j, 1)
            pv_block(j + 1, 0)
            scores_block(j - 1, 0)
            pv_block(j, 1)
            return carry

        lax.fori_loop(0, n // 2, pair, 0)

        @pl.when(n % 2 == 1)
        def _():
            scores_block(last, 1)
            pv_block(last + 1, 0)
            finish([pv_value(c, last, 1) for c in range(n_chains)])

        @pl.when(n % 2 == 0)
        def _():
            finish([pv_value(c, last, 0) for c in range(n_chains)])

    use_fixed_ref = flag_smem[0] == 1

    @pl.when(use_fixed_ref)
    def _():
        pos_q = (lax.broadcasted_iota(jnp.int32, (tq, LANES), 0) + i * tq).astype(F32) * slope
        neg_ref = -(qk_bound + ATT_SKIP_SLACK + pos_q)
        r_hi = neg_ref.astype(BF16).astype(F32)
        r1 = neg_ref - r_hi
        r_mid = r1.astype(BF16).astype(F32)
        qmaps = query_maps((r_hi, r_mid, r1 - r_mid))

        def scores_step(c, j, slot, masked):
            p32 = jnp.exp2(scores_t(qmaps, c, j, masked))
            l_scr[c] = l_scr[c] + key_sums(p32)
            store_p(slot, c, p32)

        def pv_value(c, j, slot):
            return acc_scr[c] + _dot(vt_scr[j], p_scr[slot, c])

        pipeline(scores_step, pv_value, lambda: jnp.minimum(i, n_cap))

    @pl.when(jnp.logical_not(use_fixed_ref))
    def _():
        qmaps = query_maps(None)
        m_scr[...] = jnp.full(m_scr.shape, MASK_NEG, F32)

        def scores_step(c, j, slot, masked):
            st = scores_t(qmaps, c, j, masked)
            m_prev = m_scr[c]
            m_new = jnp.maximum(m_prev, jnp.max(st, axis=0, keepdims=True))
            p32 = jnp.exp2(st - m_new)
            alpha = jnp.exp2(m_prev - m_new)
            l_scr[c] = l_scr[c] * alpha + key_sums(p32)
            store_p(slot, c, p32)
            alpha_scr[slot, c] = alpha
            m_scr[c] = m_new

        def pv_value(c, j, slot):
            pv = _dot(vt_scr[j], p_scr[slot, c])
            return acc_scr[c] * alpha_scr[slot, c] + pv

        def steps_after_diag():
            m_min = jnp.min(jnp.min(m_scr[...], axis=0), axis=-1, keepdims=True)
            need = m_min - (ATT_SKIP_EXP2 + ATT_SKIP_SLACK) - qk_bound
            j_first = jnp.ceil(need / (slope * tk) - (tk - 1) / tk)
            n_arr = jnp.clip(i.astype(F32) - jnp.maximum(j_first, 0.0), 0.0, i.astype(F32))
            return jnp.max(n_arr).astype(jnp.int32)

        pipeline(scores_step, pv_value, steps_after_diag)


def _diff_attention(dq, dk, dv, dz, norm_w, lq1, lk1, lq2, lk2, batch, seq):
    tq, tk = ATT_TQ, ATT_TK
    nq = seq // tq
    n_chains = 2 * (tq // ATT_ROWS)
    qspec = pl.BlockSpec((tq, DIFF_DV), lambda b, h, i: (b * nq + i, h))
    kvspec = pl.BlockSpec((seq, DIFF_DV), lambda b, h, i: (b, h))
    full = lambda a: pl.BlockSpec(a.shape, lambda b, h, i: (0, 0))
    return pl.pallas_call(
        _attn_kernel,
        grid=(batch, DIFF_HEADS, nq),
        in_specs=[qspec, kvspec, kvspec, kvspec, qspec, full(norm_w), full(lq1), full(lk1),
                  full(lq2), full(lk2)],
        out_specs=qspec,
        out_shape=jax.ShapeDtypeStruct((batch * seq, DIFF_WIDTH), BF16),
        scratch_shapes=[pltpu.VMEM((2, seq, DIFF_DV), BF16),
                        pltpu.VMEM((seq // tk, DIFF_DV, tk), BF16),
                        pltpu.VMEM((n_chains, 1, ATT_ROWS), F32),
                        pltpu.VMEM((n_chains, DIFF_DV, ATT_ROWS), F32),
                        pltpu.VMEM((n_chains, 8, ATT_ROWS), F32),
                        pltpu.VMEM((2, n_chains, tk, ATT_ROWS), BF16),
                        pltpu.VMEM((2, n_chains, 1, ATT_ROWS), F32),
                        pltpu.VMEM((8, LANES), F32),
                        pltpu.SMEM((1,), jnp.int32)],
        compiler_params=pltpu.CompilerParams(
            dimension_semantics=("arbitrary", "arbitrary", "arbitrary"),
            vmem_limit_bytes=VMEM_LIMIT),
        name="diff_attention",
    )(dq, dq, dk, dv, dz, norm_w, lq1, lk1, lq2, lk2)


def _out_kernel(x_ref, oa_ref, ob_ref, mab_ref, wa_ref, wb_ref, wo_ref, g_ref, b_ref, y_ref,
                merged_scr):
    for tile in range(OUT_STEP_TILES):
        r = pl.ds(tile * OUT_TM, OUT_TM)
        _out_tile(x_ref.at[r], oa_ref.at[r], ob_ref.at[r], mab_ref.at[r], wa_ref, wb_ref,
                  wo_ref, g_ref, b_ref, y_ref.at[r], merged_scr)


def _out_tile(x_ref, oa_ref, ob_ref, mab_ref, wa_ref, wb_ref, wo_ref, g_ref, b_ref, y_ref,
              merged_scr):
    for c0 in range(0, D_MODEL, OUT_CHUNK):
        cs = slice(c0, c0 + OUT_CHUNK)
        ya = _dot(oa_ref[...], wa_ref[:, cs])
        yb = _dot(ob_ref[...], wb_ref[:, cs])
        ga = _sigmoid(mab_ref[:, c0:c0 + OUT_CHUNK].astype(F32))
        gb = _sigmoid(mab_ref[:, D_MODEL + c0:D_MODEL + c0 + OUT_CHUNK].astype(F32))
        merged_scr[:, cs] = (ga * ya + gb * yb).astype(BF16)
    r0 = 0
    for nrows in OUT_ROW_GROUPS:
        rs = slice(r0, r0 + nrows)
        r0 += nrows
        y = _dot(merged_scr[rs, :], wo_ref[...])
        z = DEEPNORM_ALPHA * x_ref[rs, :] + y
        mu = jnp.mean(z, axis=-1, keepdims=True)
        zc = z - mu
        var = jnp.mean(zc * zc, axis=-1, keepdims=True)
        y_ref[rs, :] = zc * lax.rsqrt(var + LN_EPS) * g_ref[...] + b_ref[...]


def _output(x2, o_a, o_b, mab, wa, wb, wo, ln_g, ln_b):
    m = x2.shape[0]
    tm = OUT_TM * OUT_STEP_TILES
    row = lambda width: pl.BlockSpec((tm, width), lambda i: (i, 0))
    full = lambda a: pl.BlockSpec(a.shape, lambda i: (0, 0))
    return pl.pallas_call(
        _out_kernel,
        grid=(m // tm,),
        in_specs=[row(D_MODEL), row(GDN_WIDTH), row(DIFF_WIDTH), row(2 * D_MODEL),
                  full(wa), full(wb), full(wo), full(ln_g), full(ln_b)],
        out_specs=row(D_MODEL),
        out_shape=jax.ShapeDtypeStruct((m, D_MODEL), F32),
        scratch_shapes=[pltpu.VMEM((OUT_TM, D_MODEL), BF16)],
        compiler_params=pltpu.CompilerParams(
            dimension_semantics=("arbitrary",), vmem_limit_bytes=VMEM_LIMIT),
        name="merge_out_ln",
    )(x2, o_a, o_b, mab, wa, wb, wo, ln_g, ln_b)


def _lane_row(v):
    return jnp.zeros((1, LANES), F32).at[0, :v.shape[0]].set(v.astype(F32))


def kernel(x, w_in, conv_w, a_log, dt_bias, gdn_norm_w, w_up_a, lambda_q1, lambda_k1,
           lambda_q2, lambda_k2, diff_norm_w, w_up_b, w_out, ln_g, ln_b):
    batch, seq, d = x.shape
    x2 = x.reshape(batch * seq, d)
    layer = 0
    w = w_in[layer].astype(BF16)
    w_tail = w[:, PROJ_HEAD + 2 * GDN_HEADS:]

    gq, gk, gv, hab, gz, dq, dk, dv, dz, mab = _project(
        x2, w, w_tail, conv_w[layer].astype(F32), seq)

    o_a = _gdn(gq, gk, gv, hab, gz, _lane_row(a_log[layer]), _lane_row(dt_bias[layer]),
               gdn_norm_w[layer].reshape(1, GDN_DV).astype(F32), batch, seq)

    o_b = _diff_attention(
        dq, dk, dv, dz, diff_norm_w[layer].reshape(1, DIFF_DV).astype(F32),
        lambda_q1[layer].reshape(1, DIFF_DH).astype(F32),
        lambda_k1[layer].reshape(1, DIFF_DH).astype(F32),
        lambda_q2[layer].reshape(1, DIFF_DH).astype(F32),
        lambda_k2[layer].reshape(1, DIFF_DH).astype(F32), batch, seq)

    y = _output(x2, o_a, o_b, mab, w_up_a[layer].astype(BF16), w_up_b[layer].astype(BF16),
                w_out[layer].astype(BF16), ln_g[layer].reshape(1, d).astype(F32),
                ln_b[layer].reshape(1, d).astype(F32))
    return y.reshape(batch, seq, d)
```

```python
import functools
import math

import jax
import jax.numpy as jnp
from jax import lax
from jax.experimental import pallas as pl
from jax.experimental.pallas import tpu as pltpu

F32 = jnp.float32
BF16 = jnp.bfloat16

D_MODEL = 1024
GDN_HEADS = 4
GDN_DK = 128
GDN_DV = 128
GDN_QK = GDN_HEADS * GDN_DK
GDN_WIDTH = GDN_HEADS * GDN_DV
CONV_K = 4
DIFF_HEADS = 4
DIFF_DH = 64
DIFF_DV = 2 * DIFF_DH
DIFF_QK = DIFF_HEADS * 2 * DIFF_DH
DIFF_WIDTH = DIFF_HEADS * DIFF_DV
NORM_EPS = 1e-6
SUBLN_EPS = 1e-5
LN_EPS = 1e-5
DEPTH = 1
DEEPNORM_ALPHA = (2.0 * DEPTH) ** 0.25
LAM_INIT = 0.8 - 0.6 * math.exp(-0.3 * 0)
ALIBI_SLOPES = tuple(2.0 ** (-8.0 * (i + 1) / DIFF_HEADS) for i in range(DIFF_HEADS))
LOG2E = math.log2(math.e)

LANES = 128
CONV_HIST = 8
VMEM_LIMIT = 56 * 1024 * 1024

PROJ_TM = 256
PROJ_STEP_TILES = 1
PROJ_HEAD = 2 * GDN_QK + GDN_WIDTH
GDN_T = 256
GDN_STEP_CHUNKS = 4
GDN_INV_BASE = 16
ATT_TQ = 512
ATT_TK = 512
ATT_ROWS = 256
OUT_TM = 512
OUT_STEP_TILES = 2
OUT_CHUNK = 256
OUT_ROW_GROUPS = (256, 256)
assert sum(OUT_ROW_GROUPS) == OUT_TM
MASK_NEG = -1e30
ATT_SKIP_EXP2 = 150.0
ATT_SKIP_SLACK = 1.0
ATT_FIXED_REF_MAX = 48.0


def _fixed_ref_steps(head):
    slope = ALIBI_SLOPES[head] * LOG2E
    return max(math.ceil(ATT_SKIP_EXP2 / (slope * ATT_TK) + (ATT_TK - 1) / ATT_TK) - 1, 0)


def _sigmoid(x):
    return 1.0 / (1.0 + jnp.exp(-x))


def _silu(x):
    return x * _sigmoid(x)


def _bf16_part(v):
    bits = lax.bitcast_convert_type(v, jnp.uint32) & jnp.uint32(0xFFFF0000)
    return lax.bitcast_convert_type(bits, F32)


def _dot(a, b):
    return jnp.dot(a, b, preferred_element_type=F32)


def _dot_nt(a, b):
    return lax.dot_general(a, b, (((1,), (1,)), ((), ())), preferred_element_type=F32)


def _proj_kernel(tiles_per_seq, x_ref, wh_ref, wt_ref, cw_ref, *rest):
    out_refs, (hist_scr, xb_scr) = rest[:-2], rest[-2:]
    for tile in range(PROJ_STEP_TILES):
        r = pl.ds(tile * PROJ_TM, PROJ_TM)
        _proj_tile(pl.program_id(0) * PROJ_STEP_TILES + tile, tiles_per_seq, x_ref.at[r],
                   wh_ref, wt_ref, cw_ref, *[o.at[r] for o in out_refs], hist_scr, xb_scr)


def _proj_tile(tile_idx, tiles_per_seq, x_ref, wh_ref, wt_ref, cw_ref, gq_ref, gk_ref, gv_ref,
               hab_ref, gz_ref, dq_ref, dk_ref, dv_ref, dz_ref, mab_ref, hist_scr, xb_scr):
    tm = PROJ_TM

    @pl.when(tile_idx % tiles_per_seq == 0)
    def _():
        hist_scr[...] = jnp.zeros_like(hist_scr)

    xb_scr[...] = x_ref[...].astype(BF16)

    def mm(c0, width):
        if c0 < PROJ_HEAD:
            return _dot(xb_scr[...], wh_ref[:, c0:c0 + width])
        return _dot(xb_scr[...], wt_ref[:, c0 - PROJ_HEAD:c0 - PROJ_HEAD + width])

    piece = 2 * GDN_DK

    def conv_silu(c0):
        cols = slice(c0, c0 + piece)
        acc = mm(c0, piece)
        ext = jnp.concatenate([hist_scr[:, cols], acc], axis=0)
        hist_scr[:, cols] = acc[tm - CONV_HIST:]
        y = None
        for j in range(CONV_K):
            r0 = CONV_HIST - (CONV_K - 1) + j
            term = cw_ref[j:j + 1, cols] * ext[r0:r0 + tm]
            y = term if y is None else y + term
        return _silu(y)

    def l2norm_heads(y, scale):
        parts = []
        for h in range(piece // GDN_DK):
            v = y[:, h * GDN_DK:(h + 1) * GDN_DK]
            inv = lax.rsqrt(jnp.sum(v * v, axis=-1, keepdims=True) + NORM_EPS)
            parts.append(v * (inv * scale) if scale != 1.0 else v * inv)
        return jnp.concatenate(parts, axis=1)

    def plain(out_ref, c0, scale=None):
        for j in range(out_ref.shape[1] // 512):
            acc = mm(c0 + j * 512, 512)
            if scale is not None:
                acc = acc * scale
            out_ref[:, j * 512:(j + 1) * 512] = acc.astype(out_ref.dtype)

    for half in range(2):
        c0 = half * piece
        gq_ref[:, c0:c0 + piece] = l2norm_heads(conv_silu(c0), GDN_DK ** -0.5).astype(BF16)
        if half == 0:
            plain(gz_ref, 1536)
        else:
            plain(dq_ref, 2048, DIFF_DH ** -0.5 * LOG2E)
    for half in range(2):
        c0 = half * piece
        gk_ref[:, c0:c0 + piece] = l2norm_heads(conv_silu(GDN_QK + c0), 1.0).astype(BF16)
        plain(dk_ref if half == 0 else dv_ref, 2560 + half * 512)
    for half in range(2):
        c0 = half * piece
        gv_ref[:, c0:c0 + piece] = conv_silu(2 * GDN_QK + c0).astype(BF16)
        if half == 0:
            plain(dz_ref, 3584)
    hab_ref[...] = _dot(xb_scr[...], wh_ref[:, PROJ_HEAD:PROJ_HEAD + LANES])
    plain(mab_ref, 4096)


def _project(x2, w_bf, w_tail, conv_w, seq):
    m = x2.shape[0]
    tm = PROJ_TM * PROJ_STEP_TILES
    row = lambda width: pl.BlockSpec((tm, width), lambda i: (i, 0))
    full = lambda a: pl.BlockSpec(a.shape, lambda i: (0, 0))
    out_shape = (
        jax.ShapeDtypeStruct((m, GDN_QK), BF16),
        jax.ShapeDtypeStruct((m, GDN_QK), BF16),
        jax.ShapeDtypeStruct((m, GDN_WIDTH), BF16),
        jax.ShapeDtypeStruct((m, LANES), F32),
        jax.ShapeDtypeStruct((m, 512), BF16),
        jax.ShapeDtypeStruct((m, 512), BF16),
        jax.ShapeDtypeStruct((m, 512), BF16),
        jax.ShapeDtypeStruct((m, 512), BF16),
        jax.ShapeDtypeStruct((m, 512), BF16),
        jax.ShapeDtypeStruct((m, 2048), BF16),
    )
    return pl.pallas_call(
        functools.partial(_proj_kernel, seq // PROJ_TM),
        grid=(m // tm,),
        in_specs=[row(D_MODEL),
                  pl.BlockSpec((D_MODEL, PROJ_HEAD + LANES), lambda i: (0, 0)),
                  full(w_tail), full(conv_w)],
        out_specs=tuple(row(s.shape[1]) for s in out_shape),
        out_shape=out_shape,
        scratch_shapes=[pltpu.VMEM((CONV_HIST, 3 * GDN_QK), F32),
                        pltpu.VMEM((PROJ_TM, D_MODEL), BF16)],
        compiler_params=pltpu.CompilerParams(
            dimension_semantics=("arbitrary",), vmem_limit_bytes=VMEM_LIMIT),
        name="in_proj",
    )(x2, w_bf, w_tail, conv_w)


def _gdn_kernel(gq_ref, gk_ref, gv_ref, hab_ref, gz_ref, alog_ref, dtb_ref, nw_ref, o_ref,
                st_scr):
    @pl.when(pl.program_id(1) == 0)
    def _():
        st_scr[...] = jnp.zeros_like(st_scr)

    for ck in range(GDN_STEP_CHUNKS):
        r = pl.ds(ck * GDN_T, GDN_T)
        prep = _gdn_prep(gq_ref.at[r], gk_ref.at[r], gv_ref.at[r], hab_ref.at[r], alog_ref,
                         dtb_ref)
        _gdn_apply(prep, gz_ref.at[r], nw_ref, o_ref.at[r], st_scr)


def _gdn_prep(gq_ref, gk_ref, gv_ref, hab_ref, alog_ref, dtb_ref):
    t = GDN_T
    heads = range(GDN_HEADS)

    hab = hab_ref[...]
    xg = hab + dtb_ref[...]
    y_sp = jnp.exp(-jnp.abs(xg))
    u_sp = 1.0 + y_sp
    softplus = jnp.maximum(xg, 0.0) + (jnp.log(u_sp) - ((u_sp - 1.0) - y_sp) / u_sp)
    g_full = -jnp.exp(alog_ref[...]) * softplus
    beta_full = _sigmoid(hab)

    ri = lax.broadcasted_iota(jnp.int32, (t, t), 0)
    ci = lax.broadcasted_iota(jnp.int32, (t, t), 1)
    causal = ri >= ci
    strict = ri > ci
    eye = jnp.where(ri == ci, 1.0, 0.0).astype(F32)

    tri = jnp.where(causal, 1.0, 0.0).astype(BF16)
    g_hi = g_full.astype(BF16)
    g_r1 = g_full - g_hi.astype(F32)
    g_mid = g_r1.astype(BF16)
    g_lo = (g_r1 - g_mid.astype(F32)).astype(BF16)
    gc = _dot(tri, g_hi) + _dot(tri, g_mid) + _dot(tri, g_lo)
    gc_t = gc.T
    eg = jnp.exp(gc)
    g_last = gc[t - 1:t, :]
    k_dec = jnp.exp(g_last - gc)
    eg_last = jnp.exp(g_last)

    qn_b = [gq_ref[:, h * GDN_DK:(h + 1) * GDN_DK] for h in heads]
    kn_b = [gk_ref[:, h * GDN_DK:(h + 1) * GDN_DK] for h in heads]
    qn = [qn_b[h].astype(F32) for h in heads]
    kn = [kn_b[h].astype(F32) for h in heads]
    vv = [gv_ref[:, h * GDN_DV:(h + 1) * GDN_DV].astype(F32) for h in heads]

    bcol = [beta_full[:, GDN_HEADS + h:GDN_HEADS + h + 1] for h in heads]
    egc = [eg[:, h:h + 1] for h in heads]
    kb = [kn[h] * bcol[h] for h in heads]
    kq = [_dot_nt(jnp.concatenate([kb[h].astype(BF16), qn_b[h]], axis=0), kn_b[h])
          for h in heads]
    decay = []
    for h in heads:
        gd = gc[:, h:h + 1] - gc_t[h:h + 1, :]
        decay.append(jnp.where(causal, jnp.exp(jnp.where(causal, gd, 0.0)), 0.0))
    nmat = [jnp.where(strict, -(kq[h][:t] * decay[h]), 0.0) for h in heads]
    amat = [(kq[h][t:] * decay[h]).astype(BF16) for h in heads]
    rhs = [jnp.concatenate([vv[h] * bcol[h], kb[h] * egc[h]], axis=1).astype(BF16)
           for h in heads]
    qe = [qn[h] * egc[h] for h in heads]
    kd_t = [(kn[h] * k_dec[:, h:h + 1]).T.astype(BF16) for h in heads]

    def same_block(b):
        sh = int(math.log2(b))
        return (ri >> sh) == (ci >> sh)

    blk = GDN_INV_BASE
    in_blk = same_block(blk)
    n0 = [jnp.where(in_blk, nmat[h], 0.0) for h in heads]
    x = [eye + n0[h] for h in heads]
    pw = [n0[h].astype(BF16) for h in heads]
    for _ in range(int(math.log2(blk)) - 1):
        p32 = [_dot(pw[h], pw[h]) for h in heads]
        pw = [p32[h].astype(BF16) for h in heads]
        x = [x[h] + _dot(x[h].astype(BF16), pw[h]) for h in heads]
    while blk < t:
        in_big = same_block(2 * blk)
        n_off = [jnp.where(in_big, jnp.where(in_blk, 0.0, nmat[h]), 0.0).astype(BF16)
                 for h in heads]
        xb = [x[h].astype(BF16) for h in heads]
        xn = [_dot(xb[h], n_off[h]).astype(BF16) for h in heads]
        x = [x[h] + _dot(xn[h], xb[h]) for h in heads]
        in_blk = in_big
        blk *= 2
    sol = [_dot(x[h].astype(BF16), rhs[h]) for h in heads]
    return sol, qe, amat, kd_t, eg_last


def _gdn_apply(prep, gz_ref, nw_ref, o_ref, st_scr):
    t = GDN_T
    heads = range(GDN_HEADS)
    sol, qe, amat, kd_t, eg_last = prep
    s_prev = [st_scr[h] for h in heads]
    ws = [_dot(jnp.concatenate([sol[h][:, GDN_DV:], qe[h]], axis=0).astype(BF16),
               s_prev[h].astype(BF16)) for h in heads]
    v_new = [(sol[h][:, :GDN_DV] - ws[h][:t]).astype(BF16) for h in heads]
    o_l = [ws[h][t:] + _dot(amat[h], v_new[h]) for h in heads]
    for h in heads:
        st_scr[h] = s_prev[h] * eg_last[:, h:h + 1] + _dot(kd_t[h], v_new[h])
    nw = nw_ref[...]
    for h in heads:
        o = o_l[h]
        o = o * lax.rsqrt(jnp.mean(o * o, axis=-1, keepdims=True) + NORM_EPS) * nw
        gate = _silu(gz_ref[:, h * GDN_DV:(h + 1) * GDN_DV].astype(F32))
        o_ref[:, h * GDN_DV:(h + 1) * GDN_DV] = (o * gate).astype(o_ref.dtype)


def _gdn(gq, gk, gv, hab, gz, alog_row, dtb_row, norm_w, batch, seq):
    t = GDN_T * GDN_STEP_CHUNKS
    nt = seq // t
    row = lambda width: pl.BlockSpec((t, width), lambda b, s: (b * nt + s, 0))
    full = lambda a: pl.BlockSpec(a.shape, lambda b, s: (0, 0))
    return pl.pallas_call(
        _gdn_kernel,
        grid=(batch, nt),
        in_specs=[row(GDN_QK), row(GDN_QK), row(GDN_WIDTH), row(LANES), row(GDN_WIDTH),
                  full(alog_row), full(dtb_row), full(norm_w)],
        out_specs=row(GDN_WIDTH),
        out_shape=jax.ShapeDtypeStruct((batch * seq, GDN_WIDTH), BF16),
        scratch_shapes=[pltpu.VMEM((GDN_HEADS, GDN_DK, GDN_DV), F32)],
        compiler_params=pltpu.CompilerParams(
            dimension_semantics=("arbitrary", "arbitrary"), vmem_limit_bytes=VMEM_LIMIT),
        name="gated_deltanet",
    )(gq, gk, gv, hab, gz, alog_row, dtb_row, norm_w)


def _attn_kernel(q_ref, qall_ref, k_ref, v_ref, dz_ref, nw_ref, lq1_ref, lk1_ref, lq2_ref,
                 lk2_ref, o_ref, ke_scr, vt_scr, m_scr, acc_scr, l_scr, p_scr, alpha_scr,
                 bound_scr, flag_smem):
    tq, tk, cols = ATT_TQ, ATT_TK, ATT_ROWS
    n_half = tq // cols
    seq = k_ref.shape[0]
    h = pl.program_id(1)
    i = pl.program_id(2)
    slope = jnp.float32(ALIBI_SLOPES[DIFF_HEADS - 1] * LOG2E)
    n_cap = jnp.int32(_fixed_ref_steps(DIFF_HEADS - 1))
    for hh in range(DIFF_HEADS - 1):
        slope = jnp.where(h == hh, jnp.float32(ALIBI_SLOPES[hh] * LOG2E), slope)
        n_cap = jnp.where(h == hh, jnp.int32(_fixed_ref_steps(hh)), n_cap)

    lane = lax.broadcasted_iota(jnp.int32, (tk, LANES), 1)
    feat_lane = lane % DIFF_DH

    def half_norm_max(vf):
        hl = lax.broadcasted_iota(jnp.int32, (LANES, LANES), 0)
        hc = lax.broadcasted_iota(jnp.int32, (LANES, LANES), 1)
        pick = jnp.where(hc == jnp.where(hl < DIFF_DH, 0, 1), 1.0, 0.0).astype(BF16)
        sums = jnp.max(_dot((vf * vf).astype(BF16), pick), axis=0, keepdims=True)
        return jnp.sqrt(jnp.maximum(sums[:, 0:1], sums[:, 1:2])) * (1.0 + 2.0 ** -6)

    @pl.when(i == 0)
    def _():
        bound_scr[...] = jnp.zeros_like(bound_scr)

        def build(blk, carry):
            r0 = pl.multiple_of(blk * tk, tk)
            kf = k_ref[pl.ds(r0, tk), :].astype(F32)
            bound_scr[0:1, :] = jnp.maximum(bound_scr[0:1, :], half_norm_max(kf))
            bound_scr[1:2, :] = jnp.maximum(
                bound_scr[1:2, :], half_norm_max(qall_ref[pl.ds(r0, tk), :].astype(F32)))
            pos = (lax.broadcasted_iota(jnp.int32, (tk, LANES), 0) + r0).astype(F32) * slope
            p_hi = _bf16_part(pos)
            r1 = pos - p_hi
            p_mid = _bf16_part(r1)
            p_lo = r1 - p_mid
            feat = jnp.where(feat_lane == 0, p_hi, jnp.where(feat_lane == 1, p_mid, jnp.where(
                feat_lane == 2, p_lo, jnp.where(feat_lane < 6, 1.0, 0.0))))
            for mp in range(2):
                own = (lane < DIFF_DH) if mp == 0 else (lane >= DIFF_DH)
                ke_scr[mp, pl.ds(r0, tk), :] = jnp.where(own, kf, feat).astype(BF16)
            vt = v_ref[pl.ds(r0, tk), :].astype(F32).T
            vt_scr[blk] = vt.astype(BF16)
            return carry

        lax.fori_loop(0, seq // tk, build, 0, unroll=4)
        qk_all = bound_scr[0:1, :] * bound_scr[1:2, :]
        bound_scr[2:3, :] = qk_all
        flag_smem[0] = jnp.where(jnp.max(qk_all) <= ATT_FIXED_REF_MAX, 1, 0).astype(jnp.int32)

    qf = q_ref[...].astype(F32)
    qk_bound = bound_scr[2:3, 0:1]
    chains = [(mp, r) for mp in range(2) for r in range(n_half)]
    n_chains = len(chains)

    def query_maps(ref_terms):
        feat = jnp.where(feat_lane < 3, 1.0, 0.0)
        if ref_terms is not None:
            r_hi, r_mid, r_lo = ref_terms
            feat = jnp.where(feat_lane == 3, r_hi, jnp.where(
                feat_lane == 4, r_mid, jnp.where(feat_lane == 5, r_lo, feat)))
        return [jnp.where((lane < DIFF_DH) if mp == 0 else (lane >= DIFF_DH), qf, feat)
                .astype(BF16) for mp in range(2)]

    def scores_t(qmaps, c, j, masked):
        mp, r = chains[c]
        nk = (r + 1) * cols if masked else tk
        qc = qmaps[mp][r * cols:(r + 1) * cols]
        kb = ke_scr[mp, pl.ds(pl.multiple_of(j * tk, tk), nk), :]
        st = _dot_nt(kb, qc)
        if masked:
            key_l = lax.broadcasted_iota(jnp.int32, (nk, cols), 0)
            qry_l = lax.broadcasted_iota(jnp.int32, (nk, cols), 1)
            st = jnp.where(key_l <= qry_l + r * cols, st, MASK_NEG)
        return st

    def key_sums(p32):
        nk = p32.shape[0]
        return jnp.sum(p32.reshape(nk // 8, 8, cols), axis=0)

    def store_p(slot, c, p32):
        p = p32.astype(BF16)
        nk = p.shape[0]
        p_scr[slot, c, 0:nk, :] = p
        if nk < tk:
            p_scr[slot, c, nk:, :] = jnp.zeros((tk - nk, cols), BF16)

    def finish(acc_fin):
        def map_out(mp):
            parts = []
            for r in range(n_half):
                c = mp * n_half + r
                denom = jnp.sum(l_scr[c], axis=0, keepdims=True)
                parts.append(acc_fin[c] * (1.0 / denom))
            return jnp.concatenate(parts, axis=1)

        lam = (jnp.exp(jnp.sum(lq1_ref[...] * lk1_ref[...], axis=-1, keepdims=True))
               - jnp.exp(jnp.sum(lq2_ref[...] * lk2_ref[...], axis=-1, keepdims=True))
               + LAM_INIT)
        o_t = map_out(0) - lam * map_out(1)
        o_t = o_t * lax.rsqrt(jnp.mean(o_t * o_t, axis=0, keepdims=True) + SUBLN_EPS)
        o = o_t.T * (nw_ref[...] * (1.0 - LAM_INIT)) * _silu(dz_ref[...].astype(F32))
        o_ref[...] = o.astype(o_ref.dtype)

    def pipeline(scores_step, pv_value, steps_after_diag):
        def scores_block(j, slot, masked=False):
            for c in range(n_chains):
                scores_step(c, j, slot, masked)

        def pv_block(j, slot):
            for c in range(n_chains):
                acc_scr[c] = pv_value(c, j, slot)

        acc_scr[...] = jnp.zeros_like(acc_scr)
        l_scr[...] = jnp.zeros_like(l_scr)
        scores_block(i, 0, masked=True)
        n = steps_after_diag()
        last = i - n

        def pair(u, carry):
            j = i - 2 * u - 1
            scores_block(j, 1)
            pv_block(j + 1, 0)
            scores_block(j - 1, 0)
            pv_block(j, 1)
            return carry

        lax.fori_loop(0, n // 2, pair, 0)

        @pl.when(n % 2 == 1)
        def _():
            scores_block(last, 1)
            pv_block(last + 1, 0)
            finish([pv_value(c, last, 1) for c in range(n_chains)])

        @pl.when(n % 2 == 0)
        def _():
            finish([pv_value(c, last, 0) for c in range(n_chains)])

    use_fixed_ref = flag_smem[0] == 1

    @pl.when(use_fixed_ref)
    def _():
        pos_q = (lax.broadcasted_iota(jnp.int32, (tq, LANES), 0) + i * tq).astype(F32) * slope
        neg_ref = -(qk_bound + ATT_SKIP_SLACK + pos_q)
        r_hi = _bf16_part(neg_ref)
        r1 = neg_ref - r_hi
        r_mid = _bf16_part(r1)
        qmaps = query_maps((r_hi, r_mid, r1 - r_mid))

        def scores_step(c, j, slot, masked):
            p32 = jnp.exp2(scores_t(qmaps, c, j, masked))
            l_scr[c] = l_scr[c] + key_sums(p32)
            store_p(slot, c, p32)

        def pv_value(c, j, slot):
            return acc_scr[c] + _dot(vt_scr[j], p_scr[slot, c])

        pipeline(scores_step, pv_value, lambda: jnp.minimum(i, n_cap))

    @pl.when(jnp.logical_not(use_fixed_ref))
    def _():
        qmaps = query_maps(None)
        m_scr[...] = jnp.full(m_scr.shape, MASK_NEG, F32)

        def scores_step(c, j, slot, masked):
            st = scores_t(qmaps, c, j, masked)
            m_prev = m_scr[c]
            m_new = jnp.maximum(m_prev, jnp.max(st, axis=0, keepdims=True))
            p32 = jnp.exp2(st - m_new)
            alpha = jnp.exp2(m_prev - m_new)
            l_scr[c] = l_scr[c] * alpha + key_sums(p32)
            store_p(slot, c, p32)
            alpha_scr[slot, c] = alpha
            m_scr[c] = m_new

        def pv_value(c, j, slot):
            pv = _dot(vt_scr[j], p_scr[slot, c])
            return acc_scr[c] * alpha_scr[slot, c] + pv

        def steps_after_diag():
            m_min = jnp.min(jnp.min(m_scr[...], axis=0), axis=-1, keepdims=True)
            need = m_min - (ATT_SKIP_EXP2 + ATT_SKIP_SLACK) - qk_bound
            j_first = jnp.ceil(need / (slope * tk) - (tk - 1) / tk)
            n_arr = jnp.clip(i.astype(F32) - jnp.maximum(j_first, 0.0), 0.0, i.astype(F32))
            return jnp.max(n_arr).astype(jnp.int32)

        pipeline(scores_step, pv_value, steps_after_diag)


def _diff_attention(dq, dk, dv, dz, norm_w, lq1, lk1, lq2, lk2, batch, seq):
    tq, tk = ATT_TQ, ATT_TK
    nq = seq // tq
    n_chains = 2 * (tq // ATT_ROWS)
    qspec = pl.BlockSpec((tq, DIFF_DV), lambda b, h, i: (b * nq + i, h))
    kvspec = pl.BlockSpec((seq, DIFF_DV), lambda b, h, i: (b, h))
    full = lambda a: pl.BlockSpec(a.shape, lambda b, h, i: (0, 0))
    return pl.pallas_call(
        _attn_kernel,
        grid=(batch, DIFF_HEADS, nq),
        in_specs=[qspec, kvspec, kvspec, kvspec, qspec, full(norm_w), full(lq1), full(lk1),
                  full(lq2), full(lk2)],
        out_specs=qspec,
        out_shape=jax.ShapeDtypeStruct((batch * seq, DIFF_WIDTH), BF16),
        scratch_shapes=[pltpu.VMEM((2, seq, DIFF_DV), BF16),
                        pltpu.VMEM((seq // tk, DIFF_DV, tk), BF16),
                        pltpu.VMEM((n_chains, 1, ATT_ROWS), F32),
                        pltpu.VMEM((n_chains, DIFF_DV, ATT_ROWS), F32),
                        pltpu.VMEM((n_chains, 8, ATT_ROWS), F32),
                        pltpu.VMEM((2, n_chains, tk, ATT_ROWS), BF16),
                        pltpu.VMEM((2, n_chains, 1, ATT_ROWS), F32),
                        pltpu.VMEM((8, LANES), F32),
                        pltpu.SMEM((1,), jnp.int32)],
        compiler_params=pltpu.CompilerParams(
            dimension_semantics=("arbitrary", "arbitrary", "arbitrary"),
            vmem_limit_bytes=VMEM_LIMIT),
        name="diff_attention",
    )(dq, dq, dk, dv, dz, norm_w, lq1, lk1, lq2, lk2)


def _out_kernel(x_ref, oa_ref, ob_ref, mab_ref, wa_ref, wb_ref, wo_ref, g_ref, b_ref, y_ref,
                merged_scr):
    for tile in range(OUT_STEP_TILES):
        r = pl.ds(tile * OUT_TM, OUT_TM)
        _out_tile(x_ref.at[r], oa_ref.at[r], ob_ref.at[r], mab_ref.at[r], wa_ref, wb_ref,
                  wo_ref, g_ref, b_ref, y_ref.at[r], merged_scr)


def _out_tile(x_ref, oa_ref, ob_ref, mab_ref, wa_ref, wb_ref, wo_ref, g_ref, b_ref, y_ref,
              merged_scr):
    for c0 in range(0, D_MODEL, OUT_CHUNK):
        cs = slice(c0, c0 + OUT_CHUNK)
        ya = _dot(oa_ref[...], wa_ref[:, cs])
        yb = _dot(ob_ref[...], wb_ref[:, cs])
        ga = _sigmoid(mab_ref[:, c0:c0 + OUT_CHUNK].astype(F32))
        gb = _sigmoid(mab_ref[:, D_MODEL + c0:D_MODEL + c0 + OUT_CHUNK].astype(F32))
        merged_scr[:, cs] = (ga * ya + gb * yb).astype(BF16)
    r0 = 0
    for nrows in OUT_ROW_GROUPS:
        rs = slice(r0, r0 + nrows)
        r0 += nrows
        y = _dot(merged_scr[rs, :], wo_ref[...])
        z = DEEPNORM_ALPHA * x_ref[rs, :] + y
        mu = jnp.mean(z, axis=-1, keepdims=True)
        zc = z - mu
        var = jnp.mean(zc * zc, axis=-1, keepdims=True)
        y_ref[rs, :] = zc * lax.rsqrt(var + LN_EPS) * g_ref[...] + b_ref[...]


def _output(x2, o_a, o_b, mab, wa, wb, wo, ln_g, ln_b):
    m = x2.shape[0]
    tm = OUT_TM * OUT_STEP_TILES
    row = lambda width: pl.BlockSpec((tm, width), lambda i: (i, 0))
    full = lambda a: pl.BlockSpec(a.shape, lambda i: (0, 0))
    return pl.pallas_call(
        _out_kernel,
        grid=(m // tm,),
        in_specs=[row(D_MODEL), row(GDN_WIDTH), row(DIFF_WIDTH), row(2 * D_MODEL),
                  full(wa), full(wb), full(wo), full(ln_g), full(ln_b)],
        out_specs=row(D_MODEL),
        out_shape=jax.ShapeDtypeStruct((m, D_MODEL), F32),
        scratch_shapes=[pltpu.VMEM((OUT_TM, D_MODEL), BF16)],
        compiler_params=pltpu.CompilerParams(
            dimension_semantics=("arbitrary",), vmem_limit_bytes=VMEM_LIMIT),
        name="merge_out_ln",
    )(x2, o_a, o_b, mab, wa, wb, wo, ln_g, ln_b)


def _lane_row(v):
    return jnp.zeros((1, LANES), F32).at[0, :v.shape[0]].set(v.astype(F32))


def kernel(x, w_in, conv_w, a_log, dt_bias, gdn_norm_w, w_up_a, lambda_q1, lambda_k1,
           lambda_q2, lambda_k2, diff_norm_w, w_up_b, w_out, ln_g, ln_b):
    batch, seq, d = x.shape
    x2 = x.reshape(batch * seq, d)
    layer = 0
    w = w_in[layer].astype(BF16)
    w_tail = w[:, PROJ_HEAD + 2 * GDN_HEADS:]

    gq, gk, gv, hab, gz, dq, dk, dv, dz, mab = _project(
        x2, w, w_tail, conv_w[layer].astype(F32), seq)

    o_a = _gdn(gq, gk, gv, hab, gz, _lane_row(a_log[layer]), _lane_row(dt_bias[layer]),
               gdn_norm_w[layer].reshape(1, GDN_DV).astype(F32), batch, seq)

    o_b = _diff_attention(
        dq, dk, dv, dz, diff_norm_w[layer].reshape(1, DIFF_DV).astype(F32),
        lambda_q1[layer].reshape(1, DIFF_DH).astype(F32),
        lambda_k1[layer].reshape(1, DIFF_DH).astype(F32),
        lambda_q2[layer].reshape(1, DIFF_DH).astype(F32),
        lambda_k2[layer].reshape(1, DIFF_DH).astype(F32), batch, seq)

    y = _output(x2, o_a, o_b, mab, w_up_a[layer].astype(BF16), w_up_b[layer].astype(BF16),
                w_out[layer].astype(BF16), ln_g[layer].reshape(1, d).astype(F32),
                ln_b[layer].reshape(1, d).astype(F32))
    return y.reshape(batch, seq, d)
```
